```python
import math
import jax
import jax.numpy as jnp
from jax import lax
import numpy as np

D_MODEL = 1024
BATCH = 32
SEQ = 2048
DEPTH = 4

GRID_W = 64
CTX_LEN = 256
N_MIXERS = 3
EXPAND = 2
E_WIDTH = EXPAND * D_MODEL
HEAD_DIM = 128
N_Q_HEADS = E_WIDTH // HEAD_DIM
N_KV_HEADS = max(1, N_Q_HEADS // 4)
Q_PER_KV = N_Q_HEADS // N_KV_HEADS
KV_WIDTH = N_KV_HEADS * HEAD_DIM
WINDOW = 128
ATTN_BLOCK = 128
ROPE_THETA = 10000.0
S5_GROUP = 16
S5_GROUPS = E_WIDTH // S5_GROUP
S5_STATE = 64
S5_CHUNK = 128
HG_HEAD_DIM = 128
HG_HEADS = E_WIDTH // HG_HEAD_DIM
HG_CHUNK = 64
ATTN_IN = 2 * E_WIDTH + 2 * KV_WIDTH
S5_IN = 2 * E_WIDTH
HG_IN = 5 * E_WIDTH
NORM_EPS = 1e-5
NEG_INF = -1e30

kernel_name = 'hybrid_diffusion_gqa_s5_hgrn2'


def layer_norm(x, g, b):
    xf = x.astype(jnp.float32)
    mu = jnp.mean(xf, -1, keepdims=True)
    var = jnp.mean(jnp.square(xf - mu), -1, keepdims=True)
    y = (xf - mu) * lax.rsqrt(var + NORM_EPS) * g.astype(jnp.float32) + b.astype(jnp.float32)
    return y.astype(x.dtype)


def grid_positions(n_tokens):
    rows = n_tokens // GRID_W
    row = jnp.repeat(jnp.arange(rows, dtype=jnp.float32), GRID_W)
    col = jnp.tile(jnp.arange(GRID_W, dtype=jnp.float32), rows)
    return row, col


def rope_2d(x, row, col):
    half = x.shape[-1] // 2
    nf = half // 2
    inv_freq = jnp.power(ROPE_THETA, -jnp.arange(nf, dtype=jnp.float32) / nf)
    xf = x.astype(jnp.float32)

    def rotate(xp, pos):
        ang = pos[:, None] * inv_freq[None, :]
        cos = jnp.cos(ang)[None, :, None, :]
        sin = jnp.sin(ang)[None, :, None, :]
        x1, x2 = xp[..., :nf], xp[..., nf:]
        return jnp.concatenate([x1 * cos - x2 * sin, x2 * cos + x1 * sin], -1)

    return jnp.concatenate([rotate(xf[..., :half], row), rotate(xf[..., half:], col)], -1).astype(x.dtype)


def softmax_with_sink(s, sink):
    sk = jnp.broadcast_to(sink[None, :, :, None, None], s.shape[:-1] + (1,))
    return jax.nn.softmax(jnp.concatenate([s, sk], -1), axis=-1)[..., :-1]


def attention_mixer(p, pc, sink, row, col, ctx_out):
    bsz, n_lat, _ = p.shape
    n_ctx = pc.shape[1]

    def split(t):
        b_, l_ = t.shape[0], t.shape[1]
        q = t[..., :E_WIDTH].reshape(b_, l_, N_Q_HEADS, HEAD_DIM)
        k = t[..., E_WIDTH:E_WIDTH + KV_WIDTH].reshape(b_, l_, N_KV_HEADS, HEAD_DIM)
        v = t[..., E_WIDTH + KV_WIDTH:].reshape(b_, l_, N_KV_HEADS, HEAD_DIM)
        return q, k, v

    q, k, v = split(p)
    qc, kc, vc = split(pc)
    q = rope_2d(q, row, col).reshape(bsz, n_lat, N_KV_HEADS, Q_PER_KV, HEAD_DIM)
    k = rope_2d(k, row, col)
    qc = qc.reshape(bsz, n_ctx, N_KV_HEADS, Q_PER_KV, HEAD_DIM)
    scale = HEAD_DIM ** -0.5
    sink_r = sink.astype(jnp.float32).reshape(N_KV_HEADS, Q_PER_KV)

    n_blk = n_lat // ATTN_BLOCK
    n_key = ATTN_BLOCK + 2 * WINDOW
    qb = q.reshape(bsz, n_blk, ATTN_BLOCK, N_KV_HEADS, Q_PER_KV, HEAD_DIM).swapaxes(0, 1)
    pad = ((0, 0), (WINDOW, WINDOW), (0, 0), (0, 0))
    kp = jnp.pad(k, pad)
    vp = jnp.pad(v, pad)
    q_off = jnp.arange(ATTN_BLOCK)
    k_off = jnp.arange(n_key) - WINDOW

    def block(args):
        j, qj = args
        start = j * ATTN_BLOCK
        kj = lax.dynamic_slice_in_dim(kp, start, n_key, axis=1)
        vj = lax.dynamic_slice_in_dim(vp, start, n_key, axis=1)
        qpos = start + q_off
        kpos = start + k_off
        valid = ((jnp.abs(qpos[:, None] - kpos[None, :]) <= WINDOW)
                 & (kpos >= 0)[None, :] & (kpos < n_lat)[None, :])
        s_win = jnp.einsum('bqhgd,bkhd->bhgqk', qj, kj, preferred_element_type=jnp.float32) * scale
        s_win = jnp.where(valid, s_win, NEG_INF)
        s_ctx = jnp.einsum('bqhgd,bchd->bhgqc', qj, kc, preferred_element_type=jnp.float32) * scale
        pr = softmax_with_sink(jnp.concatenate([s_win, s_ctx], -1), sink_r).astype(vj.dtype)
        return (jnp.einsum('bhgqk,bkhd->bqhgd', pr[..., :n_key], vj)
                + jnp.einsum('bhgqc,bchd->bqhgd', pr[..., n_key:], vc))

    o = lax.map(block, (jnp.arange(n_blk), qb)).swapaxes(0, 1).reshape(bsz, n_lat, E_WIDTH)
    if not ctx_out:
        return o, None
    sc = jnp.einsum('bqhgd,bkhd->bhgqk', qc, kc, preferred_element_type=jnp.float32) * scale
    prc = softmax_with_sink(sc, sink_r).astype(vc.dtype)
    oc = jnp.einsum('bhgqk,bkhd->bqhgd', prc, vc).reshape(bsz, n_ctx, E_WIDTH)
    return o, oc


def flip_seq(t, rev):
    return jnp.flip(t, axis=1) if rev else t


def _linrec_combine(e1, e2):
    a1, b1 = e1
    a2, b2 = e2
    return a1 * a2, a2 * b1 + b2


def s5_discretise(lam_re, lam_im, log_step, b_re, b_im, c_re, c_im):
    lam = lax.complex(lam_re.astype(jnp.float32), lam_im.astype(jnp.float32))
    dt = jnp.exp(log_step.astype(jnp.float32))[:, None]
    abar = jnp.exp(lam * dt)
    bmat = lax.complex(b_re.astype(jnp.float32), b_im.astype(jnp.float32))
    bbar = ((abar - 1.0) / lam)[..., None] * bmat
    cmat = lax.complex(c_re.astype(jnp.float32), c_im.astype(jnp.float32))
    return abar, bbar, cmat


def s5_scan(u, abar, bbar, cmat, h0):
    bsz, n_tok, n_g, n_c = u.shape
    n_chunk = n_tok // S5_CHUNK
    uc = u.reshape(bsz, n_chunk, S5_CHUNK, n_g, n_c).swapaxes(0, 1)

    def step(h, u_blk):
        bu = jnp.einsum('btgn,gpn->btgp', u_blk.astype(jnp.complex64), bbar)
        bu = bu.at[:, 0].add(abar[None] * h)
        a = jnp.broadcast_to(abar, bu.shape)
        _, hs = lax.associative_scan(_linrec_combine, (a, bu), axis=1)
        y = jnp.einsum('btgp,gnp->btgn', hs, cmat).real
        return hs[:, -1], y

    h_last, ys = lax.scan(step, h0, uc)
    return ys.swapaxes(0, 1).reshape(bsz, n_tok, n_g, n_c), h_last


def s5_mixer(p, pc, lam_re, lam_im, log_step, b_re, b_im, c_re, c_im, d_skip, glu_w, glu_b):
    bsz, n_lat, _ = p.shape
    n_ctx = pc.shape[1]
    u = p.astype(jnp.float32).reshape(bsz, n_lat, S5_GROUPS, S5_GROUP)
    uc = pc.astype(jnp.float32).reshape(bsz, n_ctx, S5_GROUPS, S5_GROUP)
    d = d_skip.astype(jnp.float32).reshape(S5_GROUPS, S5_GROUP)
    y = d * u
    yc = d * uc
    for rev in (0, 1):
        abar, bbar, cmat = s5_discretise(lam_re[rev], lam_im[rev], log_step[rev],
                                         b_re[rev], b_im[rev], c_re[rev], c_im[rev])
        h0 = jnp.zeros((bsz, S5_GROUPS, S5_STATE), jnp.complex64)
        yc_dir, h_ctx = s5_scan(flip_seq(uc, rev), abar, bbar, cmat, h0)
        y_dir, _ = s5_scan(flip_seq(u, rev), abar, bbar, cmat, h_ctx)
        y = y + flip_seq(y_dir, rev)
        yc = yc + flip_seq(yc_dir, rev)
    w = glu_w.astype(jnp.float32)
    bias = glu_b.astype(jnp.float32)

    def glu(t, n):
        g = jax.nn.gelu(t.reshape(bsz, n, E_WIDTH))
        return (g * jax.nn.sigmoid(g @ w + bias)).astype(p.dtype)

    return glu(y, n_lat), glu(yc, n_ctx)


def hgrn2_scan(q, k, v, logf, s0):
    bsz, n_tok, n_h, _ = q.shape
    n_chunk = n_tok // HG_CHUNK
    mid = HG_CHUNK // 2
    order_mask = jnp.tril(jnp.ones((HG_CHUNK, HG_CHUNK), dtype=bool))

    def blocks(t):
        return t.reshape(bsz, n_chunk, HG_CHUNK, n_h, t.shape[-1]).transpose(1, 0, 3, 2, 4)

    def step(s, blk):
        qb, kb, vb, gb = blk
        b = jnp.cumsum(gb, axis=2)
        ref = b[:, :, mid:mid + 1]
        att = jnp.einsum('bhtk,bhsk->bhts', qb * jnp.exp(b - ref), kb * jnp.exp(ref - b))
        att = jnp.where(order_mask, att, 0.0)
        o = (jnp.einsum('bhts,bhsv->bhtv', att, vb)
             + jnp.einsum('bhtk,bhkv->bhtv', qb * jnp.exp(b), s))
        b_last = b[:, :, -1:]
        s = (jnp.exp(b_last[:, :, 0])[..., None] * s
             + jnp.einsum('bhsk,bhsv->bhkv', kb * jnp.exp(b_last - b), vb))
        return s, o

    s_last, outs = lax.scan(step, s0, (blocks(q), blocks(k), blocks(v), blocks(logf)))
    return outs.transpose(1, 0, 3, 2, 4).reshape(bsz, n_tok, n_h, v.shape[-1]), s_last


def hgrn2_mixer(p, pc, lb, norm_g):
    bsz, n_lat, _ = p.shape
    n_ctx = pc.shape[1]
    lbh = lb.astype(jnp.float32).reshape(HG_HEADS, HG_HEAD_DIM)

    def heads(t):
        t = t.astype(jnp.float32).reshape(t.shape[0], t.shape[1], 4, HG_HEADS, HG_HEAD_DIM)
        return t[:, :, 0], t[:, :, 1], t[:, :, 2], t[:, :, 3]

    q, f_fw, f_bw, v = heads(p)
    qc, fc_fw, fc_bw, vc = heads(pc)
    outs, outs_c = [], []
    for rev, (fl, flc) in enumerate(((f_fw, fc_fw), (f_bw, fc_bw))):
        f = lbh + (1.0 - lbh) * jax.nn.sigmoid(fl)
        fc = lbh + (1.0 - lbh) * jax.nn.sigmoid(flc)
        s0 = jnp.zeros((bsz, HG_HEADS, HG_HEAD_DIM, HG_HEAD_DIM), jnp.float32)
        oc_dir, s_ctx = hgrn2_scan(flip_seq(qc, rev), flip_seq(1.0 - fc, rev),
                                   flip_seq(vc, rev), flip_seq(jnp.log(fc), rev), s0)
        o_dir, _ = hgrn2_scan(flip_seq(q, rev), flip_seq(1.0 - f, rev),
                              flip_seq(v, rev), flip_seq(jnp.log(f), rev), s_ctx)
        outs.append(flip_seq(o_dir, rev))
        outs_c.append(flip_seq(oc_dir, rev))
    g = norm_g.astype(jnp.float32).reshape(HG_HEADS, HG_HEAD_DIM)

    def head_norm(t, n):
        t = t * lax.rsqrt(jnp.mean(jnp.square(t), -1, keepdims=True) + NORM_EPS) * g
        return t.reshape(bsz, n, E_WIDTH).astype(p.dtype)

    return head_norm(outs[0] + outs[1], n_lat), head_norm(outs_c[0] + outs_c[1], n_ctx)


def setup_inputs(seed: int = 0) -> dict:
    key = jax.random.key(seed)
    ks = jax.random.split(key, 25)
    f32 = jnp.float32
    n_attn = len(range(0, DEPTH, N_MIXERS))
    n_s5 = len(range(1, DEPTH, N_MIXERS))
    n_hg = len(range(2, DEPTH, N_MIXERS))
    beta = (8.0 * DEPTH) ** -0.25

    def nrm(k, shape, std):
        return std * jax.random.normal(k, shape, f32)

    s5_shape = (n_s5, 2, S5_GROUPS, S5_STATE)
    lam_im_init = math.pi * jnp.arange(S5_STATE, dtype=f32)
    return {
        'x': nrm(ks[0], (BATCH, SEQ, D_MODEL), 1.0),
        'c': nrm(ks[1], (BATCH, D_MODEL), 1.0),
        'ctx': nrm(ks[2], (BATCH, CTX_LEN, D_MODEL), 1.0),
        'c_ctx': nrm(ks[3], (D_MODEL,), 1.0),
        'ada_w': nrm(ks[4], (DEPTH, D_MODEL, 3 * D_MODEL), 0.5 * D_MODEL ** -0.5),
        'ada_b': nrm(ks[5], (DEPTH, 3 * D_MODEL), 0.02),
        'ln_g': 1.0 + nrm(ks[6], (DEPTH, D_MODEL), 0.02),
        'ln_b': nrm(ks[7], (DEPTH, D_MODEL), 0.02),
        'w_out': nrm(ks[8], (DEPTH, E_WIDTH, D_MODEL), beta * E_WIDTH ** -0.5),
        'attn_w_in': nrm(ks[9], (n_attn, D_MODEL, ATTN_IN), D_MODEL ** -0.5),
        'attn_sink': nrm(ks[10], (n_attn, N_Q_HEADS), 0.5),
        's5_w_in': nrm(ks[11], (n_s5, D_MODEL, S5_IN), D_MODEL ** -0.5),
        's5_lam_re': -0.5 + nrm(ks[12], s5_shape, 0.01),
        's5_lam_im': lam_im_init + nrm(ks[13], s5_shape, 0.01),
        's5_log_step': jax.random.uniform(ks[14], (n_s5, 2, S5_GROUPS), f32, math.log(1e-3), math.log(1e-1)),
        's5_b_re': nrm(ks[15], (n_s5, 2, S5_GROUPS, S5_STATE, S5_GROUP), (2 * S5_GROUP) ** -0.5),
        's5_b_im': nrm(ks[16], (n_s5, 2, S5_GROUPS, S5_STATE, S5_GROUP), (2 * S5_GROUP) ** -0.5),
        's5_c_re': nrm(ks[17], (n_s5, 2, S5_GROUPS, S5_GROUP, S5_STATE), S5_STATE ** -0.5),
        's5_c_im': nrm(ks[18], (n_s5, 2, S5_GROUPS, S5_GROUP, S5_STATE), S5_STATE ** -0.5),
        's5_d': nrm(ks[19], (n_s5, E_WIDTH), 0.5),
        's5_glu_w': nrm(ks[20], (n_s5, E_WIDTH, E_WIDTH), E_WIDTH ** -0.5),
        's5_glu_b': nrm(ks[21], (n_s5, E_WIDTH), 0.02),
        'hg_w_in': nrm(ks[22], (n_hg, D_MODEL, HG_IN), D_MODEL ** -0.5),
        'hg_lb': nrm(ks[23], (DEPTH, E_WIDTH), 0.1),
        'hg_norm_g': 1.0 + nrm(ks[24], (n_hg, E_WIDTH), 0.02),
    }


def reference(x, c, ctx, c_ctx, ada_w, ada_b, ln_g, ln_b, w_out, attn_w_in, attn_sink,
              s5_w_in, s5_lam_re, s5_lam_im, s5_log_step, s5_b_re, s5_b_im, s5_c_re, s5_c_im,
              s5_d, s5_glu_w, s5_glu_b, hg_w_in, hg_lb, hg_norm_g):
    n_lat = x.shape[1]
    row, col = grid_positions(n_lat)
    alpha = (2.0 * DEPTH) ** 0.25
    lb_w = jax.nn.softmax(hg_lb.astype(jnp.float32), axis=0)
    lb_all = jnp.cumsum(lb_w, axis=0) - lb_w[0:1]
    xc = ctx
    for i in range(DEPTH):
        kind, j = i % N_MIXERS, i // N_MIXERS
        last = i == DEPTH - 1
        mod = jax.nn.silu(c) @ ada_w[i] + ada_b[i]
        mod_c = jax.nn.silu(c_ctx) @ ada_w[i] + ada_b[i]
        shift, scale, gate = jnp.split(mod[:, None, :], 3, axis=-1)
        shift_c, scale_c, gate_c = jnp.split(mod_c, 3, axis=-1)
        h = x * (1.0 + scale) + shift
        hc = xc * (1.0 + scale_c) + shift_c
        w_in = (attn_w_in, s5_w_in, hg_w_in)[kind][j]
        pz = h @ w_in
        pzc = hc @ w_in
        p, z = pz[..., :-E_WIDTH], pz[..., -E_WIDTH:]
        pc, zc = pzc[..., :-E_WIDTH], pzc[..., -E_WIDTH:]
        if kind == 0:
            y, yc = attention_mixer(p, pc, attn_sink[j], row, col, not last)
        elif kind == 1:
            y, yc = s5_mixer(p, pc, s5_lam_re[j], s5_lam_im[j], s5_log_step[j], s5_b_re[j], s5_b_im[j],
                             s5_c_re[j], s5_c_im[j], s5_d[j], s5_glu_w[j], s5_glu_b[j])
        else:
            y, yc = hgrn2_mixer(p, pc, lb_all[i], hg_norm_g[j])
        x = layer_norm(alpha * x + gate * ((y * jax.nn.silu(z)) @ w_out[i]), ln_g[i], ln_b[i])
        if not last:
            xc = layer_norm(alpha * xc + gate_c * ((yc * jax.nn.silu(zc)) @ w_out[i]), ln_g[i], ln_b[i])
    return x
```

```python
import functools
import math

import jax
import jax.numpy as jnp
from jax import lax
from jax.experimental import pallas as pl
from jax.experimental.pallas import tpu as pltpu

F32 = jnp.float32
BF16 = jnp.bfloat16

N_MIXERS = 3
HEAD_DIM = 128
Q_PER_KV = 4
WINDOW = 128
ATTN_BLOCK = 128
GRID_W = 64
ROPE_THETA = 10000.0
S5_GROUP = 16
S5_STATE = 64
S5_CHUNK = 16
HG_HEAD = 128
HG_CHUNK = 64
NORM_EPS = 1e-5
NEG_INF = -1e30
MOD_ROWS_PAD = 8
VMEM_LIMIT = 56 * 1024 * 1024


def _cparams(n_axes):
    return pltpu.CompilerParams(dimension_semantics=("arbitrary",) * n_axes,
                                vmem_limit_bytes=VMEM_LIMIT)


def _sigmoid(x):
    return 1.0 / (1.0 + jnp.exp(-x))


def _ada_kernel(c_ref, w_ref, b_ref, o_ref):
    cv = c_ref[...]
    s = cv * _sigmoid(cv)
    o_ref[0] = jnp.dot(s, w_ref[0], preferred_element_type=F32,
                       precision=lax.Precision.HIGHEST) + b_ref[0]


def _ada(c_all, ada_w, ada_b):
    depth, d, d3 = ada_w.shape
    rows = c_all.shape[0]
    return pl.pallas_call(
        _ada_kernel,
        grid=(depth, d3 // d),
        in_specs=[pl.BlockSpec((rows, d), lambda i, j: (0, 0)),
                  pl.BlockSpec((1, d, d), lambda i, j: (i, 0, j)),
                  pl.BlockSpec((1, 1, d), lambda i, j: (i, 0, j))],
        out_specs=pl.BlockSpec((1, rows, d), lambda i, j: (i, 0, j)),
        out_shape=jax.ShapeDtypeStruct((depth, rows, d3), F32),
        compiler_params=_cparams(2),
    )(c_all, ada_w, ada_b.reshape(depth, 1, d3))


def _inproj_kernel(x_ref, mb_ref, mc_ref, w_ref, o_ref, h_ref, *, n_ctx, d):
    @pl.when(pl.program_id(1) == 0)
    def _():
        mb = mb_ref[0]
        mc = mc_ref[0]
        h_ref[:n_ctx] = (x_ref[0, :n_ctx] * (1.0 + mc[:, d:2 * d]) + mc[:, :d]).astype(BF16)
        h_ref[n_ctx:] = (x_ref[0, n_ctx:] * (1.0 + mb[:, d:2 * d]) + mb[:, :d]).astype(BF16)

    o_ref[0] = jnp.dot(h_ref[...], w_ref[...], preferred_element_type=F32).astype(o_ref.dtype)


def _inproj(xa, mod3, layer, mod_rows, w_bf, n_ctx, tn):
    bsz, t, d = xa.shape
    n = w_bf.shape[1]
    base = layer * mod_rows
    return pl.pallas_call(
        functools.partial(_inproj_kernel, n_ctx=n_ctx, d=d),
        grid=(bsz, n // tn),
        in_specs=[pl.BlockSpec((1, t, d), lambda b, j: (b, 0, 0)),
                  pl.BlockSpec((1, 1, 3 * d), lambda b, j: (base + b, 0, 0)),
                  pl.BlockSpec((1, 1, 3 * d), lambda b, j: (base + bsz, 0, 0)),
                  pl.BlockSpec((d, tn), lambda b, j: (0, j))],
        out_specs=pl.BlockSpec((1, t, tn), lambda b, j: (b, 0, j)),
        out_shape=jax.ShapeDtypeStruct((bsz, t, n), BF16),
        scratch_shapes=[pltpu.VMEM((t, d), BF16)],
        compiler_params=_cparams(2),
    )(xa, mod3, mod3, w_bf)


def _outproj_kernel(y_ref, z_ref, x_ref, mb_ref, mc_ref, w_ref, g_ref, b_ref, o_ref,
                    *, n_ctx, d, alpha, tm):
    z = z_ref[0].astype(F32)
    a = (y_ref[0].astype(F32) * (z * _sigmoid(z))).astype(BF16)
    br = jnp.dot(a, w_ref[...], preferred_element_type=F32)
    rows = pl.program_id(1) * tm + lax.broadcasted_iota(jnp.int32, (tm, 1), 0)
    gate = jnp.where(rows < n_ctx, mc_ref[0][:, 2 * d:], mb_ref[0][:, 2 * d:])
    v = alpha * x_ref[0] + gate * br
    mu = jnp.mean(v, axis=-1, keepdims=True)
    vc = v - mu
    var = jnp.mean(vc * vc, axis=-1, keepdims=True)
    o_ref[0] = vc * lax.rsqrt(var + NORM_EPS) * g_ref[...] + b_ref[...]


def _outproj(y, pz, xa, mod3, layer, mod_rows, w_bf, ln_g, ln_b, n_ctx, alpha, tm):
    bsz, t, d = xa.shape
    e = w_bf.shape[0]
    zblk = 0
    base = layer * mod_rows
    return pl.pallas_call(
        functools.partial(_outproj_kernel, n_ctx=n_ctx, d=d, alpha=alpha, tm=tm),
        grid=(bsz, t // tm),
        in_specs=[pl.BlockSpec((1, tm, e), lambda b, i: (b, i, 0)),
                  pl.BlockSpec((1, tm, e), lambda b, i: (b, i, zblk)),
                  pl.BlockSpec((1, tm, d), lambda b, i: (b, i, 0)),
                  pl.BlockSpec((1, 1, 3 * d), lambda b, i: (base + b, 0, 0)),
                  pl.BlockSpec((1, 1, 3 * d), lambda b, i: (base + bsz, 0, 0)),
                  pl.BlockSpec((e, d), lambda b, i: (0, 0)),
                  pl.BlockSpec((1, d), lambda b, i: (0, 0)),
                  pl.BlockSpec((1, d), lambda b, i: (0, 0))],
        out_specs=pl.BlockSpec((1, tm, d), lambda b, i: (b, i, 0)),
        out_shape=jax.ShapeDtypeStruct((bsz, t, d), F32),
        compiler_params=_cparams(2),
    )(y, pz, xa, mod3, mod3, w_bf, ln_g.reshape(1, d), ln_b.reshape(1, d))


def _rope_tables(n_lat):
    pos = jnp.arange(n_lat, dtype=jnp.int32)
    row = (pos // GRID_W).astype(F32)
    col = (pos % GRID_W).astype(F32)
    nf = HEAD_DIM // 4
    inv_freq = jnp.power(ROPE_THETA, -jnp.arange(nf, dtype=F32) / nf)
    ang_r = row[:, None] * inv_freq[None, :]
    ang_c = col[:, None] * inv_freq[None, :]
    zeros = jnp.zeros_like(ang_r)
    cos = jnp.concatenate([jnp.cos(ang_r)] * 2 + [jnp.cos(ang_c)] * 2, axis=-1)
    s1 = jnp.concatenate([-jnp.sin(ang_r), zeros, -jnp.sin(ang_c), zeros], axis=-1)
    s2 = jnp.concatenate([zeros, jnp.sin(ang_r), zeros, jnp.sin(ang_c)], axis=-1)
    return cos, s1, s2


def _rope(x, cos, s1, s2):
    quarter = HEAD_DIM // 4
    return (x * cos + pltpu.roll(x, HEAD_DIM - quarter, 1) * s1
            + pltpu.roll(x, quarter, 1) * s2)


def _attn_kernel(sink_ref, q_ref, kv_ref, cq_ref, s1q_ref, s2q_ref, ck_ref, s1k_ref, s2k_ref,
                 o_ref, kr_ref, *, n_ctx, n_lat, n_kv):
    j = pl.program_id(1)
    nb_ctx = n_ctx // ATTN_BLOCK
    kvw = n_kv * HEAD_DIM
    n_win = ATTN_BLOCK + 2 * WINDOW
    scale = HEAD_DIM ** -0.5
    rows_q = Q_PER_KV * ATTN_BLOCK
    nt = (((1,), (1,)), ((), ()))

    @pl.when(j == 0)
    def _():
        for h in range(n_kv):
            sl = slice(h * HEAD_DIM, (h + 1) * HEAD_DIM)
            k = kv_ref[0, n_ctx:, sl].astype(F32)
            kr_ref[:, sl] = _rope(k, ck_ref[...], s1k_ref[...], s2k_ref[...]).astype(BF16)

    def sink_col(h):
        return jnp.concatenate(
            [jnp.full((ATTN_BLOCK, 1), sink_ref[h * Q_PER_KV + g], F32) for g in range(Q_PER_KV)], axis=0)

    def finish(h, parts, values):
        sk = sink_col(h)
        m = sk
        for s in parts:
            m = jnp.maximum(m, jnp.max(s, axis=-1, keepdims=True))
        l = jnp.exp(sk - m)
        o = jnp.zeros((rows_q, HEAD_DIM), F32)
        for s, v in zip(parts, values):
            p = jnp.exp(s - m)
            l = l + jnp.sum(p, axis=-1, keepdims=True)
            o = o + jnp.dot(p.astype(BF16), v, preferred_element_type=F32)
        o = o / l
        for g in range(Q_PER_KV):
            hq = h * Q_PER_KV + g
            o_ref[0, :, hq * HEAD_DIM:(hq + 1) * HEAD_DIM] = (
                o[g * ATTN_BLOCK:(g + 1) * ATTN_BLOCK].astype(o_ref.dtype))

    @pl.when(j < nb_ctx)
    def _():
        for h in range(n_kv):
            sl = slice(h * HEAD_DIM, (h + 1) * HEAD_DIM)
            vsl = slice(kvw + h * HEAD_DIM, kvw + (h + 1) * HEAD_DIM)
            qh = jnp.concatenate(
                [q_ref[0, :, (h * Q_PER_KV + g) * HEAD_DIM:(h * Q_PER_KV + g + 1) * HEAD_DIM]
                 for g in range(Q_PER_KV)], axis=0)
            qh = (qh.astype(F32) * scale).astype(BF16)
            s_c = lax.dot_general(qh, kv_ref[0, :n_ctx, sl], nt, preferred_element_type=F32)
            finish(h, [s_c], [kv_ref[0, :n_ctx, vsl]])

    @pl.when(j >= nb_ctx)
    def _():
        jl = j - nb_ctx
        start = jnp.clip((jl - 1) * ATTN_BLOCK, 0, n_lat - n_win)
        start = pl.multiple_of(start, ATTN_BLOCK)
        qpos = jl * ATTN_BLOCK + (lax.broadcasted_iota(jnp.int32, (rows_q, n_win), 0) % ATTN_BLOCK)
        kpos = start + lax.broadcasted_iota(jnp.int32, (rows_q, n_win), 1)
        valid = jnp.abs(qpos - kpos) <= WINDOW
        cq = cq_ref[...] * scale
        s1q = s1q_ref[...] * scale
        s2q = s2q_ref[...] * scale
        for h in range(n_kv):
            sl = slice(h * HEAD_DIM, (h + 1) * HEAD_DIM)
            vsl = slice(kvw + h * HEAD_DIM, kvw + (h + 1) * HEAD_DIM)
            qh = jnp.concatenate(
                [_rope(q_ref[0, :, (h * Q_PER_KV + g) * HEAD_DIM:(h * Q_PER_KV + g + 1) * HEAD_DIM].astype(F32),
                       cq, s1q, s2q).astype(BF16)
                 for g in range(Q_PER_KV)], axis=0)
            kw = kr_ref[pl.ds(start, n_win), sl]
            vw = kv_ref[0, pl.ds(n_ctx + start, n_win), vsl]
            s_w = lax.dot_general(qh, kw, nt, preferred_element_type=F32)
            s_w = jnp.where(valid, s_w, NEG_INF)
            s_c = lax.dot_general(qh, kv_ref[0, :n_ctx, sl], nt, preferred_element_type=F32)
            finish(h, [s_w, s_c], [vw, kv_ref[0, :n_ctx, vsl]])


def _attention(pz, sink, n_ctx, e):
    bsz, t, _ = pz.shape
    n_lat = t - n_ctx
    n_q = e // HEAD_DIM
    n_kv = n_q // Q_PER_KV
    kvw = n_kv * HEAD_DIM
    assert e % (2 * kvw) == 0 and n_lat >= ATTN_BLOCK + 2 * WINDOW
    cos, s1, s2 = _rope_tables(n_lat)
    nb_ctx = n_ctx // ATTN_BLOCK
    qtab = pl.BlockSpec((ATTN_BLOCK, HEAD_DIM), lambda b, j: (jnp.maximum(j - nb_ctx, 0), 0))
    ktab = pl.BlockSpec((n_lat, HEAD_DIM), lambda b, j: (0, 0))
    return pl.pallas_call(
        functools.partial(_attn_kernel, n_ctx=n_ctx, n_lat=n_lat, n_kv=n_kv),
        grid=(bsz, t // ATTN_BLOCK),
        in_specs=[pl.BlockSpec(memory_space=pltpu.SMEM),
                  pl.BlockSpec((1, ATTN_BLOCK, e), lambda b, j: (b, j, 1)),
                  pl.BlockSpec((1, t, 2 * kvw), lambda b, j: (b, 0, 2 * e // (2 * kvw))),
                  qtab, qtab, qtab, ktab, ktab, ktab],
        out_specs=pl.BlockSpec((1, ATTN_BLOCK, e), lambda b, j: (b, j, 0)),
        out_shape=jax.ShapeDtypeStruct((bsz, t, e), BF16),
        scratch_shapes=[pltpu.VMEM((n_lat, kvw), BF16)],
        compiler_params=_cparams(2),
    )(sink.astype(F32), pz, pz, cos, s1, s2, cos, s1, s2)


def _s5_operators(lam_re, lam_im, log_step, b_re, b_im, c_re, c_im, d_skip):
    n_g = lam_re.shape[1]
    tc = S5_CHUNK
    dt = jnp.exp(log_step.astype(F32))[..., None]
    lr, li = lam_re.astype(F32) * dt, lam_im.astype(F32) * dt
    k = jnp.arange(tc + 1, dtype=F32)[:, None, None, None]
    mag = jnp.exp(lr[None] * k)
    pr, pi = mag * jnp.cos(li[None] * k), mag * jnp.sin(li[None] * k)
    ar1, ai1 = pr[1] - 1.0, pi[1]
    den = lam_re.astype(F32) ** 2 + lam_im.astype(F32) ** 2
    cr = (ar1 * lam_re + ai1 * lam_im) / den
    ci = (ai1 * lam_re - ar1 * lam_im) / den
    br, bi = b_re.astype(F32), b_im.astype(F32)
    bbr = cr[..., None] * br - ci[..., None] * bi
    bbi = cr[..., None] * bi + ci[..., None] * br
    ccr, cci = c_re.astype(F32), c_im.astype(F32)

    def kern(d):
        xr = pr[:tc, d][..., None] * bbr[d][None] - pi[:tc, d][..., None] * bbi[d][None]
        xi = pr[:tc, d][..., None] * bbi[d][None] + pi[:tc, d][..., None] * bbr[d][None]
        return (jnp.einsum('gnp,tgpm->tgnm', ccr[d], xr) - jnp.einsum('gnp,tgpm->tgnm', cci[d], xi))

    kf, kb = kern(0), kern(1)
    s_idx = jnp.arange(tc)[:, None]
    t_idx = jnp.arange(tc)[None, :]
    lag = t_idx - s_idx
    mf = jnp.where((lag >= 0)[:, :, None, None, None], kf[jnp.clip(lag, 0, tc - 1)], 0.0)
    mb = jnp.where((lag <= 0)[:, :, None, None, None], kb[jnp.clip(-lag, 0, tc - 1)], 0.0)
    m = (mf + mb).transpose(2, 0, 4, 1, 3)
    eye = (jnp.eye(tc, dtype=F32)[:, None, :, None] * jnp.eye(S5_GROUP, dtype=F32)[None, :, None, :])
    m = m + eye[None] * d_skip.astype(F32).reshape(n_g, 1, S5_GROUP, 1, 1)
    m = m.reshape(n_g, tc * S5_GROUP, tc * S5_GROUP)

    def qpart(d, pw_r, pw_i):
        qr = pw_r[..., None] * bbr[d][None] - pw_i[..., None] * bbi[d][None]
        qi = pw_r[..., None] * bbi[d][None] + pw_i[..., None] * bbr[d][None]
        return qr.transpose(1, 0, 3, 2), qi.transpose(1, 0, 3, 2)

    qfr, qfi = qpart(0, pr[tc - 1 - jnp.arange(tc), 0], pi[tc - 1 - jnp.arange(tc), 0])
    qbr, qbi = qpart(1, pr[jnp.arange(tc), 1], pi[jnp.arange(tc), 1])
    q = jnp.concatenate([qfr, qbr, qfi, qbi], axis=-1).reshape(n_g, tc * S5_GROUP, 4 * S5_STATE)

    def ppart(d, pw_r, pw_i):
        xr = ccr[d][None] * pw_r[:, :, None, :] - cci[d][None] * pw_i[:, :, None, :]
        xi = ccr[d][None] * pw_i[:, :, None, :] + cci[d][None] * pw_r[:, :, None, :]
        return xr.transpose(1, 3, 0, 2), (-xi).transpose(1, 3, 0, 2)

    pfr, pfi = ppart(0, pr[1 + jnp.arange(tc), 0], pi[1 + jnp.arange(tc), 0])
    pbr, pbi = ppart(1, pr[tc - jnp.arange(tc), 1], pi[tc - jnp.arange(tc), 1])
    p = jnp.concatenate([pfr, pbr, pfi, pbi], axis=1).reshape(n_g, 4 * S5_STATE, tc * S5_GROUP)
    a = jnp.stack([jnp.concatenate([pr[tc, 0], pr[tc, 1]], axis=-1),
                   jnp.concatenate([pi[tc, 0], pi[tc, 1]], axis=-1)], axis=1)
    a = jnp.pad(a, ((0, 0), (0, 6), (0, 0)))
    return m.astype(BF16), q.astype(BF16), p.astype(BF16), a


def _s5_kernel(u_ref, m_ref, q_ref, p_ref, a_ref, y_ref, s_ref, h_ref, *, nb, n_chunks, nc_ctx):
    ns = S5_STATE
    u = u_ref[0]
    s_ref[...] = jnp.dot(u, q_ref[0], preferred_element_type=F32)
    ar = a_ref[0, 0:1, :]
    ai = a_ref[0, 1:2, :]
    is_fwd = lax.broadcasted_iota(jnp.int32, (nb, 2 * ns), 1) < ns

    def step(i, carry):
        hr, hi = carry
        cb = jnp.where(i < nc_ctx, nc_ctx - 1 - i, n_chunks - 1 - (i - nc_ctx))
        rf = pl.multiple_of(i * nb, nb)
        rb = pl.multiple_of(cb * nb, nb)
        h_ref[pl.ds(rf, nb), 0:ns] = hr[:, 0:ns]
        h_ref[pl.ds(rb, nb), ns:2 * ns] = hr[:, ns:2 * ns]
        h_ref[pl.ds(rf, nb), 2 * ns:3 * ns] = hi[:, 0:ns]
        h_ref[pl.ds(rb, nb), 3 * ns:4 * ns] = hi[:, ns:2 * ns]
        sf = s_ref[pl.ds(rf, nb), :]
        sb = s_ref[pl.ds(rb, nb), :]
        sr = jnp.where(is_fwd, sf[:, :2 * ns], sb[:, :2 * ns])
        si = jnp.where(is_fwd, sf[:, 2 * ns:], sb[:, 2 * ns:])
        return ar * hr - ai * hi + sr, ar * hi + ai * hr + si

    zero = jnp.zeros((nb, 2 * ns), F32)
    lax.fori_loop(0, n_chunks, step, (zero, zero))
    y = jnp.dot(u, m_ref[0], preferred_element_type=F32)
    y = y + jnp.dot(h_ref[...].astype(BF16), p_ref[0], preferred_element_type=F32)
    y_ref[0] = y.astype(y_ref.dtype)


def _s5_scan(uf, m, q, p, a, nb, nc_ctx):
    n_g, rows, w = uf.shape
    n_chunks = rows // nb
    wspec = pl.BlockSpec((1, w, w), lambda g: (g, 0, 0))
    return pl.pallas_call(
        functools.partial(_s5_kernel, nb=nb, n_chunks=n_chunks, nc_ctx=nc_ctx),
        grid=(n_g,),
        in_specs=[pl.BlockSpec((1, rows, w), lambda g: (g, 0, 0)), wspec, wspec, wspec,
                  pl.BlockSpec((1, 8, 2 * S5_STATE), lambda g: (g, 0, 0))],
        out_specs=pl.BlockSpec((1, rows, w), lambda g: (g, 0, 0)),
        out_shape=jax.ShapeDtypeStruct((n_g, rows, w), BF16),
        scratch_shapes=[pltpu.VMEM((rows, w), F32), pltpu.VMEM((rows, w), F32)],
        compiler_params=_cparams(1),
    )(uf, m, q, p, a)


def _glu_kernel(t_ref, w_ref, b_ref, o_ref):
    t = t_ref[0].astype(F32)
    g = 0.5 * t * (1.0 + jnp.tanh(math.sqrt(2.0 / math.pi) * (t + 0.044715 * t * t * t)))
    lin = jnp.dot(g.astype(BF16), w_ref[...], preferred_element_type=F32) + b_ref[...]
    o_ref[0] = (g * _sigmoid(lin)).astype(o_ref.dtype)


def _glu(t, w_bf, bias, tm):
    bsz, n_tok, e = t.shape
    return pl.pallas_call(
        _glu_kernel,
        grid=(bsz, n_tok // tm),
        in_specs=[pl.BlockSpec((1, tm, e), lambda b, i: (b, i, 0)),
                  pl.BlockSpec((e, e), lambda b, i: (0, 0)),
                  pl.BlockSpec((1, e), lambda b, i: (0, 0))],
        out_specs=pl.BlockSpec((1, tm, e), lambda b, i: (b, i, 0)),
        out_shape=jax.ShapeDtypeStruct((bsz, n_tok, e), BF16),
        compiler_params=_cparams(2),
    )(t, w_bf, bias.astype(F32).reshape(1, e))


def _s5_mixer(pz, n_ctx, e, ops, glu_w_bf, glu_b, tm):
    bsz, t, _ = pz.shape
    n_g = e // S5_GROUP
    n_chunks = t // S5_CHUNK
    m, q, p, a = ops
    u = pz[:, :, e:2 * e].reshape(bsz, n_chunks, S5_CHUNK, n_g, S5_GROUP)
    uf = u.transpose(3, 1, 0, 2, 4).reshape(n_g, n_chunks * bsz, S5_CHUNK * S5_GROUP)
    yf = _s5_scan(uf, m, q, p, a, bsz, n_ctx // S5_CHUNK)
    y = yf.reshape(n_g, n_chunks, bsz, S5_CHUNK, S5_GROUP).transpose(2, 1, 3, 0, 4).reshape(bsz, t, e)
    return _glu(y, glu_w_bf, glu_b, tm)


def _cumsum_rows(x, reverse):
    n = x.shape[0]
    row = lax.broadcasted_iota(jnp.int32, x.shape, 0)
    s = 1
    while s < n:
        if reverse:
            x = x + jnp.where(row < n - s, pltpu.roll(x, n - s, 0), 0.0)
        else:
            x = x + jnp.where(row >= s, pltpu.roll(x, s, 0), 0.0)
        s *= 2
    return x


def _hgrn_kernel(q_ref, ff_ref, fb_ref, v_ref, lb_ref, ng_ref, o_ref,
                 ds_ref, dec_ref, st_ref, qd_ref, oi_ref, *, layer, n_chunks, nc_ctx):
    c = HG_CHUNK
    hd = HG_HEAD
    nt = (((1,), (1,)), ((), ()))
    tn = (((0,), (0,)), ((), ()))

    lbw = lb_ref[...].astype(F32)
    ew = jnp.exp(lbw - jnp.max(lbw, axis=0, keepdims=True))
    lb = jnp.sum(ew[1:layer + 1], axis=0, keepdims=True) / jnp.sum(ew, axis=0, keepdims=True)

    ti = lax.broadcasted_iota(jnp.int32, (c, c), 0)
    si = lax.broadcasted_iota(jnp.int32, (c, c), 1)
    low = si <= ti
    upp = si >= ti

    def chunk_a(ci, carry):
        r0 = pl.multiple_of(ci * c, c)
        q = q_ref[0, pl.ds(r0, c), :].astype(F32)
        v = v_ref[0, pl.ds(r0, c), :]
        att = jnp.zeros((c, c), F32)
        kds = []
        for d, f_ref in enumerate((ff_ref, fb_ref)):
            fl = f_ref[0, pl.ds(r0, c), :].astype(F32)
            f = lb + (1.0 - lb) * _sigmoid(fl)
            b = _cumsum_rows(jnp.log(f), reverse=bool(d))
            mid = c // 2 if d == 0 else c // 2 - 1
            last = c - 1 if d == 0 else 0
            ref = b[mid:mid + 1]
            b_last = b[last:last + 1]
            eb = jnp.exp(b - ref)
            qe = q * eb
            ke = (1.0 - f) * jnp.exp(ref - b)
            a = lax.dot_general(qe.astype(BF16), ke.astype(BF16), nt, preferred_element_type=F32)
            att = att + jnp.where(low if d == 0 else upp, a, 0.0)
            qd_ref[pl.ds(r0, c), d * hd:(d + 1) * hd] = (qe * jnp.exp(ref)).astype(BF16)
            kds.append((ke * jnp.exp(b_last - ref)).astype(BF16))
            dec_ref[ci, :, d * hd:(d + 1) * hd] = jnp.exp(b_last)
        oi_ref[pl.ds(r0, c), :] = jnp.dot(att.astype(BF16), v, preferred_element_type=F32)
        kd = jnp.concatenate(kds, axis=1)
        ds_ref[ci] = lax.dot_general(v, kd, tn, preferred_element_type=F32)
        return carry

    lax.fori_loop(0, n_chunks, chunk_a, 0)

    def scan_b(i, carry):
        sf, sb = carry
        cb = jnp.where(i < nc_ctx, nc_ctx - 1 - i, n_chunks - 1 - (i - nc_ctx))
        st_ref[i, :, 0:hd] = sf.astype(BF16)
        st_ref[cb, :, hd:2 * hd] = sb.astype(BF16)
        sf = sf * dec_ref[i, :, 0:hd] + ds_ref[i, :, 0:hd]
        sb = sb * dec_ref[cb, :, hd:2 * hd] + ds_ref[cb, :, hd:2 * hd]
        return sf, sb

    zero = jnp.zeros((hd, hd), F32)
    lax.fori_loop(0, n_chunks, scan_b, (zero, zero))

    gn = ng_ref[...].astype(F32)

    def chunk_c(ci, carry):
        r0 = pl.multiple_of(ci * c, c)
        o = oi_ref[pl.ds(r0, c), :] + lax.dot_general(
            qd_ref[pl.ds(r0, c), :], st_ref[ci], nt, preferred_element_type=F32)
        o = o * lax.rsqrt(jnp.mean(o * o, axis=-1, keepdims=True) + NORM_EPS) * gn
        o_ref[0, pl.ds(r0, c), :] = o.astype(o_ref.dtype)
        return carry

    lax.fori_loop(0, n_chunks, chunk_c, 0)


def _hgrn(pz, hg_lb, norm_g, layer, n_ctx, e):
    bsz, t, _ = pz.shape
    n_h = e // HG_HEAD
    n_chunks = t // HG_CHUNK
    depth = hg_lb.shape[0]
    hd = HG_HEAD

    def col(k):
        return pl.BlockSpec((1, t, hd), lambda b, h: (b, 0, (k + 1) * n_h + h))

    return pl.pallas_call(
        functools.partial(_hgrn_kernel, layer=layer, n_chunks=n_chunks, nc_ctx=n_ctx // HG_CHUNK),
        grid=(bsz, n_h),
        in_specs=[col(0), col(1), col(2), col(3),
                  pl.BlockSpec((depth, hd), lambda b, h: (0, h)),
                  pl.BlockSpec((1, hd), lambda b, h: (0, h))],
        out_specs=pl.BlockSpec((1, t, hd), lambda b, h: (b, 0, h)),
        out_shape=jax.ShapeDtypeStruct((bsz, t, e), BF16),
        scratch_shapes=[pltpu.VMEM((n_chunks, hd, 2 * hd), F32),
                        pltpu.VMEM((n_chunks, 1, 2 * hd), F32),
                        pltpu.VMEM((n_chunks, hd, 2 * hd), BF16),
                        pltpu.VMEM((t, 2 * hd), BF16),
                        pltpu.VMEM((t, hd), F32)],
        compiler_params=_cparams(2),
    )(pz, pz, pz, pz, hg_lb.astype(F32), norm_g.astype(F32).reshape(1, e))


def _z_first(w, e):
    return jnp.concatenate([w[..., -e:], w[..., :-e]], axis=-1)


def kernel(x, c, ctx, c_ctx, ada_w, ada_b, ln_g, ln_b, w_out, attn_w_in, attn_sink, s5_w_in, s5_lam_re, s5_lam_im, s5_log_step, s5_b_re, s5_b_im, s5_c_re, s5_c_im, s5_d, s5_glu_w, s5_glu_b, hg_w_in, hg_lb, hg_norm_g):
    bsz, n_lat, d = x.shape
    n_ctx = ctx.shape[1]
    depth = ada_w.shape[0]
    e = w_out.shape[1]
    alpha = (2.0 * depth) ** 0.25
    t = n_ctx + n_lat
    tm = 768 if t % 768 == 0 else 256

    mod_rows = -(-(bsz + 1) // MOD_ROWS_PAD) * MOD_ROWS_PAD
    c_all = jnp.concatenate([c.astype(F32), c_ctx.astype(F32)[None],
                             jnp.zeros((mod_rows - bsz - 1, d), F32)], axis=0)
    mod = _ada(c_all, ada_w.astype(F32), ada_b.astype(F32))
    mod3 = mod.reshape(depth * mod_rows, 1, 3 * d)

    xa = jnp.concatenate([ctx.astype(F32), x.astype(F32)], axis=1)
    for i in range(depth):
        kind, j = i % N_MIXERS, i // N_MIXERS
        w_in = _z_first((attn_w_in, s5_w_in, hg_w_in)[kind][j], e).astype(BF16)
        pz = _inproj(xa, mod3, i, mod_rows, w_in, n_ctx, tn=512)
        if kind == 0:
            y = _attention(pz, attn_sink[j], n_ctx, e)
        elif kind == 1:
            ops = _s5_operators(s5_lam_re[j], s5_lam_im[j], s5_log_step[j], s5_b_re[j], s5_b_im[j],
                                s5_c_re[j], s5_c_im[j], s5_d[j])
            y = _s5_mixer(pz, n_ctx, e, ops, s5_glu_w[j].astype(BF16), s5_glu_b[j], tm)
        else:
            y = _hgrn(pz, hg_lb, hg_norm_g[j], i, n_ctx, e)
        xa = _outproj(y, pz, xa, mod3, i, mod_rows, w_out[i].astype(BF16), ln_g[i].astype(F32),
                      ln_b[i].astype(F32), n_ctx, alpha, tm)
    return xa[:, n_ctx:].astype(x.dtype)
```

```python
import functools
import math

import jax
import jax.numpy as jnp
from jax import lax
from jax.experimental import pallas as pl
from jax.experimental.pallas import tpu as pltpu

F32 = jnp.float32
BF16 = jnp.bfloat16

N_MIXERS = 3
HEAD_DIM = 128
Q_PER_KV = 4
WINDOW = 128
ATTN_BLOCK = 128
GRID_W = 64
ROPE_THETA = 10000.0
S5_GROUP = 16
S5_STATE = 64
S5_CHUNK = 16
HG_HEAD = 128
HG_CHUNK = 64
HG_UNROLL = 6
HG_UNROLL_INTER = 12
HG_PAIR = 2
HG_NORM_ROWS = 256
NORM_EPS = 1e-5
NEG_INF = -1e30
LOG2E = math.log2(math.e)
ATTN_ROWS = 32
MOD_ROWS_PAD = 8
VMEM_LIMIT = 56 * 1024 * 1024


def _cparams(n_axes):
    return pltpu.CompilerParams(dimension_semantics=("arbitrary",) * n_axes,
                                vmem_limit_bytes=VMEM_LIMIT)


def _sigmoid(x):
    return 1.0 / (1.0 + jnp.exp(-x))


def _ada_kernel(c_ref, w_ref, b_ref, o_ref):
    cv = c_ref[...]
    s = cv * _sigmoid(cv)
    o_ref[0] = jnp.dot(s, w_ref[0], preferred_element_type=F32,
                       precision=lax.Precision.HIGHEST) + b_ref[0]


def _ada(c_all, ada_w, ada_b):
    depth, d, d3 = ada_w.shape
    rows = c_all.shape[0]
    return pl.pallas_call(
        _ada_kernel,
        grid=(depth, d3 // d),
        in_specs=[pl.BlockSpec((rows, d), lambda i, j: (0, 0)),
                  pl.BlockSpec((1, d, d), lambda i, j: (i, 0, j)),
                  pl.BlockSpec((1, 1, d), lambda i, j: (i, 0, j))],
        out_specs=pl.BlockSpec((1, rows, d), lambda i, j: (i, 0, j)),
        out_shape=jax.ShapeDtypeStruct((depth, rows, d3), F32),
        compiler_params=_cparams(2),
    )(c_all, ada_w, ada_b.reshape(depth, 1, d3))


def _inproj_kernel(x_ref, mb_ref, mc_ref, w_ref, o_ref, h_ref, *, n_ctx, d):
    @pl.when(pl.program_id(1) == 0)
    def _():
        mb = mb_ref[0]
        mc = mc_ref[0]
        h_ref[:n_ctx] = (x_ref[0, :n_ctx] * (1.0 + mc[:, d:2 * d]) + mc[:, :d]).astype(BF16)
        h_ref[n_ctx:] = (x_ref[0, n_ctx:] * (1.0 + mb[:, d:2 * d]) + mb[:, :d]).astype(BF16)

    o_ref[0] = jnp.dot(h_ref[...], w_ref[...], preferred_element_type=F32).astype(o_ref.dtype)


def _inproj(xa, mod3, layer, mod_rows, w_bf, n_ctx, tn):
    bsz, t, d = xa.shape
    n = w_bf.shape[1]
    base = layer * mod_rows
    return pl.pallas_call(
        functools.partial(_inproj_kernel, n_ctx=n_ctx, d=d),
        grid=(bsz, n // tn),
        in_specs=[pl.BlockSpec((1, t, d), lambda b, j: (b, 0, 0)),
                  pl.BlockSpec((1, 1, 3 * d), lambda b, j: (base + b, 0, 0)),
                  pl.BlockSpec((1, 1, 3 * d), lambda b, j: (base + bsz, 0, 0)),
                  pl.BlockSpec((d, tn), lambda b, j: (0, j))],
        out_specs=pl.BlockSpec((1, t, tn), lambda b, j: (b, 0, j)),
        out_shape=jax.ShapeDtypeStruct((bsz, t, n), BF16),
        scratch_shapes=[pltpu.VMEM((t, d), BF16)],
        compiler_params=_cparams(2),
    )(xa, mod3, mod3, w_bf)


def _outproj_kernel(y_ref, z_ref, x_ref, mb_ref, mc_ref, w_ref, g_ref, b_ref, o_ref,
                    *, n_ctx, d, alpha, tm):
    z = z_ref[0].astype(F32)
    a = (y_ref[0].astype(F32) * (z * _sigmoid(z))).astype(BF16)
    br = jnp.dot(a, w_ref[...], preferred_element_type=F32)
    rows = pl.program_id(1) * tm + lax.broadcasted_iota(jnp.int32, (tm, 1), 0)
    gate = jnp.where(rows < n_ctx, mc_ref[0][:, 2 * d:], mb_ref[0][:, 2 * d:])
    v = alpha * x_ref[0] + gate * br
    mu = jnp.mean(v, axis=-1, keepdims=True)
    vc = v - mu
    var = jnp.mean(vc * vc, axis=-1, keepdims=True)
    o_ref[0] = vc * lax.rsqrt(var + NORM_EPS) * g_ref[...] + b_ref[...]


def _outproj(y, pz, xa, mod3, layer, mod_rows, w_bf, ln_g, ln_b, n_ctx, alpha, tm):
    bsz, t, d = xa.shape
    e = w_bf.shape[0]
    base = layer * mod_rows
    return pl.pallas_call(
        functools.partial(_outproj_kernel, n_ctx=n_ctx, d=d, alpha=alpha, tm=tm),
        grid=(bsz, t // tm),
        in_specs=[pl.BlockSpec((1, tm, e), lambda b, i: (b, i, 0)),
                  pl.BlockSpec((1, tm, e), lambda b, i: (b, i, 0)),
                  pl.BlockSpec((1, tm, d), lambda b, i: (b, i, 0)),
                  pl.BlockSpec((1, 1, 3 * d), lambda b, i: (base + b, 0, 0)),
                  pl.BlockSpec((1, 1, 3 * d), lambda b, i: (base + bsz, 0, 0)),
                  pl.BlockSpec((e, d), lambda b, i: (0, 0)),
                  pl.BlockSpec((1, d), lambda b, i: (0, 0)),
                  pl.BlockSpec((1, d), lambda b, i: (0, 0))],
        out_specs=pl.BlockSpec((1, tm, d), lambda b, i: (b, i, 0)),
        out_shape=jax.ShapeDtypeStruct((bsz, t, d), F32),
        compiler_params=_cparams(2),
    )(y, pz, xa, mod3, mod3, w_bf, ln_g.reshape(1, d), ln_b.reshape(1, d))


def _rope_tables(n_lat):
    pos = jnp.arange(n_lat, dtype=jnp.int32)
    row = (pos // GRID_W).astype(F32)
    col = (pos % GRID_W).astype(F32)
    nf = HEAD_DIM // 4
    inv_freq = jnp.power(ROPE_THETA, -jnp.arange(nf, dtype=F32) / nf)
    ang_r = row[:, None] * inv_freq[None, :]
    ang_c = col[:, None] * inv_freq[None, :]
    zeros = jnp.zeros_like(ang_r)
    cos = jnp.concatenate([jnp.cos(ang_r)] * 2 + [jnp.cos(ang_c)] * 2, axis=-1)
    s1 = jnp.concatenate([-jnp.sin(ang_r), zeros, -jnp.sin(ang_c), zeros], axis=-1)
    s2 = jnp.concatenate([zeros, jnp.sin(ang_r), zeros, jnp.sin(ang_c)], axis=-1)
    return cos, s1, s2


def _rope(x, cos, s1, s2):
    quarter = HEAD_DIM // 4
    return (x * cos + pltpu.roll(x, HEAD_DIM - quarter, 1) * s1
            + pltpu.roll(x, quarter, 1) * s2)


def _attn_kernel(sink_ref, q_ref, kv_ref, cq_ref, s1q_ref, s2q_ref, ck_ref, s1k_ref, s2k_ref,
                 o_ref, kr_ref, vx_ref, s_ref, p_ref, e_ref, bias_ref, *, n_ctx, n_lat, n_kv):
    j = pl.program_id(1)
    nb_ctx = n_ctx // ATTN_BLOCK
    kvw = n_kv * HEAD_DIM
    n_win = ATTN_BLOCK + 2 * WINDOW
    qscale = HEAD_DIM ** -0.5 * LOG2E
    rows_q = Q_PER_KV * ATTN_BLOCK
    vxw = 2 * HEAD_DIM
    nt = (((1,), (1,)), ((), ()))

    @pl.when(j == 0)
    def _():
        for h in range(n_kv):
            sl = slice(h * HEAD_DIM, (h + 1) * HEAD_DIM)
            k = kv_ref[0, n_ctx:, sl].astype(F32)
            kr_ref[:, sl] = _rope(k, ck_ref[...], s1k_ref[...], s2k_ref[...]).astype(BF16)
            vx_ref[:, h * vxw:h * vxw + HEAD_DIM] = kv_ref[0, :, kvw + h * HEAD_DIM:kvw + (h + 1) * HEAD_DIM]
            vx_ref[:, h * vxw + HEAD_DIM:(h + 1) * vxw] = jnp.ones((n_ctx + n_lat, HEAD_DIM), BF16)

    def attend(h, qh, kparts, vparts, use_bias):
        cols = []
        off = 0
        for kp in kparts:
            n = kp.shape[0]
            s_ref[:, off:off + n] = lax.dot_general(qh, kp, nt, preferred_element_type=F32)
            cols.append((off, n))
            off += n
        for rb in range(rows_q // ATTN_ROWS):
            r0 = rb * ATTN_ROWS
            sk = sink_ref[h * Q_PER_KV + r0 // ATTN_BLOCK] * LOG2E
            parts = []
            for idx, (o_, n) in enumerate(cols):
                s = s_ref[r0:r0 + ATTN_ROWS, o_:o_ + n]
                if use_bias and idx == 0:
                    ql0 = r0 % ATTN_BLOCK
                    s = s + bias_ref[ql0:ql0 + ATTN_ROWS, :]
                parts.append(s)
            m = jnp.max(parts[0], axis=-1, keepdims=True)
            for s in parts[1:]:
                m = jnp.maximum(m, jnp.max(s, axis=-1, keepdims=True))
            m = jnp.maximum(m, sk)
            for (o_, n), s in zip(cols, parts):
                p_ref[r0:r0 + ATTN_ROWS, o_:o_ + n] = jnp.exp2(s - m).astype(BF16)
            e_ref[r0:r0 + ATTN_ROWS, :] = jnp.exp2(sk - m)
        ox = None
        for (o_, n), vp in zip(cols, vparts):
            part = jnp.dot(p_ref[:, o_:o_ + n], vp, preferred_element_type=F32)
            ox = part if ox is None else ox + part
        l = ox[:, HEAD_DIM:HEAD_DIM + 1] + e_ref[...]
        o = ox[:, :HEAD_DIM] * (1.0 / l)
        for g in range(Q_PER_KV):
            hq = h * Q_PER_KV + g
            o_ref[0, :, hq * HEAD_DIM:(hq + 1) * HEAD_DIM] = (
                o[g * ATTN_BLOCK:(g + 1) * ATTN_BLOCK].astype(o_ref.dtype))

    def q_head(hq):
        return q_ref[0, :, hq * HEAD_DIM:(hq + 1) * HEAD_DIM].astype(F32)

    @pl.when(j < nb_ctx)
    def _():
        for h in range(n_kv):
            sl = slice(h * HEAD_DIM, (h + 1) * HEAD_DIM)
            qh = jnp.concatenate([(q_head(h * Q_PER_KV + g) * qscale).astype(BF16)
                                  for g in range(Q_PER_KV)], axis=0)
            attend(h, qh, [kv_ref[0, :n_ctx, sl]], [vx_ref[:n_ctx, h * vxw:(h + 1) * vxw]], False)

    @pl.when(j >= nb_ctx)
    def _():
        jl = j - nb_ctx
        start = jnp.clip((jl - 1) * ATTN_BLOCK, 0, n_lat - n_win)
        start = pl.multiple_of(start, ATTN_BLOCK)
        delta = jl * ATTN_BLOCK - start
        ql = lax.broadcasted_iota(jnp.int32, (ATTN_BLOCK, n_win), 0)
        kl = lax.broadcasted_iota(jnp.int32, (ATTN_BLOCK, n_win), 1)
        bias_ref[...] = jnp.where(jnp.abs(delta + ql - kl) <= WINDOW, 0.0, NEG_INF)
        cq = cq_ref[...] * qscale
        s1q = s1q_ref[...] * qscale
        s2q = s2q_ref[...] * qscale
        for h in range(n_kv):
            sl = slice(h * HEAD_DIM, (h + 1) * HEAD_DIM)
            qh = jnp.concatenate([_rope(q_head(h * Q_PER_KV + g), cq, s1q, s2q).astype(BF16)
                                  for g in range(Q_PER_KV)], axis=0)
            attend(h, qh,
                   [kr_ref[pl.ds(start, n_win), sl], kv_ref[0, :n_ctx, sl]],
                   [vx_ref[pl.ds(n_ctx + start, n_win), h * vxw:(h + 1) * vxw],
                    vx_ref[:n_ctx, h * vxw:(h + 1) * vxw]], True)


def _attention(pz, sink, n_ctx, e):
    bsz, t, _ = pz.shape
    n_lat = t - n_ctx
    n_q = e // HEAD_DIM
    n_kv = n_q // Q_PER_KV
    kvw = n_kv * HEAD_DIM
    n_win = ATTN_BLOCK + 2 * WINDOW
    rows_q = Q_PER_KV * ATTN_BLOCK
    assert (2 * e) % (2 * kvw) == 0 and n_lat >= n_win
    cos, s1, s2 = _rope_tables(n_lat)
    nb_ctx = n_ctx // ATTN_BLOCK
    qtab = pl.BlockSpec((ATTN_BLOCK, HEAD_DIM), lambda b, j: (jnp.maximum(j - nb_ctx, 0), 0))
    ktab = pl.BlockSpec((n_lat, HEAD_DIM), lambda b, j: (0, 0))
    return pl.pallas_call(
        functools.partial(_attn_kernel, n_ctx=n_ctx, n_lat=n_lat, n_kv=n_kv),
        grid=(bsz, t // ATTN_BLOCK),
        in_specs=[pl.BlockSpec(memory_space=pltpu.SMEM),
                  pl.BlockSpec((1, ATTN_BLOCK, e), lambda b, j: (b, j, 1)),
                  pl.BlockSpec((1, t, 2 * kvw), lambda b, j: (b, 0, 2 * e // (2 * kvw))),
                  qtab, qtab, qtab, ktab, ktab, ktab],
        out_specs=pl.BlockSpec((1, ATTN_BLOCK, e), lambda b, j: (b, j, 0)),
        out_shape=jax.ShapeDtypeStruct((bsz, t, e), BF16),
        scratch_shapes=[pltpu.VMEM((n_lat, kvw), BF16),
                        pltpu.VMEM((t, n_kv * 2 * HEAD_DIM), BF16),
                        pltpu.VMEM((rows_q, n_win + n_ctx), F32),
                        pltpu.VMEM((rows_q, n_win + n_ctx), BF16),
                        pltpu.VMEM((rows_q, 1), F32),
                        pltpu.VMEM((ATTN_BLOCK, n_win), F32)],
        compiler_params=_cparams(2),
    )(sink.astype(F32), pz, pz, cos, s1, s2, cos, s1, s2)


def _cg_order(w, axis):
    axis = axis % w.ndim
    n = w.shape[axis]
    shp = w.shape[:axis] + (n // S5_GROUP, S5_GROUP) + w.shape[axis + 1:]
    return jnp.swapaxes(w.reshape(shp), axis, axis + 1).reshape(w.shape)


def _s5_operators(lam_re, lam_im, log_step, b_re, b_im, c_re, c_im, d_skip):
    n_g = lam_re.shape[1]
    tc = S5_CHUNK
    dt = jnp.exp(log_step.astype(F32))[..., None]
    lr, li = lam_re.astype(F32) * dt, lam_im.astype(F32) * dt
    k = jnp.arange(tc + 1, dtype=F32)[:, None, None, None]
    mag = jnp.exp(lr[None] * k)
    pr, pi = mag * jnp.cos(li[None] * k), mag * jnp.sin(li[None] * k)
    ar1, ai1 = pr[1] - 1.0, pi[1]
    den = lam_re.astype(F32) ** 2 + lam_im.astype(F32) ** 2
    cr = (ar1 * lam_re + ai1 * lam_im) / den
    ci = (ai1 * lam_re - ar1 * lam_im) / den
    br, bi = b_re.astype(F32), b_im.astype(F32)
    bbr = cr[..., None] * br - ci[..., None] * bi
    bbi = cr[..., None] * bi + ci[..., None] * br
    ccr, cci = c_re.astype(F32), c_im.astype(F32)

    def kern(d):
        xr = pr[:tc, d][..., None] * bbr[d][None] - pi[:tc, d][..., None] * bbi[d][None]
        xi = pr[:tc, d][..., None] * bbi[d][None] + pi[:tc, d][..., None] * bbr[d][None]
        return (jnp.einsum('gnp,tgpm->tgnm', ccr[d], xr) - jnp.einsum('gnp,tgpm->tgnm', cci[d], xi))

    kf, kb = kern(0), kern(1)
    s_idx = jnp.arange(tc)[:, None]
    t_idx = jnp.arange(tc)[None, :]
    lag = t_idx - s_idx
    mf = jnp.where((lag >= 0)[:, :, None, None, None], kf[jnp.clip(lag, 0, tc - 1)], 0.0)
    mb = jnp.where((lag <= 0)[:, :, None, None, None], kb[jnp.clip(-lag, 0, tc - 1)], 0.0)
    m = (mf + mb).transpose(2, 4, 0, 3, 1)
    eye = (jnp.eye(S5_GROUP, dtype=F32)[:, None, :, None] * jnp.eye(tc, dtype=F32)[None, :, None, :])
    m = m + eye[None] * d_skip.astype(F32).reshape(n_g, S5_GROUP, 1, 1, 1)
    m = m.reshape(n_g, tc * S5_GROUP, tc * S5_GROUP)

    def qpart(d, pw_r, pw_i):
        qr = pw_r[..., None] * bbr[d][None] - pw_i[..., None] * bbi[d][None]
        qi = pw_r[..., None] * bbi[d][None] + pw_i[..., None] * bbr[d][None]
        return qr.transpose(1, 3, 0, 2), qi.transpose(1, 3, 0, 2)

    qfr, qfi = qpart(0, pr[tc - 1 - jnp.arange(tc), 0], pi[tc - 1 - jnp.arange(tc), 0])
    qbr, qbi = qpart(1, pr[jnp.arange(tc), 1], pi[jnp.arange(tc), 1])
    q = jnp.concatenate([qfr, qbr, qfi, qbi], axis=-1).reshape(n_g, tc * S5_GROUP, 4 * S5_STATE)

    def ppart(d, pw_r, pw_i):
        xr = ccr[d][None] * pw_r[:, :, None, :] - cci[d][None] * pw_i[:, :, None, :]
        xi = ccr[d][None] * pw_i[:, :, None, :] + cci[d][None] * pw_r[:, :, None, :]
        return xr.transpose(1, 3, 2, 0), (-xi).transpose(1, 3, 2, 0)

    pfr, pfi = ppart(0, pr[1 + jnp.arange(tc), 0], pi[1 + jnp.arange(tc), 0])
    pbr, pbi = ppart(1, pr[tc - jnp.arange(tc), 1], pi[tc - jnp.arange(tc), 1])
    p = jnp.concatenate([pfr, pbr, pfi, pbi], axis=1).reshape(n_g, 4 * S5_STATE, tc * S5_GROUP)
    a = jnp.stack([jnp.concatenate([pr[tc, 0], pr[tc, 1]], axis=-1),
                   jnp.concatenate([pi[tc, 0], pi[tc, 1]], axis=-1)], axis=1)
    a = jnp.pad(a, ((0, 0), (0, 6), (0, 0)))
    return m.astype(BF16), q.astype(BF16), p.astype(BF16), a


def _s5_pack_kernel(x_ref, o_ref, f_ref, *, nb, n_g):
    half = S5_GROUP // 2
    for b in range(nb):
        xb = x_ref[b].astype(F32)
        for k in range(2):
            a = jnp.concatenate([xb[:, cc * n_g:(cc + 1) * n_g]
                                 for cc in range(k * half, (k + 1) * half)], axis=0)
            f_ref[k, pl.ds(b, n_g, stride=nb), :] = a.T
    full = jnp.concatenate([f_ref[0], f_ref[1]], axis=1)
    o_ref[...] = full.reshape(n_g, nb, S5_CHUNK * S5_GROUP).astype(o_ref.dtype)


def _s5_pack(pz, e):
    bsz, t, _ = pz.shape
    n_g = e // S5_GROUP
    n_chunks = t // S5_CHUNK
    w = S5_CHUNK * S5_GROUP
    return pl.pallas_call(
        functools.partial(_s5_pack_kernel, nb=bsz, n_g=n_g),
        grid=(n_chunks,),
        in_specs=[pl.BlockSpec((bsz, S5_CHUNK, e), lambda i: (0, i, 1))],
        out_specs=pl.BlockSpec((n_g, bsz, w), lambda i: (0, i, 0)),
        out_shape=jax.ShapeDtypeStruct((n_g, n_chunks * bsz, w), BF16),
        scratch_shapes=[pltpu.VMEM((2, n_g * bsz, w // 2), F32)],
        compiler_params=_cparams(1),
    )(pz)


def _s5_kernel(u_ref, m_ref, q_ref, p_ref, a_ref, y_ref, s_ref, h_ref, *, nb, n_chunks, nc_ctx):
    ns = S5_STATE
    u = u_ref[0]
    s_ref[...] = jnp.dot(u, q_ref[0], preferred_element_type=F32)
    ar = a_ref[0, 0:1, :]
    ai = a_ref[0, 1:2, :]
    is_fwd = lax.broadcasted_iota(jnp.int32, (nb, 2 * ns), 1) < ns

    def step(i, carry):
        hr, hi = carry
        cb = jnp.where(i < nc_ctx, nc_ctx - 1 - i, n_chunks - 1 - (i - nc_ctx))
        rf = pl.multiple_of(i * nb, nb)
        rb = pl.multiple_of(cb * nb, nb)
        h_ref[pl.ds(rf, nb), 0:ns] = hr[:, 0:ns]
        h_ref[pl.ds(rb, nb), ns:2 * ns] = hr[:, ns:2 * ns]
        h_ref[pl.ds(rf, nb), 2 * ns:3 * ns] = hi[:, 0:ns]
        h_ref[pl.ds(rb, nb), 3 * ns:4 * ns] = hi[:, ns:2 * ns]
        sf = s_ref[pl.ds(rf, nb), :]
        sb = s_ref[pl.ds(rb, nb), :]
        sr = jnp.where(is_fwd, sf[:, :2 * ns], sb[:, :2 * ns])
        si = jnp.where(is_fwd, sf[:, 2 * ns:], sb[:, 2 * ns:])
        return ar * hr - ai * hi + sr, ar * hi + ai * hr + si

    zero = jnp.zeros((nb, 2 * ns), F32)
    lax.fori_loop(0, n_chunks, step, (zero, zero))
    y = jnp.dot(u, m_ref[0], preferred_element_type=F32)
    y = y + jnp.dot(h_ref[...].astype(BF16), p_ref[0], preferred_element_type=F32)
    y_ref[0] = y.astype(y_ref.dtype)


def _s5_scan(uf, m, q, p, a, nb, nc_ctx):
    n_g, rows, w = uf.shape
    n_chunks = rows // nb
    wspec = pl.BlockSpec((1, w, w), lambda g: (g, 0, 0))
    return pl.pallas_call(
        functools.partial(_s5_kernel, nb=nb, n_chunks=n_chunks, nc_ctx=nc_ctx),
        grid=(n_g,),
        in_specs=[pl.BlockSpec((1, rows, w), lambda g: (g, 0, 0)), wspec, wspec, wspec,
                  pl.BlockSpec((1, 8, 2 * S5_STATE), lambda g: (g, 0, 0))],
        out_specs=pl.BlockSpec((1, rows, w), lambda g: (g, 0, 0)),
        out_shape=jax.ShapeDtypeStruct((n_g, rows, w), BF16),
        scratch_shapes=[pltpu.VMEM((rows, w), F32), pltpu.VMEM((rows, w), F32)],
        compiler_params=_cparams(1),
    )(uf, m, q, p, a)


def _s5_glu_kernel(y_ref, w_ref, b_ref, o_ref, f_ref, t_ref, *, nb, n_g):
    w = S5_CHUNK * S5_GROUP
    half = S5_GROUP // 2
    yv = y_ref[...].astype(F32).reshape(n_g * nb, w)
    for k in range(2):
        f_ref[k] = yv[:, k * (w // 2):(k + 1) * (w // 2)]
    for b in range(nb):
        for k in range(2):
            a = f_ref[k, pl.ds(b, n_g, stride=nb), :].T
            for cc in range(half):
                ch = k * half + cc
                t_ref[b * S5_CHUNK:(b + 1) * S5_CHUNK, ch * n_g:(ch + 1) * n_g] = (
                    a[cc * S5_CHUNK:(cc + 1) * S5_CHUNK, :])
    t = t_ref[...]
    g = 0.5 * t * (1.0 + jnp.tanh(math.sqrt(2.0 / math.pi) * (t + 0.044715 * t * t * t)))
    lin = jnp.dot(g.astype(BF16), w_ref[...], preferred_element_type=F32) + b_ref[...]
    out = (g * _sigmoid(lin)).astype(o_ref.dtype)
    o_ref[...] = out.reshape(nb, S5_CHUNK, out.shape[-1])


def _s5_glu(yf, w_bf, bias, bsz):
    n_g, rows, w = yf.shape
    e = n_g * S5_GROUP
    n_chunks = rows // bsz
    return pl.pallas_call(
        functools.partial(_s5_glu_kernel, nb=bsz, n_g=n_g),
        grid=(n_chunks,),
        in_specs=[pl.BlockSpec((n_g, bsz, w), lambda i: (0, i, 0)),
                  pl.BlockSpec((e, e), lambda i: (0, 0)),
                  pl.BlockSpec((1, e), lambda i: (0, 0))],
        out_specs=pl.BlockSpec((bsz, S5_CHUNK, e), lambda i: (0, i, 0)),
        out_shape=jax.ShapeDtypeStruct((bsz, n_chunks * S5_CHUNK, e), BF16),
        scratch_shapes=[pltpu.VMEM((2, n_g * bsz, w // 2), F32), pltpu.VMEM((bsz * S5_CHUNK, e), F32)],
        compiler_params=_cparams(1),
    )(yf, w_bf, bias.astype(F32).reshape(1, e))


def _s5_mixer(pz, n_ctx, e, ops, glu_w_bf, glu_b):
    bsz = pz.shape[0]
    m, q, p, a = ops
    uf = _s5_pack(pz, e)
    yf = _s5_scan(uf, m, q, p, a, bsz, n_ctx // S5_CHUNK)
    return _s5_glu(yf, glu_w_bf, glu_b, bsz)


def _cumsum_rows(x, reverse):
    n = x.shape[0]
    row = lax.broadcasted_iota(jnp.int32, x.shape, 0)
    s = 1
    while s < n:
        if reverse:
            x = x + jnp.where(row < n - s, pltpu.roll(x, n - s, 0), 0.0)
        else:
            x = x + jnp.where(row >= s, pltpu.roll(x, s, 0), 0.0)
        s *= 2
    return x


def _hgrn_kernel(q_ref, ff_ref, fb_ref, v_ref, lb_ref, ng_ref, o_ref,
                 ds_ref, dec_ref, st_ref, qd_ref, oi_ref, *, layer, n_chunks, nc_ctx, n_tok):
    c = HG_CHUNK
    hd = HG_HEAD
    nt = (((1,), (1,)), ((), ()))
    tn = (((0,), (0,)), ((), ()))

    lbw = lb_ref[...].astype(F32)
    ew = jnp.exp(lbw - jnp.max(lbw, axis=0, keepdims=True))
    lb = jnp.sum(ew[1:layer + 1], axis=0, keepdims=True) / jnp.sum(ew, axis=0, keepdims=True)

    ti = lax.broadcasted_iota(jnp.int32, (c, 2 * c), 0)
    si = lax.broadcasted_iota(jnp.int32, (c, 2 * c), 1)
    m_fwd = si <= ti
    m_bwd = si - c >= ti

    def chunk_a(ci):
        r0 = pl.multiple_of(ci * c, c)
        q = q_ref[0, pl.ds(r0, c), :].astype(F32)
        v = v_ref[0, pl.ds(r0, c), :]
        qes, kes, kds = [], [], []
        for d, f_ref in enumerate((ff_ref, fb_ref)):
            fl = f_ref[0, pl.ds(r0, c), :].astype(F32)
            f = lb + (1.0 - lb) * _sigmoid(fl)
            b = _cumsum_rows(jnp.log(f), reverse=bool(d))
            mid = c // 2 if d == 0 else c // 2 - 1
            last = c - 1 if d == 0 else 0
            ref = b[mid:mid + 1]
            b_last = b[last:last + 1]
            qe = q * jnp.exp(b - ref)
            ke = (1.0 - f) * jnp.exp(ref - b)
            qes.append(qe.astype(BF16))
            kes.append(ke.astype(BF16))
            qd_ref[pl.ds(r0, c), d * hd:(d + 1) * hd] = (qe * jnp.exp(ref)).astype(BF16)
            kds.append((ke * jnp.exp(b_last - ref)).astype(BF16))
            dec_ref[ci, :, d * hd:(d + 1) * hd] = jnp.exp(b_last)
        a = lax.dot_general(jnp.concatenate(qes, axis=0), jnp.concatenate(kes, axis=0), nt,
                            preferred_element_type=F32)
        att = jnp.where(m_fwd, a[:c], jnp.where(m_bwd, a[c:], 0.0))
        oi_ref[pl.ds(r0, c), :] = jnp.dot(att.astype(BF16), jnp.concatenate([v, v], axis=0),
                                          preferred_element_type=F32)
        ds_ref[ci] = lax.dot_general(v, jnp.concatenate(kds, axis=1), tn,
                                     preferred_element_type=F32)

    def loop_a(i, carry):
        for u in range(HG_UNROLL):
            chunk_a(i * HG_UNROLL + u)
        return carry

    lax.fori_loop(0, n_chunks // HG_UNROLL, loop_a, 0)

    def scan_b(i, carry):
        sf, sb = carry
        cb = jnp.where(i < nc_ctx, nc_ctx - 1 - i, n_chunks - 1 - (i - nc_ctx))
        st_ref[i, :, 0:hd] = sf.astype(BF16)
        st_ref[cb, :, hd:2 * hd] = sb.astype(BF16)
        sf = sf * dec_ref[i, :, 0:hd] + ds_ref[i, :, 0:hd]
        sb = sb * dec_ref[cb, :, hd:2 * hd] + ds_ref[cb, :, hd:2 * hd]
        return sf, sb

    zero = jnp.zeros((hd, hd), F32)
    lax.fori_loop(0, n_chunks, scan_b, (zero, zero))

    def chunk_c(ci):
        r0 = pl.multiple_of(ci * c, c)
        oi_ref[pl.ds(r0, c), :] += lax.dot_general(
            qd_ref[pl.ds(r0, c), :], st_ref[ci], nt, preferred_element_type=F32)

    def loop_c(i, carry):
        for u in range(HG_UNROLL):
            chunk_c(i * HG_UNROLL + u)
        return carry

    lax.fori_loop(0, n_chunks // HG_UNROLL, loop_c, 0)

    gn = ng_ref[...].astype(F32)
    for r0 in range(0, n_tok, HG_NORM_ROWS):
        o = oi_ref[r0:r0 + HG_NORM_ROWS, :]
        o = o * lax.rsqrt(jnp.mean(o * o, axis=-1, keepdims=True) + NORM_EPS) * gn
        o_ref[0, r0:r0 + HG_NORM_ROWS, :] = o.astype(o_ref.dtype)


def _hgrn(pz, hg_lb, norm_g, layer, n_ctx, e):
    bsz, t, _ = pz.shape
    n_h = e // HG_HEAD
    n_chunks = t // HG_CHUNK
    depth = hg_lb.shape[0]
    hd = HG_HEAD
    assert n_chunks % HG_UNROLL == 0 and t % HG_NORM_ROWS == 0

    def col(k):
        return pl.BlockSpec((1, t, hd), lambda b, h: (b, 0, (k + 1) * n_h + h))

    return pl.pallas_call(
        functools.partial(_hgrn_kernel, layer=layer, n_chunks=n_chunks, nc_ctx=n_ctx // HG_CHUNK, n_tok=t),
        grid=(bsz, n_h),
        in_specs=[col(0), col(1), col(2), col(3),
                  pl.BlockSpec((depth, hd), lambda b, h: (0, h)),
                  pl.BlockSpec((1, hd), lambda b, h: (0, h))],
        out_specs=pl.BlockSpec((1, t, hd), lambda b, h: (b, 0, h)),
        out_shape=jax.ShapeDtypeStruct((bsz, t, e), BF16),
        scratch_shapes=[pltpu.VMEM((n_chunks, hd, 2 * hd), F32),
                        pltpu.VMEM((n_chunks, 1, 2 * hd), F32),
                        pltpu.VMEM((n_chunks, hd, 2 * hd), BF16),
                        pltpu.VMEM((t, 2 * hd), BF16),
                        pltpu.VMEM((t, hd), F32)],
        compiler_params=_cparams(2),
    )(pz, pz, pz, pz, hg_lb.astype(F32), norm_g.astype(F32).reshape(1, e))


def _cumsum_rows_multi(xs, reverse):
    n = xs[0].shape[0]
    row = lax.broadcasted_iota(jnp.int32, xs[0].shape, 0)
    s = 1
    while s < n:
        nxt = []
        for x, rev in zip(xs, reverse):
            if rev:
                nxt.append(x + jnp.where(row < n - s, pltpu.roll(x, n - s, 0), 0.0))
            else:
                nxt.append(x + jnp.where(row >= s, pltpu.roll(x, s, 0), 0.0))
        xs = nxt
        s *= 2
    return xs


def _hgrn2_kernel(q_ref, ff_ref, fb_ref, v_ref, lb_ref, ng_ref, o_ref,
                  g_ref, k_ref, qe_ref, ke_ref, qd_ref, kd_ref, ds_ref, dec_ref, st_ref, oi_ref,
                  *, layer, n_chunks, nc_ctx, n_tok):
    c = HG_CHUNK
    hd = HG_HEAD
    nt = (((1,), (1,)), ((), ()))
    tn = (((0,), (0,)), ((), ()))
    f_refs = (ff_ref, fb_ref)

    lbw = lb_ref[...].astype(F32)
    ew = jnp.exp(lbw - jnp.max(lbw, axis=0, keepdims=True))
    lb = jnp.sum(ew[1:layer + 1], axis=0, keepdims=True) / jnp.sum(ew, axis=0, keepdims=True)

    def gates(i, carry):
        r0 = pl.multiple_of(i * c, c)
        for d in range(2):
            fl = f_refs[d][0, pl.ds(r0, c), :].astype(F32)
            f = lb + (1.0 - lb) * _sigmoid(fl)
            g_ref[d, pl.ds(r0, c), :] = jnp.log2(f)
            k_ref[d, pl.ds(r0, c), :] = 1.0 - f
        return carry

    lax.fori_loop(0, n_chunks, gates, 0, unroll=2)

    def cumdecay(i, carry):
        r0s = [pl.multiple_of((i * HG_PAIR + u) * c, c) for u in range(HG_PAIR)]
        xs = [g_ref[d, pl.ds(r0, c), :] for r0 in r0s for d in range(2)]
        xs = _cumsum_rows_multi(xs, [False, True] * HG_PAIR)
        k = 0
        for r0 in r0s:
            for d in range(2):
                g_ref[d, pl.ds(r0, c), :] = xs[k]
                k += 1
        return carry

    lax.fori_loop(0, n_chunks // HG_PAIR, cumdecay, 0)

    def decayed(ci, carry):
        r0 = pl.multiple_of(ci * c, c)
        q = q_ref[0, pl.ds(r0, c), :].astype(F32)
        for d in range(2):
            mid = c // 2 if d == 0 else c // 2 - 1
            last = c - 1 if d == 0 else 0
            b = g_ref[d, pl.ds(r0, c), :]
            ref = g_ref[d, pl.ds(r0 + mid, 1), :]
            b_last = g_ref[d, pl.ds(r0 + last, 1), :]
            qe = q * jnp.exp2(b - ref)
            ke = k_ref[d, pl.ds(r0, c), :] * jnp.exp2(ref - b)
            qe_ref[d, pl.ds(r0, c), :] = qe.astype(BF16)
            ke_ref[d, pl.ds(r0, c), :] = ke.astype(BF16)
            qd_ref[pl.ds(r0, c), d * hd:(d + 1) * hd] = (qe * jnp.exp2(ref)).astype(BF16)
            kd_ref[pl.ds(r0, c), d * hd:(d + 1) * hd] = (ke * jnp.exp2(b_last - ref)).astype(BF16)
            dec_ref[ci, :, d * hd:(d + 1) * hd] = jnp.exp2(b_last)
        return carry

    lax.fori_loop(0, n_chunks, decayed, 0, unroll=2)

    ti = lax.broadcasted_iota(jnp.int32, (c, 2 * c), 0)
    si = lax.broadcasted_iota(jnp.int32, (c, 2 * c), 1)
    m_fwd = si <= ti
    m_bwd = si - c >= ti

    def intra(i, carry):
        cis = [i * HG_UNROLL + u for u in range(HG_UNROLL)]
        r0s = [pl.multiple_of(ci * c, c) for ci in cis]
        vs = [v_ref[0, pl.ds(r0, c), :] for r0 in r0s]
        aa = [lax.dot_general(
            jnp.concatenate([qe_ref[0, pl.ds(r0, c), :], qe_ref[1, pl.ds(r0, c), :]], axis=0),
            jnp.concatenate([ke_ref[0, pl.ds(r0, c), :], ke_ref[1, pl.ds(r0, c), :]], axis=0),
            nt, preferred_element_type=F32) for r0 in r0s]
        dss = [lax.dot_general(v, kd_ref[pl.ds(r0, c), :], tn, preferred_element_type=F32)
               for v, r0 in zip(vs, r0s)]
        atts = [jnp.where(m_fwd, a[:c], jnp.where(m_bwd, a[c:], 0.0)).astype(BF16) for a in aa]
        ois = [jnp.dot(att, jnp.concatenate([v, v], axis=0), preferred_element_type=F32)
               for att, v in zip(atts, vs)]
        for ci, r0, ds, oi in zip(cis, r0s, dss, ois):
            ds_ref[ci] = ds
            oi_ref[pl.ds(r0, c), :] = oi
        return carry

    lax.fori_loop(0, n_chunks // HG_UNROLL, intra, 0)

    def scan_b(i, carry):
        sf, sb = carry
        cb = jnp.where(i < nc_ctx, nc_ctx - 1 - i, n_chunks - 1 - (i - nc_ctx))
        st_ref[i, :, 0:hd] = sf.astype(BF16)
        st_ref[cb, :, hd:2 * hd] = sb.astype(BF16)
        sf = sf * dec_ref[i, :, 0:hd] + ds_ref[i, :, 0:hd]
        sb = sb * dec_ref[cb, :, hd:2 * hd] + ds_ref[cb, :, hd:2 * hd]
        return sf, sb

    zero = jnp.zeros((hd, hd), F32)
    lax.fori_loop(0, n_chunks, scan_b, (zero, zero))

    def inter(i, carry):
        cis = [i * HG_UNROLL_INTER + u for u in range(HG_UNROLL_INTER)]
        r0s = [pl.multiple_of(ci * c, c) for ci in cis]
        os_ = [lax.dot_general(qd_ref[pl.ds(r0, c), :], st_ref[ci], nt, preferred_element_type=F32)
               for ci, r0 in zip(cis, r0s)]
        for r0, o in zip(r0s, os_):
            oi_ref[pl.ds(r0, c), :] += o
        return carry

    lax.fori_loop(0, n_chunks // HG_UNROLL_INTER, inter, 0)

    gn = ng_ref[...].astype(F32)
    for r0 in range(0, n_tok, HG_NORM_ROWS):
        o = oi_ref[r0:r0 + HG_NORM_ROWS, :]
        o = o * lax.rsqrt(jnp.mean(o * o, axis=-1, keepdims=True) + NORM_EPS) * gn
        o_ref[0, r0:r0 + HG_NORM_ROWS, :] = o.astype(o_ref.dtype)


def _hgrn2(pz, hg_lb, norm_g, layer, n_ctx, e):
    bsz, t, _ = pz.shape
    n_h = e // HG_HEAD
    n_chunks = t // HG_CHUNK
    depth = hg_lb.shape[0]
    hd = HG_HEAD
    assert n_chunks % HG_UNROLL == 0 and n_chunks % HG_UNROLL_INTER == 0
    assert n_chunks % HG_PAIR == 0 and t % HG_NORM_ROWS == 0

    def col(k):
        return pl.BlockSpec((1, t, hd), lambda b, h: (b, 0, (k + 1) * n_h + h))

    return pl.pallas_call(
        functools.partial(_hgrn2_kernel, layer=layer, n_chunks=n_chunks, nc_ctx=n_ctx // HG_CHUNK, n_tok=t),
        grid=(bsz, n_h),
        in_specs=[col(0), col(1), col(2), col(3),
                  pl.BlockSpec((depth, hd), lambda b, h: (0, h)),
                  pl.BlockSpec((1, hd), lambda b, h: (0, h))],
        out_specs=pl.BlockSpec((1, t, hd), lambda b, h: (b, 0, h)),
        out_shape=jax.ShapeDtypeStruct((bsz, t, e), BF16),
        scratch_shapes=[pltpu.VMEM((2, t, hd), F32),
                        pltpu.VMEM((2, t, hd), F32),
                        pltpu.VMEM((2, t, hd), BF16),
                        pltpu.VMEM((2, t, hd), BF16),
                        pltpu.VMEM((t, 2 * hd), BF16),
                        pltpu.VMEM((t, 2 * hd), BF16),
                        pltpu.VMEM((n_chunks, hd, 2 * hd), F32),
                        pltpu.VMEM((n_chunks, 1, 2 * hd), F32),
                        pltpu.VMEM((n_chunks, hd, 2 * hd), BF16),
                        pltpu.VMEM((t, hd), F32)],
        compiler_params=_cparams(2),
    )(pz, pz, pz, pz, hg_lb.astype(F32), norm_g.astype(F32).reshape(1, e))


def _z_first(w, e):
    return jnp.concatenate([w[..., -e:], w[..., :-e]], axis=-1)


def kernel(x, c, ctx, c_ctx, ada_w, ada_b, ln_g, ln_b, w_out, attn_w_in, attn_sink, s5_w_in, s5_lam_re, s5_lam_im, s5_log_step, s5_b_re, s5_b_im, s5_c_re, s5_c_im, s5_d, s5_glu_w, s5_glu_b, hg_w_in, hg_lb, hg_norm_g):
    bsz, n_lat, d = x.shape
    n_ctx = ctx.shape[1]
    depth = ada_w.shape[0]
    e = w_out.shape[1]
    alpha = (2.0 * depth) ** 0.25
    t = n_ctx + n_lat
    tm = 768 if t % 768 == 0 else 256

    mod_rows = -(-(bsz + 1) // MOD_ROWS_PAD) * MOD_ROWS_PAD
    c_all = jnp.concatenate([c.astype(F32), c_ctx.astype(F32)[None],
                             jnp.zeros((mod_rows - bsz - 1, d), F32)], axis=0)
    mod = _ada(c_all, ada_w.astype(F32), ada_b.astype(F32))
    mod3 = mod.reshape(depth * mod_rows, 1, 3 * d)

    xa = jnp.concatenate([ctx.astype(F32), x.astype(F32)], axis=1)
    for i in range(depth):
        kind, j = i % N_MIXERS, i // N_MIXERS
        w_in = _z_first((attn_w_in, s5_w_in, hg_w_in)[kind][j], e)
        w_o = w_out[i]
        if kind == 1:
            w_in = jnp.concatenate([_cg_order(w_in[:, :e], 1), _cg_order(w_in[:, e:], 1)], axis=1)
            w_o = _cg_order(w_o, 0)
        pz = _inproj(xa, mod3, i, mod_rows, w_in.astype(BF16), n_ctx, tn=512)
        if kind == 0:
            y = _attention(pz, attn_sink[j], n_ctx, e)
        elif kind == 1:
            ops = _s5_operators(s5_lam_re[j], s5_lam_im[j], s5_log_step[j], s5_b_re[j], s5_b_im[j],
                                s5_c_re[j], s5_c_im[j], s5_d[j])
            glu_w = _cg_order(_cg_order(s5_glu_w[j], 0), 1).astype(BF16)
            y = _s5_mixer(pz, n_ctx, e, ops, glu_w, _cg_order(s5_glu_b[j], 0))
        else:
            y = _hgrn2(pz, hg_lb, hg_norm_g[j], i, n_ctx, e)
        xa = _outproj(y, pz, xa, mod3, i, mod_rows, w_o.astype(BF16), ln_g[i].astype(F32),
                      ln_b[i].astype(F32), n_ctx, alpha, tm)
    return xa[:, n_ctx:].astype(x.dtype)
```

```python
import functools
import math

import jax
import jax.numpy as jnp
from jax import lax
from jax.experimental import pallas as pl
from jax.experimental.pallas import tpu as pltpu

F32 = jnp.float32
BF16 = jnp.bfloat16

N_MIXERS = 3
HEAD_DIM = 128
Q_PER_KV = 4
WINDOW = 128
ATTN_BLOCK = 128
GRID_W = 64
ROPE_THETA = 10000.0
S5_GROUP = 16
S5_STATE = 64
S5_CHUNK = 16
HG_HEAD = 128
HG_CHUNK = 64
HG_UNROLL = 6
HG_UNROLL_INTER = 12
HG_PAIR = 2
HG_NORM_ROWS = 256
NORM_EPS = 1e-5
NEG_INF = -1e30
LOG2E = math.log2(math.e)
ATTN_ROWS = 32
MOD_ROWS_PAD = 8
INPROJ_TN = 1024
VMEM_LIMIT = 56 * 1024 * 1024


def _cparams(n_axes):
    return pltpu.CompilerParams(dimension_semantics=("arbitrary",) * n_axes,
                                vmem_limit_bytes=VMEM_LIMIT)


def _sigmoid(x):
    return 1.0 / (1.0 + jnp.exp(-x))


def _ada_kernel(c_ref, w_ref, b_ref, o_ref):
    cv = c_ref[...]
    s = cv * _sigmoid(cv)
    o_ref[0] = jnp.dot(s, w_ref[0], preferred_element_type=F32,
                       precision=lax.Precision.HIGHEST) + b_ref[0]


def _ada(c_all, ada_w, ada_b):
    depth, d, d3 = ada_w.shape
    rows = c_all.shape[0]
    return pl.pallas_call(
        _ada_kernel,
        grid=(depth, d3 // d),
        in_specs=[pl.BlockSpec((rows, d), lambda i, j: (0, 0)),
                  pl.BlockSpec((1, d, d), lambda i, j: (i, 0, j)),
                  pl.BlockSpec((1, 1, d), lambda i, j: (i, 0, j))],
        out_specs=pl.BlockSpec((1, rows, d), lambda i, j: (i, 0, j)),
        out_shape=jax.ShapeDtypeStruct((depth, rows, d3), F32),
        compiler_params=_cparams(2),
    )(c_all, ada_w, ada_b.reshape(depth, 1, d3))


def _inproj_kernel(x_ref, mb_ref, mc_ref, w_ref, o_ref, h_ref, *, n_ctx, d):
    @pl.when(pl.program_id(1) == 0)
    def _():
        mb = mb_ref[0]
        mc = mc_ref[0]
        h_ref[:n_ctx] = (x_ref[0, :n_ctx] * (1.0 + mc[:, d:2 * d]) + mc[:, :d]).astype(BF16)
        h_ref[n_ctx:] = (x_ref[0, n_ctx:] * (1.0 + mb[:, d:2 * d]) + mb[:, :d]).astype(BF16)

    o_ref[0] = jnp.dot(h_ref[...], w_ref[...], preferred_element_type=F32).astype(o_ref.dtype)


def _inproj(xa, mod3, layer, mod_rows, w_bf, n_ctx, tn):
    bsz, t, d = xa.shape
    n = w_bf.shape[1]
    base = layer * mod_rows
    return pl.pallas_call(
        functools.partial(_inproj_kernel, n_ctx=n_ctx, d=d),
        grid=(bsz, n // tn),
        in_specs=[pl.BlockSpec((1, t, d), lambda b, j: (b, 0, 0)),
                  pl.BlockSpec((1, 1, 3 * d), lambda b, j: (base + b, 0, 0)),
                  pl.BlockSpec((1, 1, 3 * d), lambda b, j: (base + bsz, 0, 0)),
                  pl.BlockSpec((d, tn), lambda b, j: (0, j))],
        out_specs=pl.BlockSpec((1, t, tn), lambda b, j: (b, 0, j)),
        out_shape=jax.ShapeDtypeStruct((bsz, t, n), BF16),
        scratch_shapes=[pltpu.VMEM((t, d), BF16)],
        compiler_params=_cparams(2),
    )(xa, mod3, mod3, w_bf)


def _outproj_kernel(y_ref, z_ref, x_ref, mb_ref, mc_ref, w_ref, g_ref, b_ref, o_ref,
                    *, n_ctx, d, alpha, tm):
    z = z_ref[0]
    a = y_ref[0] * (z * _sigmoid(z))
    br = jnp.dot(a, w_ref[...], preferred_element_type=F32)
    rows = pl.program_id(1) * tm + lax.broadcasted_iota(jnp.int32, (tm, 1), 0)
    gate = jnp.where(rows < n_ctx, mc_ref[0][:, 2 * d:], mb_ref[0][:, 2 * d:])
    v = alpha * x_ref[0] + gate * br
    mu = jnp.mean(v, axis=-1, keepdims=True)
    vc = v - mu
    var = jnp.mean(vc * vc, axis=-1, keepdims=True)
    o_ref[0] = vc * lax.rsqrt(var + NORM_EPS) * g_ref[...] + b_ref[...]


def _outproj(y, pz, xa, mod3, layer, mod_rows, w_bf, ln_g, ln_b, n_ctx, alpha, tm):
    bsz, t, d = xa.shape
    e = w_bf.shape[0]
    base = layer * mod_rows
    return pl.pallas_call(
        functools.partial(_outproj_kernel, n_ctx=n_ctx, d=d, alpha=alpha, tm=tm),
        grid=(bsz, t // tm),
        in_specs=[pl.BlockSpec((1, tm, e), lambda b, i: (b, i, 0)),
                  pl.BlockSpec((1, tm, e), lambda b, i: (b, i, 0)),
                  pl.BlockSpec((1, tm, d), lambda b, i: (b, i, 0)),
                  pl.BlockSpec((1, 1, 3 * d), lambda b, i: (base + b, 0, 0)),
                  pl.BlockSpec((1, 1, 3 * d), lambda b, i: (base + bsz, 0, 0)),
                  pl.BlockSpec((e, d), lambda b, i: (0, 0)),
                  pl.BlockSpec((1, d), lambda b, i: (0, 0)),
                  pl.BlockSpec((1, d), lambda b, i: (0, 0))],
        out_specs=pl.BlockSpec((1, tm, d), lambda b, i: (b, i, 0)),
        out_shape=jax.ShapeDtypeStruct((bsz, t, d), F32),
        compiler_params=_cparams(2),
    )(y, pz, xa, mod3, mod3, w_bf, ln_g.reshape(1, d), ln_b.reshape(1, d))


def _rope_tables(n_lat):
    pos = jnp.arange(n_lat, dtype=jnp.int32)
    row = (pos // GRID_W).astype(F32)
    col = (pos % GRID_W).astype(F32)
    nf = HEAD_DIM // 4
    inv_freq = jnp.power(ROPE_THETA, -jnp.arange(nf, dtype=F32) / nf)
    ang_r = row[:, None] * inv_freq[None, :]
    ang_c = col[:, None] * inv_freq[None, :]
    zeros = jnp.zeros_like(ang_r)
    cos = jnp.concatenate([jnp.cos(ang_r)] * 2 + [jnp.cos(ang_c)] * 2, axis=-1)
    s1 = jnp.concatenate([-jnp.sin(ang_r), zeros, -jnp.sin(ang_c), zeros], axis=-1)
    s2 = jnp.concatenate([zeros, jnp.sin(ang_r), zeros, jnp.sin(ang_c)], axis=-1)
    return cos, s1, s2


def _rope(x, cos, s1, s2):
    quarter = HEAD_DIM // 4
    return (x * cos + pltpu.roll(x, HEAD_DIM - quarter, 1) * s1
            + pltpu.roll(x, quarter, 1) * s2)


def _attn_kernel(sink_ref, q_ref, kv_ref, cq_ref, s1q_ref, s2q_ref, ck_ref, s1k_ref, s2k_ref,
                 o_ref, kr_ref, vx_ref, s_ref, p_ref, e_ref, bias_ref, *, n_ctx, n_lat, n_kv):
    j = pl.program_id(1)
    nb_ctx = n_ctx // ATTN_BLOCK
    kvw = n_kv * HEAD_DIM
    n_win = ATTN_BLOCK + 2 * WINDOW
    qscale = HEAD_DIM ** -0.5 * LOG2E
    rows_q = Q_PER_KV * ATTN_BLOCK
    vxw = 2 * HEAD_DIM
    nt = (((1,), (1,)), ((), ()))

    @pl.when(j == 0)
    def _():
        for h in range(n_kv):
            sl = slice(h * HEAD_DIM, (h + 1) * HEAD_DIM)
            k = kv_ref[0, n_ctx:, sl].astype(F32)
            kr_ref[:, sl] = _rope(k, ck_ref[...], s1k_ref[...], s2k_ref[...]).astype(BF16)
            vx_ref[:, h * vxw:h * vxw + HEAD_DIM] = kv_ref[0, :, kvw + h * HEAD_DIM:kvw + (h + 1) * HEAD_DIM]
            vx_ref[:, h * vxw + HEAD_DIM:(h + 1) * vxw] = jnp.ones((n_ctx + n_lat, HEAD_DIM), BF16)

    def logits(h, qh, kparts):
        cols = []
        off = 0
        for kp in kparts:
            n = kp.shape[0]
            s_ref[h % 2, :, off:off + n] = lax.dot_general(qh, kp, nt, preferred_element_type=F32)
            cols.append((off, n))
            off += n
        return cols

    def softmax_rows(h, cols, use_bias, blocks):
        for rb in blocks:
            r0 = rb * ATTN_ROWS
            sk = sink_ref[h * Q_PER_KV + r0 // ATTN_BLOCK] * LOG2E
            parts = []
            for idx, (o_, n) in enumerate(cols):
                s = s_ref[h % 2, r0:r0 + ATTN_ROWS, o_:o_ + n]
                if use_bias and idx == 0:
                    ql0 = r0 % ATTN_BLOCK
                    s = s + bias_ref[ql0:ql0 + ATTN_ROWS, :]
                parts.append(s)
            m = jnp.max(parts[0], axis=-1, keepdims=True)
            for s in parts[1:]:
                m = jnp.maximum(m, jnp.max(s, axis=-1, keepdims=True))
            m = jnp.maximum(m, sk)
            for (o_, n), s in zip(cols, parts):
                p_ref[h % 2, r0:r0 + ATTN_ROWS, o_:o_ + n] = jnp.exp2(s - m).astype(BF16)
            e_ref[h % 2, r0:r0 + ATTN_ROWS, :] = jnp.exp2(sk - m)

    def weighted_values(h, cols, vparts):
        ox = None
        for (o_, n), vp in zip(cols, vparts):
            part = jnp.dot(p_ref[h % 2, :, o_:o_ + n], vp, preferred_element_type=F32)
            ox = part if ox is None else ox + part
        l = ox[:, HEAD_DIM:HEAD_DIM + 1] + e_ref[h % 2]
        o = ox[:, :HEAD_DIM] * (1.0 / l)
        for g in range(Q_PER_KV):
            hq = h * Q_PER_KV + g
            o_ref[0, :, hq * HEAD_DIM:(hq + 1) * HEAD_DIM] = (
                o[g * ATTN_BLOCK:(g + 1) * ATTN_BLOCK].astype(o_ref.dtype))

    def attend_all(q_fn, k_fn, v_fn, use_bias):
        n_blocks = rows_q // ATTN_ROWS
        cols = [logits(0, q_fn(0), k_fn(0))]
        for h in range(n_kv):
            softmax_rows(h, cols[h], use_bias, range(0, n_blocks // 2))
            if h + 1 < n_kv:
                cols.append(logits(h + 1, q_fn(h + 1), k_fn(h + 1)))
            softmax_rows(h, cols[h], use_bias, range(n_blocks // 2, n_blocks))
            if h >= 1:
                weighted_values(h - 1, cols[h - 1], v_fn(h - 1))
        weighted_values(n_kv - 1, cols[n_kv - 1], v_fn(n_kv - 1))

    def q_head(hq):
        return q_ref[0, :, hq * HEAD_DIM:(hq + 1) * HEAD_DIM].astype(F32)

    def head_slice(h):
        return slice(h * HEAD_DIM, (h + 1) * HEAD_DIM)

    @pl.when(j < nb_ctx)
    def _():
        attend_all(
            lambda h: jnp.concatenate([(q_head(h * Q_PER_KV + g) * qscale).astype(BF16)
                                       for g in range(Q_PER_KV)], axis=0),
            lambda h: [kv_ref[0, :n_ctx, head_slice(h)]],
            lambda h: [vx_ref[:n_ctx, h * vxw:(h + 1) * vxw]], False)

    @pl.when(j >= nb_ctx)
    def _():
        jl = j - nb_ctx
        start = jnp.clip((jl - 1) * ATTN_BLOCK, 0, n_lat - n_win)
        start = pl.multiple_of(start, ATTN_BLOCK)
        delta = jl * ATTN_BLOCK - start
        ql = lax.broadcasted_iota(jnp.int32, (ATTN_BLOCK, n_win), 0)
        kl = lax.broadcasted_iota(jnp.int32, (ATTN_BLOCK, n_win), 1)
        bias_ref[...] = jnp.where(jnp.abs(delta + ql - kl) <= WINDOW, 0.0, NEG_INF)
        cq = cq_ref[...] * qscale
        s1q = s1q_ref[...] * qscale
        s2q = s2q_ref[...] * qscale
        attend_all(
            lambda h: jnp.concatenate([_rope(q_head(h * Q_PER_KV + g), cq, s1q, s2q).astype(BF16)
                                       for g in range(Q_PER_KV)], axis=0),
            lambda h: [kr_ref[pl.ds(start, n_win), head_slice(h)], kv_ref[0, :n_ctx, head_slice(h)]],
            lambda h: [vx_ref[pl.ds(n_ctx + start, n_win), h * vxw:(h + 1) * vxw],
                       vx_ref[:n_ctx, h * vxw:(h + 1) * vxw]], True)


def _attention(pz, sink, n_ctx, e):
    bsz, t, _ = pz.shape
    n_lat = t - n_ctx
    n_q = e // HEAD_DIM
    n_kv = n_q // Q_PER_KV
    kvw = n_kv * HEAD_DIM
    n_win = ATTN_BLOCK + 2 * WINDOW
    rows_q = Q_PER_KV * ATTN_BLOCK
    assert (2 * e) % (2 * kvw) == 0 and n_lat >= n_win
    cos, s1, s2 = _rope_tables(n_lat)
    nb_ctx = n_ctx // ATTN_BLOCK
    qtab = pl.BlockSpec((ATTN_BLOCK, HEAD_DIM), lambda b, j: (jnp.maximum(j - nb_ctx, 0), 0))
    ktab = pl.BlockSpec((n_lat, HEAD_DIM), lambda b, j: (0, 0))
    return pl.pallas_call(
        functools.partial(_attn_kernel, n_ctx=n_ctx, n_lat=n_lat, n_kv=n_kv),
        grid=(bsz, t // ATTN_BLOCK),
        in_specs=[pl.BlockSpec(memory_space=pltpu.SMEM),
                  pl.BlockSpec((1, ATTN_BLOCK, e), lambda b, j: (b, j, 1)),
                  pl.BlockSpec((1, t, 2 * kvw), lambda b, j: (b, 0, 2 * e // (2 * kvw))),
                  qtab, qtab, qtab, ktab, ktab, ktab],
        out_specs=pl.BlockSpec((1, ATTN_BLOCK, e), lambda b, j: (b, j, 0)),
        out_shape=jax.ShapeDtypeStruct((bsz, t, e), BF16),
        scratch_shapes=[pltpu.VMEM((n_lat, kvw), BF16),
                        pltpu.VMEM((t, n_kv * 2 * HEAD_DIM), BF16),
                        pltpu.VMEM((2, rows_q, n_win + n_ctx), F32),
                        pltpu.VMEM((2, rows_q, n_win + n_ctx), BF16),
                        pltpu.VMEM((2, rows_q, 1), F32),
                        pltpu.VMEM((ATTN_BLOCK, n_win), F32)],
        compiler_params=_cparams(2),
    )(sink.astype(F32), pz, pz, cos, s1, s2, cos, s1, s2)


def _cg_order(w, axis):
    axis = axis % w.ndim
    n = w.shape[axis]
    shp = w.shape[:axis] + (n // S5_GROUP, S5_GROUP) + w.shape[axis + 1:]
    return jnp.swapaxes(w.reshape(shp), axis, axis + 1).reshape(w.shape)


def _s5_operators(lam_re, lam_im, log_step, b_re, b_im, c_re, c_im, d_skip):
    n_g = lam_re.shape[1]
    tc = S5_CHUNK
    dt = jnp.exp(log_step.astype(F32))[..., None]
    lr, li = lam_re.astype(F32) * dt, lam_im.astype(F32) * dt
    k = jnp.arange(tc + 1, dtype=F32)[:, None, None, None]
    mag = jnp.exp(lr[None] * k)
    pr, pi = mag * jnp.cos(li[None] * k), mag * jnp.sin(li[None] * k)
    ar1, ai1 = pr[1] - 1.0, pi[1]
    den = lam_re.astype(F32) ** 2 + lam_im.astype(F32) ** 2
    cr = (ar1 * lam_re + ai1 * lam_im) / den
    ci = (ai1 * lam_re - ar1 * lam_im) / den
    br, bi = b_re.astype(F32), b_im.astype(F32)
    bbr = cr[..., None] * br - ci[..., None] * bi
    bbi = cr[..., None] * bi + ci[..., None] * br
    ccr, cci = c_re.astype(F32), c_im.astype(F32)

    def kern(d):
        xr = pr[:tc, d][..., None] * bbr[d][None] - pi[:tc, d][..., None] * bbi[d][None]
        xi = pr[:tc, d][..., None] * bbi[d][None] + pi[:tc, d][..., None] * bbr[d][None]
        return (jnp.einsum('gnp,tgpm->tgnm', ccr[d], xr) - jnp.einsum('gnp,tgpm->tgnm', cci[d], xi))

    kf, kb = kern(0), kern(1)
    s_idx = jnp.arange(tc)[:, None]
    t_idx = jnp.arange(tc)[None, :]
    lag = t_idx - s_idx
    mf = jnp.where((lag >= 0)[:, :, None, None, None], kf[jnp.clip(lag, 0, tc - 1)], 0.0)
    mb = jnp.where((lag <= 0)[:, :, None, None, None], kb[jnp.clip(-lag, 0, tc - 1)], 0.0)
    m = (mf + mb).transpose(2, 4, 0, 3, 1)
    eye = (jnp.eye(S5_GROUP, dtype=F32)[:, None, :, None] * jnp.eye(tc, dtype=F32)[None, :, None, :])
    m = m + eye[None] * d_skip.astype(F32).reshape(n_g, S5_GROUP, 1, 1, 1)
    m = m.reshape(n_g, tc * S5_GROUP, tc * S5_GROUP)

    def qpart(d, pw_r, pw_i):
        qr = pw_r[..., None] * bbr[d][None] - pw_i[..., None] * bbi[d][None]
        qi = pw_r[..., None] * bbi[d][None] + pw_i[..., None] * bbr[d][None]
        return qr.transpose(1, 3, 0, 2), qi.transpose(1, 3, 0, 2)

    qfr, qfi = qpart(0, pr[tc - 1 - jnp.arange(tc), 0], pi[tc - 1 - jnp.arange(tc), 0])
    qbr, qbi = qpart(1, pr[jnp.arange(tc), 1], pi[jnp.arange(tc), 1])
    q = jnp.concatenate([qfr, qbr, qfi, qbi], axis=-1).reshape(n_g, tc * S5_GROUP, 4 * S5_STATE)

    def ppart(d, pw_r, pw_i):
        xr = ccr[d][None] * pw_r[:, :, None, :] - cci[d][None] * pw_i[:, :, None, :]
        xi = ccr[d][None] * pw_i[:, :, None, :] + cci[d][None] * pw_r[:, :, None, :]
        return xr.transpose(1, 3, 2, 0), (-xi).transpose(1, 3, 2, 0)

    pfr, pfi = ppart(0, pr[1 + jnp.arange(tc), 0], pi[1 + jnp.arange(tc), 0])
    pbr, pbi = ppart(1, pr[tc - jnp.arange(tc), 1], pi[tc - jnp.arange(tc), 1])
    p = jnp.concatenate([pfr, pbr, pfi, pbi], axis=1).reshape(n_g, 4 * S5_STATE, tc * S5_GROUP)
    a = jnp.stack([jnp.concatenate([pr[tc, 0], pr[tc, 1]], axis=-1),
                   jnp.concatenate([pi[tc, 0], pi[tc, 1]], axis=-1)], axis=1)
    a = jnp.pad(a, ((0, 0), (0, 6), (0, 0)))
    return m.astype(BF16), q.astype(BF16), p.astype(BF16), a


def _s5_pack_kernel(x_ref, o_ref, f_ref, *, nb, n_g):
    half = S5_GROUP // 2
    for b in range(nb):
        xb = x_ref[b].astype(F32)
        for k in range(2):
            a = jnp.concatenate([xb[:, cc * n_g:(cc + 1) * n_g]
                                 for cc in range(k * half, (k + 1) * half)], axis=0)
            f_ref[k, pl.ds(b, n_g, stride=nb), :] = a.T
    full = jnp.concatenate([f_ref[0], f_ref[1]], axis=1)
    o_ref[...] = full.reshape(n_g, nb, S5_CHUNK * S5_GROUP).astype(o_ref.dtype)


def _s5_pack(pz, e):
    bsz, t, _ = pz.shape
    n_g = e // S5_GROUP
    n_chunks = t // S5_CHUNK
    w = S5_CHUNK * S5_GROUP
    return pl.pallas_call(
        functools.partial(_s5_pack_kernel, nb=bsz, n_g=n_g),
        grid=(n_chunks,),
        in_specs=[pl.BlockSpec((bsz, S5_CHUNK, e), lambda i: (0, i, 1))],
        out_specs=pl.BlockSpec((n_g, bsz, w), lambda i: (0, i, 0)),
        out_shape=jax.ShapeDtypeStruct((n_g, n_chunks * bsz, w), BF16),
        scratch_shapes=[pltpu.VMEM((2, n_g * bsz, w // 2), F32)],
        compiler_params=_cparams(1),
    )(pz)


def _s5_kernel(u_ref, m_ref, q_ref, p_ref, a_ref, y_ref, s_ref, h_ref, *, nb, n_chunks, nc_ctx):
    ns = S5_STATE
    u = u_ref[0]
    s_ref[...] = jnp.dot(u, q_ref[0], preferred_element_type=F32)
    ar = a_ref[0, 0:1, :]
    ai = a_ref[0, 1:2, :]
    is_fwd = lax.broadcasted_iota(jnp.int32, (nb, 2 * ns), 1) < ns

    def step(i, carry):
        hr, hi = carry
        cb = jnp.where(i < nc_ctx, nc_ctx - 1 - i, n_chunks - 1 - (i - nc_ctx))
        rf = pl.multiple_of(i * nb, nb)
        rb = pl.multiple_of(cb * nb, nb)
        h_ref[pl.ds(rf, nb), 0:ns] = hr[:, 0:ns]
        h_ref[pl.ds(rb, nb), ns:2 * ns] = hr[:, ns:2 * ns]
        h_ref[pl.ds(rf, nb), 2 * ns:3 * ns] = hi[:, 0:ns]
        h_ref[pl.ds(rb, nb), 3 * ns:4 * ns] = hi[:, ns:2 * ns]
        sf = s_ref[pl.ds(rf, nb), :]
        sb = s_ref[pl.ds(rb, nb), :]
        sr = jnp.where(is_fwd, sf[:, :2 * ns], sb[:, :2 * ns])
        si = jnp.where(is_fwd, sf[:, 2 * ns:], sb[:, 2 * ns:])
        return ar * hr - ai * hi + sr, ar * hi + ai * hr + si

    zero = jnp.zeros((nb, 2 * ns), F32)
    lax.fori_loop(0, n_chunks, step, (zero, zero))
    y = jnp.dot(u, m_ref[0], preferred_element_type=F32)
    y = y + jnp.dot(h_ref[...].astype(BF16), p_ref[0], preferred_element_type=F32)
    y_ref[0] = y.astype(y_ref.dtype)


def _s5_scan(uf, m, q, p, a, nb, nc_ctx):
    n_g, rows, w = uf.shape
    n_chunks = rows // nb
    wspec = pl.BlockSpec((1, w, w), lambda g: (g, 0, 0))
    return pl.pallas_call(
        functools.partial(_s5_kernel, nb=nb, n_chunks=n_chunks, nc_ctx=nc_ctx),
        grid=(n_g,),
        in_specs=[pl.BlockSpec((1, rows, w), lambda g: (g, 0, 0)), wspec, wspec, wspec,
                  pl.BlockSpec((1, 8, 2 * S5_STATE), lambda g: (g, 0, 0))],
        out_specs=pl.BlockSpec((1, rows, w), lambda g: (g, 0, 0)),
        out_shape=jax.ShapeDtypeStruct((n_g, rows, w), BF16),
        scratch_shapes=[pltpu.VMEM((rows, w), F32), pltpu.VMEM((rows, w), F32)],
        compiler_params=_cparams(1),
    )(uf, m, q, p, a)


def _s5_glu_kernel(y_ref, w_ref, b_ref, o_ref, f_ref, t_ref, *, nb, n_g):
    w = S5_CHUNK * S5_GROUP
    half = S5_GROUP // 2
    yv = y_ref[...].astype(F32).reshape(n_g * nb, w)
    for k in range(2):
        f_ref[k] = yv[:, k * (w // 2):(k + 1) * (w // 2)]
    for b in range(nb):
        for k in range(2):
            a = f_ref[k, pl.ds(b, n_g, stride=nb), :].T
            for cc in range(half):
                ch = k * half + cc
                t_ref[b * S5_CHUNK:(b + 1) * S5_CHUNK, ch * n_g:(ch + 1) * n_g] = (
                    a[cc * S5_CHUNK:(cc + 1) * S5_CHUNK, :])
    t = t_ref[...]
    g = 0.5 * t * (1.0 + jnp.tanh(math.sqrt(2.0 / math.pi) * (t + 0.044715 * t * t * t)))
    lin = jnp.dot(g.astype(BF16), w_ref[...], preferred_element_type=F32) + b_ref[...]
    out = (g * _sigmoid(lin)).astype(o_ref.dtype)
    o_ref[...] = out.reshape(nb, S5_CHUNK, out.shape[-1])


def _s5_glu(yf, w_bf, bias, bsz):
    n_g, rows, w = yf.shape
    e = n_g * S5_GROUP
    n_chunks = rows // bsz
    return pl.pallas_call(
        functools.partial(_s5_glu_kernel, nb=bsz, n_g=n_g),
        grid=(n_chunks,),
        in_specs=[pl.BlockSpec((n_g, bsz, w), lambda i: (0, i, 0)),
                  pl.BlockSpec((e, e), lambda i: (0, 0)),
                  pl.BlockSpec((1, e), lambda i: (0, 0))],
        out_specs=pl.BlockSpec((bsz, S5_CHUNK, e), lambda i: (0, i, 0)),
        out_shape=jax.ShapeDtypeStruct((bsz, n_chunks * S5_CHUNK, e), BF16),
        scratch_shapes=[pltpu.VMEM((2, n_g * bsz, w // 2), F32), pltpu.VMEM((bsz * S5_CHUNK, e), F32)],
        compiler_params=_cparams(1),
    )(yf, w_bf, bias.astype(F32).reshape(1, e))


def _s5_mixer(pz, n_ctx, e, ops, glu_w_bf, glu_b):
    bsz = pz.shape[0]
    m, q, p, a = ops
    uf = _s5_pack(pz, e)
    yf = _s5_scan(uf, m, q, p, a, bsz, n_ctx // S5_CHUNK)
    return _s5_glu(yf, glu_w_bf, glu_b, bsz)


def _cumsum_rows(x, reverse):
    n = x.shape[0]
    row = lax.broadcasted_iota(jnp.int32, x.shape, 0)
    s = 1
    while s < n:
        if reverse:
            x = x + jnp.where(row < n - s, pltpu.roll(x, n - s, 0), 0.0)
        else:
            x = x + jnp.where(row >= s, pltpu.roll(x, s, 0), 0.0)
        s *= 2
    return x


def _hgrn_kernel(q_ref, ff_ref, fb_ref, v_ref, lb_ref, ng_ref, o_ref,
                 ds_ref, dec_ref, st_ref, qd_ref, oi_ref, *, layer, n_chunks, nc_ctx, n_tok):
    c = HG_CHUNK
    hd = HG_HEAD
    nt = (((1,), (1,)), ((), ()))
    tn = (((0,), (0,)), ((), ()))

    lbw = lb_ref[...].astype(F32)
    ew = jnp.exp(lbw - jnp.max(lbw, axis=0, keepdims=True))
    lb = jnp.sum(ew[1:layer + 1], axis=0, keepdims=True) / jnp.sum(ew, axis=0, keepdims=True)

    ti = lax.broadcasted_iota(jnp.int32, (c, 2 * c), 0)
    si = lax.broadcasted_iota(jnp.int32, (c, 2 * c), 1)
    m_fwd = si <= ti
    m_bwd = si - c >= ti

    def chunk_a(ci):
        r0 = pl.multiple_of(ci * c, c)
        q = q_ref[0, pl.ds(r0, c), :].astype(F32)
        v = v_ref[0, pl.ds(r0, c), :]
        qes, kes, kds = [], [], []
        for d, f_ref in enumerate((ff_ref, fb_ref)):
            fl = f_ref[0, pl.ds(r0, c), :].astype(F32)
            f = lb + (1.0 - lb) * _sigmoid(fl)
            b = _cumsum_rows(jnp.log(f), reverse=bool(d))
            mid = c // 2 if d == 0 else c // 2 - 1
            last = c - 1 if d == 0 else 0
            ref = b[mid:mid + 1]
            b_last = b[last:last + 1]
            qe = q * jnp.exp(b - ref)
            ke = (1.0 - f) * jnp.exp(ref - b)
            qes.append(qe.astype(BF16))
            kes.append(ke.astype(BF16))
            qd_ref[pl.ds(r0, c), d * hd:(d + 1) * hd] = (qe * jnp.exp(ref)).astype(BF16)
            kds.append((ke * jnp.exp(b_last - ref)).astype(BF16))
            dec_ref[ci, :, d * hd:(d + 1) * hd] = jnp.exp(b_last)
        a = lax.dot_general(jnp.concatenate(qes, axis=0), jnp.concatenate(kes, axis=0), nt,
                            preferred_element_type=F32)
        att = jnp.where(m_fwd, a[:c], jnp.where(m_bwd, a[c:], 0.0))
        oi_ref[pl.ds(r0, c), :] = jnp.dot(att.astype(BF16), jnp.concatenate([v, v], axis=0),
                                          preferred_element_type=F32)
        ds_ref[ci] = lax.dot_general(v, jnp.concatenate(kds, axis=1), tn,
                                     preferred_element_type=F32)

    def loop_a(i, carry):
        for u in range(HG_UNROLL):
            chunk_a(i * HG_UNROLL + u)
        return carry

    lax.fori_loop(0, n_chunks // HG_UNROLL, loop_a, 0)

    def scan_b(i, carry):
        sf, sb = carry
        cb = jnp.where(i < nc_ctx, nc_ctx - 1 - i, n_chunks - 1 - (i - nc_ctx))
        st_ref[i, :, 0:hd] = sf.astype(BF16)
        st_ref[cb, :, hd:2 * hd] = sb.astype(BF16)
        sf = sf * dec_ref[i, :, 0:hd] + ds_ref[i, :, 0:hd]
        sb = sb * dec_ref[cb, :, hd:2 * hd] + ds_ref[cb, :, hd:2 * hd]
        return sf, sb

    zero = jnp.zeros((hd, hd), F32)
    lax.fori_loop(0, n_chunks, scan_b, (zero, zero))

    def chunk_c(ci):
        r0 = pl.multiple_of(ci * c, c)
        oi_ref[pl.ds(r0, c), :] += lax.dot_general(
            qd_ref[pl.ds(r0, c), :], st_ref[ci], nt, preferred_element_type=F32)

    def loop_c(i, carry):
        for u in range(HG_UNROLL):
            chunk_c(i * HG_UNROLL + u)
        return carry

    lax.fori_loop(0, n_chunks // HG_UNROLL, loop_c, 0)

    gn = ng_ref[...].astype(F32)
    for r0 in range(0, n_tok, HG_NORM_ROWS):
        o = oi_ref[r0:r0 + HG_NORM_ROWS, :]
        o = o * lax.rsqrt(jnp.mean(o * o, axis=-1, keepdims=True) + NORM_EPS) * gn
        o_ref[0, r0:r0 + HG_NORM_ROWS, :] = o.astype(o_ref.dtype)


def _hgrn(pz, hg_lb, norm_g, layer, n_ctx, e):
    bsz, t, _ = pz.shape
    n_h = e // HG_HEAD
    n_chunks = t // HG_CHUNK
    depth = hg_lb.shape[0]
    hd = HG_HEAD
    assert n_chunks % HG_UNROLL == 0 and t % HG_NORM_ROWS == 0

    def col(k):
        return pl.BlockSpec((1, t, hd), lambda b, h: (b, 0, (k + 1) * n_h + h))

    return pl.pallas_call(
        functools.partial(_hgrn_kernel, layer=layer, n_chunks=n_chunks, nc_ctx=n_ctx // HG_CHUNK, n_tok=t),
        grid=(bsz, n_h),
        in_specs=[col(0), col(1), col(2), col(3),
                  pl.BlockSpec((depth, hd), lambda b, h: (0, h)),
                  pl.BlockSpec((1, hd), lambda b, h: (0, h))],
        out_specs=pl.BlockSpec((1, t, hd), lambda b, h: (b, 0, h)),
        out_shape=jax.ShapeDtypeStruct((bsz, t, e), BF16),
        scratch_shapes=[pltpu.VMEM((n_chunks, hd, 2 * hd), F32),
                        pltpu.VMEM((n_chunks, 1, 2 * hd), F32),
                        pltpu.VMEM((n_chunks, hd, 2 * hd), BF16),
                        pltpu.VMEM((t, 2 * hd), BF16),
                        pltpu.VMEM((t, hd), F32)],
        compiler_params=_cparams(2),
    )(pz, pz, pz, pz, hg_lb.astype(F32), norm_g.astype(F32).reshape(1, e))


def _cumsum_rows_multi(xs, reverse):
    n = xs[0].shape[0]
    row = lax.broadcasted_iota(jnp.int32, xs[0].shape, 0)
    s = 1
    while s < n:
        nxt = []
        for x, rev in zip(xs, reverse):
            if rev:
                nxt.append(x + jnp.where(row < n - s, pltpu.roll(x, n - s, 0), 0.0))
            else:
                nxt.append(x + jnp.where(row >= s, pltpu.roll(x, s, 0), 0.0))
        xs = nxt
        s *= 2
    return xs


def _hgrn2_kernel(q_ref, ff_ref, fb_ref, v_ref, lb_ref, ng_ref, o_ref,
                  g_ref, k_ref, qe_ref, ke_ref, qd_ref, kd_ref, ds_ref, dec_ref, st_ref, oi_ref,
                  *, layer, n_chunks, nc_ctx, n_tok):
    c = HG_CHUNK
    hd = HG_HEAD
    nt = (((1,), (1,)), ((), ()))
    tn = (((0,), (0,)), ((), ()))
    f_refs = (ff_ref, fb_ref)

    lbw = lb_ref[...].astype(F32)
    ew = jnp.exp(lbw - jnp.max(lbw, axis=0, keepdims=True))
    lb = jnp.sum(ew[1:layer + 1], axis=0, keepdims=True) / jnp.sum(ew, axis=0, keepdims=True)

    def gates(ci):
        r0 = pl.multiple_of(ci * c, c)
        for d in range(2):
            fl = f_refs[d][0, pl.ds(r0, c), :].astype(F32)
            f = lb + (1.0 - lb) * _sigmoid(fl)
            g_ref[d, pl.ds(r0, c), :] = jnp.log2(f)
            k_ref[d, pl.ds(r0, c), :] = 1.0 - f

    def cumdecay(cis):
        r0s = [pl.multiple_of(ci * c, c) for ci in cis]
        xs = [g_ref[d, pl.ds(r0, c), :] for r0 in r0s for d in range(2)]
        xs = _cumsum_rows_multi(xs, [False, True] * len(cis))
        k = 0
        for r0 in r0s:
            for d in range(2):
                g_ref[d, pl.ds(r0, c), :] = xs[k]
                k += 1

    def decayed(ci):
        r0 = pl.multiple_of(ci * c, c)
        q = q_ref[0, pl.ds(r0, c), :].astype(F32)
        for d in range(2):
            mid = c // 2 if d == 0 else c // 2 - 1
            last = c - 1 if d == 0 else 0
            b = g_ref[d, pl.ds(r0, c), :]
            ref = g_ref[d, pl.ds(r0 + mid, 1), :]
            b_last = g_ref[d, pl.ds(r0 + last, 1), :]
            qe = q * jnp.exp2(b - ref)
            ke = k_ref[d, pl.ds(r0, c), :] * jnp.exp2(ref - b)
            qe_ref[d, pl.ds(r0, c), :] = qe.astype(BF16)
            ke_ref[d, pl.ds(r0, c), :] = ke.astype(BF16)
            qd_ref[pl.ds(r0, c), d * hd:(d + 1) * hd] = (qe * jnp.exp2(ref)).astype(BF16)
            kd_ref[pl.ds(r0, c), d * hd:(d + 1) * hd] = (ke * jnp.exp2(b_last - ref)).astype(BF16)
            dec_ref[ci, :, d * hd:(d + 1) * hd] = jnp.exp2(b_last)

    def prepare(cis):
        for ci in cis:
            gates(ci)
        cumdecay(cis)
        for ci in cis:
            decayed(ci)

    ti = lax.broadcasted_iota(jnp.int32, (c, 2 * c), 0)
    si = lax.broadcasted_iota(jnp.int32, (c, 2 * c), 1)
    m_fwd = si <= ti
    m_bwd = si - c >= ti

    def logits(cis):
        r0s = [pl.multiple_of(ci * c, c) for ci in cis]
        vs = [v_ref[0, pl.ds(r0, c), :] for r0 in r0s]
        aa = [lax.dot_general(
            jnp.concatenate([qe_ref[0, pl.ds(r0, c), :], qe_ref[1, pl.ds(r0, c), :]], axis=0),
            jnp.concatenate([ke_ref[0, pl.ds(r0, c), :], ke_ref[1, pl.ds(r0, c), :]], axis=0),
            nt, preferred_element_type=F32) for r0 in r0s]
        dss = [lax.dot_general(v, kd_ref[pl.ds(r0, c), :], tn, preferred_element_type=F32)
               for v, r0 in zip(vs, r0s)]
        return r0s, vs, aa, dss

    def values(cis, vs, aa, dss):
        atts = [jnp.where(m_fwd, a[:c], jnp.where(m_bwd, a[c:], 0.0)).astype(BF16) for a in aa]
        ois = [jnp.dot(att, jnp.concatenate([v, v], axis=0), preferred_element_type=F32)
               for att, v in zip(atts, vs)]
        for ci, ds in zip(cis, dss):
            ds_ref[ci] = ds
        return tuple(ois)

    def store_intra(cis, ois):
        for ci, oi in zip(cis, ois):
            oi_ref[pl.ds(pl.multiple_of(ci * c, c), c), :] = oi

    n_pairs = n_chunks // HG_PAIR

    def pair(t):
        return [t * HG_PAIR + u for u in range(HG_PAIR)]

    def stage(t, pending, with_next):
        cur = pair(t)
        _, vs, aa, dss = logits(cur)
        if pending is not None:
            store_intra(pair(t - 1), pending)
        if with_next:
            nxt = pair(t + 1)
            for ci in nxt:
                gates(ci)
            cumdecay(nxt)
        ois = values(cur, vs, aa, dss)
        if with_next:
            for ci in nxt:
                decayed(ci)
        return ois

    prepare(pair(0))
    pending = stage(0, None, True)
    for t in range(1, n_pairs - 1):
        pending = stage(t, pending, True)
    pending = stage(n_pairs - 1, pending, False)
    store_intra(pair(n_pairs - 1), pending)

    def scan_b(i, carry):
        sf, sb = carry
        cb = jnp.where(i < nc_ctx, nc_ctx - 1 - i, n_chunks - 1 - (i - nc_ctx))
        st_ref[i, :, 0:hd] = sf.astype(BF16)
        st_ref[cb, :, hd:2 * hd] = sb.astype(BF16)
        sf = sf * dec_ref[i, :, 0:hd] + ds_ref[i, :, 0:hd]
        sb = sb * dec_ref[cb, :, hd:2 * hd] + ds_ref[cb, :, hd:2 * hd]
        return sf, sb

    zero = jnp.zeros((hd, hd), F32)
    lax.fori_loop(0, n_chunks, scan_b, (zero, zero))

    def inter(i, carry):
        cis = [i * HG_UNROLL_INTER + u for u in range(HG_UNROLL_INTER)]
        r0s = [pl.multiple_of(ci * c, c) for ci in cis]
        os_ = [lax.dot_general(qd_ref[pl.ds(r0, c), :], st_ref[ci], nt, preferred_element_type=F32)
               for ci, r0 in zip(cis, r0s)]
        for r0, o in zip(r0s, os_):
            oi_ref[pl.ds(r0, c), :] += o
        return carry

    lax.fori_loop(0, n_chunks // HG_UNROLL_INTER, inter, 0)

    gn = ng_ref[...].astype(F32)
    for r0 in range(0, n_tok, HG_NORM_ROWS):
        o = oi_ref[r0:r0 + HG_NORM_ROWS, :]
        o = o * lax.rsqrt(jnp.mean(o * o, axis=-1, keepdims=True) + NORM_EPS) * gn
        o_ref[0, r0:r0 + HG_NORM_ROWS, :] = o.astype(o_ref.dtype)


def _hgrn2(pz, hg_lb, norm_g, layer, n_ctx, e):
    bsz, t, _ = pz.shape
    n_h = e // HG_HEAD
    n_chunks = t // HG_CHUNK
    depth = hg_lb.shape[0]
    hd = HG_HEAD
    assert n_chunks % HG_PAIR == 0 and n_chunks // HG_PAIR >= 3 and n_chunks % HG_UNROLL_INTER == 0
    assert t % HG_NORM_ROWS == 0

    def col(k):
        return pl.BlockSpec((1, t, hd), lambda b, h: (b, 0, (k + 1) * n_h + h))

    return pl.pallas_call(
        functools.partial(_hgrn2_kernel, layer=layer, n_chunks=n_chunks, nc_ctx=n_ctx // HG_CHUNK, n_tok=t),
        grid=(bsz, n_h),
        in_specs=[col(0), col(1), col(2), col(3),
                  pl.BlockSpec((depth, hd), lambda b, h: (0, h)),
                  pl.BlockSpec((1, hd), lambda b, h: (0, h))],
        out_specs=pl.BlockSpec((1, t, hd), lambda b, h: (b, 0, h)),
        out_shape=jax.ShapeDtypeStruct((bsz, t, e), BF16),
        scratch_shapes=[pltpu.VMEM((2, t, hd), F32),
                        pltpu.VMEM((2, t, hd), F32),
                        pltpu.VMEM((2, t, hd), BF16),
                        pltpu.VMEM((2, t, hd), BF16),
                        pltpu.VMEM((t, 2 * hd), BF16),
                        pltpu.VMEM((t, 2 * hd), BF16),
                        pltpu.VMEM((n_chunks, hd, 2 * hd), F32),
                        pltpu.VMEM((n_chunks, 1, 2 * hd), F32),
                        pltpu.VMEM((n_chunks, hd, 2 * hd), BF16),
                        pltpu.VMEM((t, hd), F32)],
        compiler_params=_cparams(2),
    )(pz, pz, pz, pz, hg_lb.astype(F32), norm_g.astype(F32).reshape(1, e))


def _z_first(w, e):
    return jnp.concatenate([w[..., -e:], w[..., :-e]], axis=-1)


def kernel(x, c, ctx, c_ctx, ada_w, ada_b, ln_g, ln_b, w_out, attn_w_in, attn_sink, s5_w_in, s5_lam_re, s5_lam_im, s5_log_step, s5_b_re, s5_b_im, s5_c_re, s5_c_im, s5_d, s5_glu_w, s5_glu_b, hg_w_in, hg_lb, hg_norm_g):
    bsz, n_lat, d = x.shape
    n_ctx = ctx.shape[1]
    depth = ada_w.shape[0]
    e = w_out.shape[1]
    alpha = (2.0 * depth) ** 0.25
    t = n_ctx + n_lat
    tm = 768 if t % 768 == 0 else 256

    mod_rows = -(-(bsz + 1) // MOD_ROWS_PAD) * MOD_ROWS_PAD
    c_all = jnp.concatenate([c.astype(F32), c_ctx.astype(F32)[None],
                             jnp.zeros((mod_rows - bsz - 1, d), F32)], axis=0)
    mod = _ada(c_all, ada_w.astype(F32), ada_b.astype(F32))
    mod3 = mod.reshape(depth * mod_rows, 1, 3 * d)

    xa = jnp.concatenate([ctx.astype(F32), x.astype(F32)], axis=1)
    for i in range(depth):
        kind, j = i % N_MIXERS, i // N_MIXERS
        w_in = _z_first((attn_w_in, s5_w_in, hg_w_in)[kind][j], e)
        w_o = w_out[i]
        if kind == 1:
            w_in = jnp.concatenate([_cg_order(w_in[:, :e], 1), _cg_order(w_in[:, e:], 1)], axis=1)
            w_o = _cg_order(w_o, 0)
        pz = _inproj(xa, mod3, i, mod_rows, w_in.astype(BF16), n_ctx, tn=INPROJ_TN)
        if kind == 0:
            y = _attention(pz, attn_sink[j], n_ctx, e)
        elif kind == 1:
            ops = _s5_operators(s5_lam_re[j], s5_lam_im[j], s5_log_step[j], s5_b_re[j], s5_b_im[j],
                                s5_c_re[j], s5_c_im[j], s5_d[j])
            glu_w = _cg_order(_cg_order(s5_glu_w[j], 0), 1).astype(BF16)
            y = _s5_mixer(pz, n_ctx, e, ops, glu_w, _cg_order(s5_glu_b[j], 0))
        else:
            y = _hgrn2(pz, hg_lb, hg_norm_g[j], i, n_ctx, e)
        xa = _outproj(y, pz, xa, mod3, i, mod_rows, w_o.astype(BF16), ln_g[i].astype(F32),
                      ln_b[i].astype(F32), n_ctx, alpha, tm)
    return xa[:, n_ctx:].astype(x.dtype)
```

```python
import functools
import math

import jax
import jax.numpy as jnp
from jax import lax
from jax.experimental import pallas as pl
from jax.experimental.pallas import tpu as pltpu

F32 = jnp.float32
BF16 = jnp.bfloat16

N_MIXERS = 3
HEAD_DIM = 128
Q_PER_KV = 4
WINDOW = 128
ATTN_BLOCK = 128
GRID_W = 64
ROPE_THETA = 10000.0
S5_GROUP = 16
S5_STATE = 64
S5_CHUNK = 16
HG_HEAD = 128
HG_CHUNK = 64
HG_UNROLL = 6
HG_UNROLL_INTER = 12
HG_PAIR = 2
HG_NORM_ROWS = 256
NORM_EPS = 1e-5
NEG_INF = -1e30
LOG2E = math.log2(math.e)
ATTN_ROWS = 32
MOD_ROWS_PAD = 8
INPROJ_TN = 1024
GLU_PIECES = 8
VMEM_LIMIT = 56 * 1024 * 1024


def _cparams(n_axes):
    return pltpu.CompilerParams(dimension_semantics=("arbitrary",) * n_axes,
                                vmem_limit_bytes=VMEM_LIMIT)


def _sigmoid(x):
    return 1.0 / (1.0 + jnp.exp(-x))


def _ada_kernel(c_ref, w_ref, b_ref, o_ref):
    cv = c_ref[...]
    s = cv * _sigmoid(cv)
    o_ref[0] = jnp.dot(s, w_ref[0], preferred_element_type=F32,
                       precision=lax.Precision.HIGHEST) + b_ref[0]


def _ada(c_all, ada_w, ada_b):
    depth, d, d3 = ada_w.shape
    rows = c_all.shape[0]
    return pl.pallas_call(
        _ada_kernel,
        grid=(depth, d3 // d),
        in_specs=[pl.BlockSpec((rows, d), lambda i, j: (0, 0)),
                  pl.BlockSpec((1, d, d), lambda i, j: (i, 0, j)),
                  pl.BlockSpec((1, 1, d), lambda i, j: (i, 0, j))],
        out_specs=pl.BlockSpec((1, rows, d), lambda i, j: (i, 0, j)),
        out_shape=jax.ShapeDtypeStruct((depth, rows, d3), F32),
        compiler_params=_cparams(2),
    )(c_all, ada_w, ada_b.reshape(depth, 1, d3))


def _inproj_kernel(x_ref, mb_ref, mc_ref, w_ref, o_ref, h_ref, *, n_ctx, d):
    @pl.when(pl.program_id(1) == 0)
    def _():
        mb = mb_ref[0]
        mc = mc_ref[0]
        h_ref[:n_ctx] = (x_ref[0, :n_ctx] * (1.0 + mc[:, d:2 * d]) + mc[:, :d]).astype(BF16)
        h_ref[n_ctx:] = (x_ref[0, n_ctx:] * (1.0 + mb[:, d:2 * d]) + mb[:, :d]).astype(BF16)

    o_ref[0] = jnp.dot(h_ref[...], w_ref[...], preferred_element_type=F32).astype(o_ref.dtype)


def _inproj(xa, mod3, layer, mod_rows, w_bf, n_ctx, tn):
    bsz, t, d = xa.shape
    n = w_bf.shape[1]
    base = layer * mod_rows
    return pl.pallas_call(
        functools.partial(_inproj_kernel, n_ctx=n_ctx, d=d),
        grid=(bsz, n // tn),
        in_specs=[pl.BlockSpec((1, t, d), lambda b, j: (b, 0, 0)),
                  pl.BlockSpec((1, 1, 3 * d), lambda b, j: (base + b, 0, 0)),
                  pl.BlockSpec((1, 1, 3 * d), lambda b, j: (base + bsz, 0, 0)),
                  pl.BlockSpec((d, tn), lambda b, j: (0, j))],
        out_specs=pl.BlockSpec((1, t, tn), lambda b, j: (b, 0, j)),
        out_shape=jax.ShapeDtypeStruct((bsz, t, n), BF16),
        scratch_shapes=[pltpu.VMEM((t, d), BF16)],
        compiler_params=_cparams(2),
    )(xa, mod3, mod3, w_bf)


def _outproj_kernel(y_ref, z_ref, x_ref, mb_ref, mc_ref, w_ref, g_ref, b_ref, o_ref,
                    *, n_ctx, d, alpha, tm):
    z = z_ref[0]
    a = y_ref[0] * (z * _sigmoid(z))
    br = jnp.dot(a, w_ref[...], preferred_element_type=F32)
    rows = pl.program_id(1) * tm + lax.broadcasted_iota(jnp.int32, (tm, 1), 0)
    gate = jnp.where(rows < n_ctx, mc_ref[0][:, 2 * d:], mb_ref[0][:, 2 * d:])
    v = alpha * x_ref[0] + gate * br
    mu = jnp.mean(v, axis=-1, keepdims=True)
    vc = v - mu
    var = jnp.mean(vc * vc, axis=-1, keepdims=True)
    o_ref[0] = vc * lax.rsqrt(var + NORM_EPS) * g_ref[...] + b_ref[...]


def _outproj(y, pz, xa, mod3, layer, mod_rows, w_bf, ln_g, ln_b, n_ctx, alpha, tm):
    bsz, t, d = xa.shape
    e = w_bf.shape[0]
    base = layer * mod_rows
    return pl.pallas_call(
        functools.partial(_outproj_kernel, n_ctx=n_ctx, d=d, alpha=alpha, tm=tm),
        grid=(bsz, t // tm),
        in_specs=[pl.BlockSpec((1, tm, e), lambda b, i: (b, i, 0)),
                  pl.BlockSpec((1, tm, e), lambda b, i: (b, i, 0)),
                  pl.BlockSpec((1, tm, d), lambda b, i: (b, i, 0)),
                  pl.BlockSpec((1, 1, 3 * d), lambda b, i: (base + b, 0, 0)),
                  pl.BlockSpec((1, 1, 3 * d), lambda b, i: (base + bsz, 0, 0)),
                  pl.BlockSpec((e, d), lambda b, i: (0, 0)),
                  pl.BlockSpec((1, d), lambda b, i: (0, 0)),
                  pl.BlockSpec((1, d), lambda b, i: (0, 0))],
        out_specs=pl.BlockSpec((1, tm, d), lambda b, i: (b, i, 0)),
        out_shape=jax.ShapeDtypeStruct((bsz, t, d), F32),
        compiler_params=_cparams(2),
    )(y, pz, xa, mod3, mod3, w_bf, ln_g.reshape(1, d), ln_b.reshape(1, d))


def _rope_tables(n_lat):
    pos = jnp.arange(n_lat, dtype=jnp.int32)
    row = (pos // GRID_W).astype(F32)
    col = (pos % GRID_W).astype(F32)
    nf = HEAD_DIM // 4
    inv_freq = jnp.power(ROPE_THETA, -jnp.arange(nf, dtype=F32) / nf)
    ang_r = row[:, None] * inv_freq[None, :]
    ang_c = col[:, None] * inv_freq[None, :]
    zeros = jnp.zeros_like(ang_r)
    cos = jnp.concatenate([jnp.cos(ang_r)] * 2 + [jnp.cos(ang_c)] * 2, axis=-1)
    s1 = jnp.concatenate([-jnp.sin(ang_r), zeros, -jnp.sin(ang_c), zeros], axis=-1)
    s2 = jnp.concatenate([zeros, jnp.sin(ang_r), zeros, jnp.sin(ang_c)], axis=-1)
    return cos, s1, s2


def _rope(x, cos, s1, s2):
    quarter = HEAD_DIM // 4
    return (x * cos + pltpu.roll(x, HEAD_DIM - quarter, 1) * s1
            + pltpu.roll(x, quarter, 1) * s2)


def _attn_kernel(sink_ref, q_ref, kv_ref, cq_ref, s1q_ref, s2q_ref, ck_ref, s1k_ref, s2k_ref,
                 o_ref, kr_ref, vx_ref, s_ref, p_ref, e_ref, bias_ref, *, n_ctx, n_lat, n_kv):
    j = pl.program_id(1)
    nb_ctx = n_ctx // ATTN_BLOCK
    kvw = n_kv * HEAD_DIM
    n_win = ATTN_BLOCK + 2 * WINDOW
    qscale = HEAD_DIM ** -0.5 * LOG2E
    rows_q = Q_PER_KV * ATTN_BLOCK
    vxw = 2 * HEAD_DIM
    nt = (((1,), (1,)), ((), ()))

    @pl.when(j == 0)
    def _():
        for h in range(n_kv):
            sl = slice(h * HEAD_DIM, (h + 1) * HEAD_DIM)
            k = kv_ref[0, n_ctx:, sl].astype(F32)
            kr_ref[:, sl] = _rope(k, ck_ref[...], s1k_ref[...], s2k_ref[...]).astype(BF16)
            vx_ref[:, h * vxw:h * vxw + HEAD_DIM] = kv_ref[0, :, kvw + h * HEAD_DIM:kvw + (h + 1) * HEAD_DIM]
            vx_ref[:, h * vxw + HEAD_DIM:(h + 1) * vxw] = jnp.ones((n_ctx + n_lat, HEAD_DIM), BF16)

    def logits(h, qh, kparts):
        cols = []
        off = 0
        for kp in kparts:
            n = kp.shape[0]
            s_ref[h % 2, :, off:off + n] = lax.dot_general(qh, kp, nt, preferred_element_type=F32)
            cols.append((off, n))
            off += n
        return cols

    def softmax_rows(h, cols, use_bias, blocks):
        for rb in blocks:
            r0 = rb * ATTN_ROWS
            sk = sink_ref[h * Q_PER_KV + r0 // ATTN_BLOCK] * LOG2E
            parts = []
            for idx, (o_, n) in enumerate(cols):
                s = s_ref[h % 2, r0:r0 + ATTN_ROWS, o_:o_ + n]
                if use_bias and idx == 0:
                    ql0 = r0 % ATTN_BLOCK
                    s = s + bias_ref[ql0:ql0 + ATTN_ROWS, :]
                parts.append(s)
            m = jnp.max(parts[0], axis=-1, keepdims=True)
            for s in parts[1:]:
                m = jnp.maximum(m, jnp.max(s, axis=-1, keepdims=True))
            m = jnp.maximum(m, sk)
            for (o_, n), s in zip(cols, parts):
                p_ref[h % 2, r0:r0 + ATTN_ROWS, o_:o_ + n] = jnp.exp2(s - m).astype(BF16)
            e_ref[h % 2, r0:r0 + ATTN_ROWS, :] = jnp.exp2(sk - m)

    def weighted_values(h, cols, vparts):
        ox = None
        for (o_, n), vp in zip(cols, vparts):
            part = jnp.dot(p_ref[h % 2, :, o_:o_ + n], vp, preferred_element_type=F32)
            ox = part if ox is None else ox + part
        l = ox[:, HEAD_DIM:HEAD_DIM + 1] + e_ref[h % 2]
        o = ox[:, :HEAD_DIM] * (1.0 / l)
        for g in range(Q_PER_KV):
            hq = h * Q_PER_KV + g
            o_ref[0, :, hq * HEAD_DIM:(hq + 1) * HEAD_DIM] = (
                o[g * ATTN_BLOCK:(g + 1) * ATTN_BLOCK].astype(o_ref.dtype))

    def attend_all(q_fn, k_fn, v_fn, use_bias):
        n_blocks = rows_q // ATTN_ROWS
        cols = [logits(0, q_fn(0), k_fn(0))]
        for h in range(n_kv):
            softmax_rows(h, cols[h], use_bias, range(0, n_blocks // 2))
            if h + 1 < n_kv:
                cols.append(logits(h + 1, q_fn(h + 1), k_fn(h + 1)))
            softmax_rows(h, cols[h], use_bias, range(n_blocks // 2, n_blocks))
            if h >= 1:
                weighted_values(h - 1, cols[h - 1], v_fn(h - 1))
        weighted_values(n_kv - 1, cols[n_kv - 1], v_fn(n_kv - 1))

    def q_head(hq):
        return q_ref[0, :, hq * HEAD_DIM:(hq + 1) * HEAD_DIM].astype(F32)

    def head_slice(h):
        return slice(h * HEAD_DIM, (h + 1) * HEAD_DIM)

    @pl.when(j < nb_ctx)
    def _():
        attend_all(
            lambda h: jnp.concatenate([(q_head(h * Q_PER_KV + g) * qscale).astype(BF16)
                                       for g in range(Q_PER_KV)], axis=0),
            lambda h: [kv_ref[0, :n_ctx, head_slice(h)]],
            lambda h: [vx_ref[:n_ctx, h * vxw:(h + 1) * vxw]], False)

    @pl.when(j >= nb_ctx)
    def _():
        jl = j - nb_ctx
        start = jnp.clip((jl - 1) * ATTN_BLOCK, 0, n_lat - n_win)
        start = pl.multiple_of(start, ATTN_BLOCK)
        delta = jl * ATTN_BLOCK - start
        ql = lax.broadcasted_iota(jnp.int32, (ATTN_BLOCK, n_win), 0)
        kl = lax.broadcasted_iota(jnp.int32, (ATTN_BLOCK, n_win), 1)
        bias_ref[...] = jnp.where(jnp.abs(delta + ql - kl) <= WINDOW, 0.0, NEG_INF)
        cq = cq_ref[...] * qscale
        s1q = s1q_ref[...] * qscale
        s2q = s2q_ref[...] * qscale
        attend_all(
            lambda h: jnp.concatenate([_rope(q_head(h * Q_PER_KV + g), cq, s1q, s2q).astype(BF16)
                                       for g in range(Q_PER_KV)], axis=0),
            lambda h: [kr_ref[pl.ds(start, n_win), head_slice(h)], kv_ref[0, :n_ctx, head_slice(h)]],
            lambda h: [vx_ref[pl.ds(n_ctx + start, n_win), h * vxw:(h + 1) * vxw],
                       vx_ref[:n_ctx, h * vxw:(h + 1) * vxw]], True)


def _attention(pz, sink, n_ctx, e):
    bsz, t, _ = pz.shape
    n_lat = t - n_ctx
    n_q = e // HEAD_DIM
    n_kv = n_q // Q_PER_KV
    kvw = n_kv * HEAD_DIM
    n_win = ATTN_BLOCK + 2 * WINDOW
    rows_q = Q_PER_KV * ATTN_BLOCK
    assert (2 * e) % (2 * kvw) == 0 and n_lat >= n_win
    cos, s1, s2 = _rope_tables(n_lat)
    nb_ctx = n_ctx // ATTN_BLOCK
    qtab = pl.BlockSpec((ATTN_BLOCK, HEAD_DIM), lambda b, j: (jnp.maximum(j - nb_ctx, 0), 0))
    ktab = pl.BlockSpec((n_lat, HEAD_DIM), lambda b, j: (0, 0))
    return pl.pallas_call(
        functools.partial(_attn_kernel, n_ctx=n_ctx, n_lat=n_lat, n_kv=n_kv),
        grid=(bsz, t // ATTN_BLOCK),
        in_specs=[pl.BlockSpec(memory_space=pltpu.SMEM),
                  pl.BlockSpec((1, ATTN_BLOCK, e), lambda b, j: (b, j, 1)),
                  pl.BlockSpec((1, t, 2 * kvw), lambda b, j: (b, 0, 2 * e // (2 * kvw))),
                  qtab, qtab, qtab, ktab, ktab, ktab],
        out_specs=pl.BlockSpec((1, ATTN_BLOCK, e), lambda b, j: (b, j, 0)),
        out_shape=jax.ShapeDtypeStruct((bsz, t, e), BF16),
        scratch_shapes=[pltpu.VMEM((n_lat, kvw), BF16),
                        pltpu.VMEM((t, n_kv * 2 * HEAD_DIM), BF16),
                        pltpu.VMEM((2, rows_q, n_win + n_ctx), F32),
                        pltpu.VMEM((2, rows_q, n_win + n_ctx), BF16),
                        pltpu.VMEM((2, rows_q, 1), F32),
                        pltpu.VMEM((ATTN_BLOCK, n_win), F32)],
        compiler_params=_cparams(2),
    )(sink.astype(F32), pz, pz, cos, s1, s2, cos, s1, s2)


def _cg_order(w, axis):
    axis = axis % w.ndim
    n = w.shape[axis]
    shp = w.shape[:axis] + (n // S5_GROUP, S5_GROUP) + w.shape[axis + 1:]
    return jnp.swapaxes(w.reshape(shp), axis, axis + 1).reshape(w.shape)


def _s5_operators(lam_re, lam_im, log_step, b_re, b_im, c_re, c_im, d_skip):
    n_g = lam_re.shape[1]
    tc = S5_CHUNK
    dt = jnp.exp(log_step.astype(F32))[..., None]
    lr, li = lam_re.astype(F32) * dt, lam_im.astype(F32) * dt
    k = jnp.arange(tc + 1, dtype=F32)[:, None, None, None]
    mag = jnp.exp(lr[None] * k)
    pr, pi = mag * jnp.cos(li[None] * k), mag * jnp.sin(li[None] * k)
    ar1, ai1 = pr[1] - 1.0, pi[1]
    den = lam_re.astype(F32) ** 2 + lam_im.astype(F32) ** 2
    cr = (ar1 * lam_re + ai1 * lam_im) / den
    ci = (ai1 * lam_re - ar1 * lam_im) / den
    br, bi = b_re.astype(F32), b_im.astype(F32)
    bbr = cr[..., None] * br - ci[..., None] * bi
    bbi = cr[..., None] * bi + ci[..., None] * br
    ccr, cci = c_re.astype(F32), c_im.astype(F32)

    def kern(d):
        xr = pr[:tc, d][..., None] * bbr[d][None] - pi[:tc, d][..., None] * bbi[d][None]
        xi = pr[:tc, d][..., None] * bbi[d][None] + pi[:tc, d][..., None] * bbr[d][None]
        return (jnp.einsum('gnp,tgpm->tgnm', ccr[d], xr) - jnp.einsum('gnp,tgpm->tgnm', cci[d], xi))

    kf, kb = kern(0), kern(1)
    s_idx = jnp.arange(tc)[:, None]
    t_idx = jnp.arange(tc)[None, :]
    lag = t_idx - s_idx
    mf = jnp.where((lag >= 0)[:, :, None, None, None], kf[jnp.clip(lag, 0, tc - 1)], 0.0)
    mb = jnp.where((lag <= 0)[:, :, None, None, None], kb[jnp.clip(-lag, 0, tc - 1)], 0.0)
    m = (mf + mb).transpose(2, 4, 0, 3, 1)
    eye = (jnp.eye(S5_GROUP, dtype=F32)[:, None, :, None] * jnp.eye(tc, dtype=F32)[None, :, None, :])
    m = m + eye[None] * d_skip.astype(F32).reshape(n_g, S5_GROUP, 1, 1, 1)
    m = m.reshape(n_g, tc * S5_GROUP, tc * S5_GROUP)

    def qpart(d, pw_r, pw_i):
        qr = pw_r[..., None] * bbr[d][None] - pw_i[..., None] * bbi[d][None]
        qi = pw_r[..., None] * bbi[d][None] + pw_i[..., None] * bbr[d][None]
        return qr.transpose(1, 3, 0, 2), qi.transpose(1, 3, 0, 2)

    qfr, qfi = qpart(0, pr[tc - 1 - jnp.arange(tc), 0], pi[tc - 1 - jnp.arange(tc), 0])
    qbr, qbi = qpart(1, pr[jnp.arange(tc), 1], pi[jnp.arange(tc), 1])
    q = jnp.concatenate([qfr, qbr, qfi, qbi], axis=-1).reshape(n_g, tc * S5_GROUP, 4 * S5_STATE)

    def ppart(d, pw_r, pw_i):
        xr = ccr[d][None] * pw_r[:, :, None, :] - cci[d][None] * pw_i[:, :, None, :]
        xi = ccr[d][None] * pw_i[:, :, None, :] + cci[d][None] * pw_r[:, :, None, :]
        return xr.transpose(1, 3, 2, 0), (-xi).transpose(1, 3, 2, 0)

    pfr, pfi = ppart(0, pr[1 + jnp.arange(tc), 0], pi[1 + jnp.arange(tc), 0])
    pbr, pbi = ppart(1, pr[tc - jnp.arange(tc), 1], pi[tc - jnp.arange(tc), 1])
    p = jnp.concatenate([pfr, pbr, pfi, pbi], axis=1).reshape(n_g, 4 * S5_STATE, tc * S5_GROUP)
    a = jnp.stack([jnp.concatenate([pr[tc, 0], pr[tc, 1]], axis=-1),
                   jnp.concatenate([pi[tc, 0], pi[tc, 1]], axis=-1)], axis=1)
    a = jnp.pad(a, ((0, 0), (0, 6), (0, 0)))
    return m.astype(BF16), q.astype(BF16), p.astype(BF16), a


def _s5_pack_kernel(x_ref, o_ref, f_ref, *, nb, n_g):
    half = S5_GROUP // 2
    for b in range(nb):
        xb = x_ref[b].astype(F32)
        for k in range(2):
            a = jnp.concatenate([xb[:, cc * n_g:(cc + 1) * n_g]
                                 for cc in range(k * half, (k + 1) * half)], axis=0)
            f_ref[k, pl.ds(b, n_g, stride=nb), :] = a.T
    full = jnp.concatenate([f_ref[0], f_ref[1]], axis=1)
    o_ref[...] = full.reshape(n_g, nb, S5_CHUNK * S5_GROUP).astype(o_ref.dtype)


def _s5_pack(pz, e):
    bsz, t, _ = pz.shape
    n_g = e // S5_GROUP
    n_chunks = t // S5_CHUNK
    w = S5_CHUNK * S5_GROUP
    return pl.pallas_call(
        functools.partial(_s5_pack_kernel, nb=bsz, n_g=n_g),
        grid=(n_chunks,),
        in_specs=[pl.BlockSpec((bsz, S5_CHUNK, e), lambda i: (0, i, 1))],
        out_specs=pl.BlockSpec((n_g, bsz, w), lambda i: (0, i, 0)),
        out_shape=jax.ShapeDtypeStruct((n_g, n_chunks * bsz, w), BF16),
        scratch_shapes=[pltpu.VMEM((2, n_g * bsz, w // 2), F32)],
        compiler_params=_cparams(1),
    )(pz)


def _s5_kernel(u_ref, m_ref, q_ref, p_ref, a_ref, y_ref, s_ref, h_ref, *, nb, n_chunks, nc_ctx):
    ns = S5_STATE
    u = u_ref[0]
    s_ref[...] = jnp.dot(u, q_ref[0], preferred_element_type=F32)
    ar = a_ref[0, 0:1, :]
    ai = a_ref[0, 1:2, :]
    is_fwd = lax.broadcasted_iota(jnp.int32, (nb, 2 * ns), 1) < ns

    def step(i, carry):
        hr, hi = carry
        cb = jnp.where(i < nc_ctx, nc_ctx - 1 - i, n_chunks - 1 - (i - nc_ctx))
        rf = pl.multiple_of(i * nb, nb)
        rb = pl.multiple_of(cb * nb, nb)
        h_ref[pl.ds(rf, nb), 0:ns] = hr[:, 0:ns]
        h_ref[pl.ds(rb, nb), ns:2 * ns] = hr[:, ns:2 * ns]
        h_ref[pl.ds(rf, nb), 2 * ns:3 * ns] = hi[:, 0:ns]
        h_ref[pl.ds(rb, nb), 3 * ns:4 * ns] = hi[:, ns:2 * ns]
        sf = s_ref[pl.ds(rf, nb), :]
        sb = s_ref[pl.ds(rb, nb), :]
        sr = jnp.where(is_fwd, sf[:, :2 * ns], sb[:, :2 * ns])
        si = jnp.where(is_fwd, sf[:, 2 * ns:], sb[:, 2 * ns:])
        return ar * hr - ai * hi + sr, ar * hi + ai * hr + si

    zero = jnp.zeros((nb, 2 * ns), F32)
    lax.fori_loop(0, n_chunks, step, (zero, zero))
    y = jnp.dot(u, m_ref[0], preferred_element_type=F32)
    y = y + jnp.dot(h_ref[...].astype(BF16), p_ref[0], preferred_element_type=F32)
    y_ref[0] = y.astype(y_ref.dtype)


def _s5_scan(uf, m, q, p, a, nb, nc_ctx):
    n_g, rows, w = uf.shape
    n_chunks = rows // nb
    wspec = pl.BlockSpec((1, w, w), lambda g: (g, 0, 0))
    return pl.pallas_call(
        functools.partial(_s5_kernel, nb=nb, n_chunks=n_chunks, nc_ctx=nc_ctx),
        grid=(n_g,),
        in_specs=[pl.BlockSpec((1, rows, w), lambda g: (g, 0, 0)), wspec, wspec, wspec,
                  pl.BlockSpec((1, 8, 2 * S5_STATE), lambda g: (g, 0, 0))],
        out_specs=pl.BlockSpec((1, rows, w), lambda g: (g, 0, 0)),
        out_shape=jax.ShapeDtypeStruct((n_g, rows, w), BF16),
        scratch_shapes=[pltpu.VMEM((rows, w), F32), pltpu.VMEM((rows, w), F32)],
        compiler_params=_cparams(1),
    )(uf, m, q, p, a)


def _s5_glu_kernel(y0_ref, yn_ref, w_ref, b_ref, o_ref, f_ref, g_ref, *, nb, n_g):
    i = pl.program_id(0)
    w = S5_CHUNK * S5_GROUP
    half = S5_GROUP // 2
    e = n_g * S5_GROUP

    def widen(y_ref):
        yv = y_ref[...].astype(F32).reshape(n_g * nb, w)
        for k in range(2):
            f_ref[k] = yv[:, k * (w // 2):(k + 1) * (w // 2)]

    def build(slot, batches):
        for b in batches:
            for k in range(2):
                t = f_ref[k, pl.ds(b, n_g, stride=nb), :].T
                g = 0.5 * t * (1.0 + jnp.tanh(math.sqrt(2.0 / math.pi) * (t + 0.044715 * t * t * t)))
                g = g.astype(BF16)
                for cc in range(half):
                    ch = k * half + cc
                    g_ref[slot, b * S5_CHUNK:(b + 1) * S5_CHUNK, ch * n_g:(ch + 1) * n_g] = (
                        g[cc * S5_CHUNK:(cc + 1) * S5_CHUNK, :])

    @pl.when(i == 0)
    def _():
        widen(y0_ref)
        build(0, range(nb))

    cur = i % 2
    nxt = 1 - cur
    n_pieces = GLU_PIECES
    tn = e // n_pieces
    per = nb // n_pieces
    widen(yn_ref)
    for p in range(n_pieces):
        cols = slice(p * tn, (p + 1) * tn)
        g = g_ref[cur]
        lin = jnp.dot(g, w_ref[:, cols], preferred_element_type=F32) + b_ref[:, cols]
        out = (g[:, cols].astype(F32) * _sigmoid(lin)).astype(o_ref.dtype)
        o_ref[:, :, cols] = out.reshape(nb, S5_CHUNK, tn)
        build(nxt, range(p * per, (p + 1) * per))


def _s5_glu(yf, w_bf, bias, bsz):
    n_g, rows, w = yf.shape
    e = n_g * S5_GROUP
    n_chunks = rows // bsz
    assert bsz % GLU_PIECES == 0 and e % GLU_PIECES == 0
    return pl.pallas_call(
        functools.partial(_s5_glu_kernel, nb=bsz, n_g=n_g),
        grid=(n_chunks,),
        in_specs=[pl.BlockSpec((n_g, bsz, w), lambda i: (0, 0, 0)),
                  pl.BlockSpec((n_g, bsz, w), lambda i: (0, jnp.minimum(i + 1, n_chunks - 1), 0)),
                  pl.BlockSpec((e, e), lambda i: (0, 0)),
                  pl.BlockSpec((1, e), lambda i: (0, 0))],
        out_specs=pl.BlockSpec((bsz, S5_CHUNK, e), lambda i: (0, i, 0)),
        out_shape=jax.ShapeDtypeStruct((bsz, n_chunks * S5_CHUNK, e), BF16),
        scratch_shapes=[pltpu.VMEM((2, n_g * bsz, w // 2), F32),
                        pltpu.VMEM((2, bsz * S5_CHUNK, e), BF16)],
        compiler_params=_cparams(1),
    )(yf, yf, w_bf, bias.astype(F32).reshape(1, e))


def _s5_mixer(pz, n_ctx, e, ops, glu_w_bf, glu_b):
    bsz = pz.shape[0]
    m, q, p, a = ops
    uf = _s5_pack(pz, e)
    yf = _s5_scan(uf, m, q, p, a, bsz, n_ctx // S5_CHUNK)
    return _s5_glu(yf, glu_w_bf, glu_b, bsz)


def _cumsum_rows(x, reverse):
    n = x.shape[0]
    row = lax.broadcasted_iota(jnp.int32, x.shape, 0)
    s = 1
    while s < n:
        if reverse:
            x = x + jnp.where(row < n - s, pltpu.roll(x, n - s, 0), 0.0)
        else:
            x = x + jnp.where(row >= s, pltpu.roll(x, s, 0), 0.0)
        s *= 2
    return x


def _hgrn_kernel(q_ref, ff_ref, fb_ref, v_ref, lb_ref, ng_ref, o_ref,
                 ds_ref, dec_ref, st_ref, qd_ref, oi_ref, *, layer, n_chunks, nc_ctx, n_tok):
    c = HG_CHUNK
    hd = HG_HEAD
    nt = (((1,), (1,)), ((), ()))
    tn = (((0,), (0,)), ((), ()))

    lbw = lb_ref[...].astype(F32)
    ew = jnp.exp(lbw - jnp.max(lbw, axis=0, keepdims=True))
    lb = jnp.sum(ew[1:layer + 1], axis=0, keepdims=True) / jnp.sum(ew, axis=0, keepdims=True)

    ti = lax.broadcasted_iota(jnp.int32, (c, 2 * c), 0)
    si = lax.broadcasted_iota(jnp.int32, (c, 2 * c), 1)
    m_fwd = si <= ti
    m_bwd = si - c >= ti

    def chunk_a(ci):
        r0 = pl.multiple_of(ci * c, c)
        q = q_ref[0, pl.ds(r0, c), :].astype(F32)
        v = v_ref[0, pl.ds(r0, c), :]
        qes, kes, kds = [], [], []
        for d, f_ref in enumerate((ff_ref, fb_ref)):
            fl = f_ref[0, pl.ds(r0, c), :].astype(F32)
            f = lb + (1.0 - lb) * _sigmoid(fl)
            b = _cumsum_rows(jnp.log(f), reverse=bool(d))
            mid = c // 2 if d == 0 else c // 2 - 1
            last = c - 1 if d == 0 else 0
            ref = b[mid:mid + 1]
            b_last = b[last:last + 1]
            qe = q * jnp.exp(b - ref)
            ke = (1.0 - f) * jnp.exp(ref - b)
            qes.append(qe.astype(BF16))
            kes.append(ke.astype(BF16))
            qd_ref[pl.ds(r0, c), d * hd:(d + 1) * hd] = (qe * jnp.exp(ref)).astype(BF16)
            kds.append((ke * jnp.exp(b_last - ref)).astype(BF16))
            dec_ref[ci, :, d * hd:(d + 1) * hd] = jnp.exp(b_last)
        a = lax.dot_general(jnp.concatenate(qes, axis=0), jnp.concatenate(kes, axis=0), nt,
                            preferred_element_type=F32)
        att = jnp.where(m_fwd, a[:c], jnp.where(m_bwd, a[c:], 0.0))
        oi_ref[pl.ds(r0, c), :] = jnp.dot(att.astype(BF16), jnp.concatenate([v, v], axis=0),
                                          preferred_element_type=F32)
        ds_ref[ci] = lax.dot_general(v, jnp.concatenate(kds, axis=1), tn,
                                     preferred_element_type=F32)

    def loop_a(i, carry):
        for u in range(HG_UNROLL):
            chunk_a(i * HG_UNROLL + u)
        return carry

    lax.fori_loop(0, n_chunks // HG_UNROLL, loop_a, 0)

    def scan_b(i, carry):
        sf, sb = carry
        cb = jnp.where(i < nc_ctx, nc_ctx - 1 - i, n_chunks - 1 - (i - nc_ctx))
        st_ref[i, :, 0:hd] = sf.astype(BF16)
        st_ref[cb, :, hd:2 * hd] = sb.astype(BF16)
        sf = sf * dec_ref[i, :, 0:hd] + ds_ref[i, :, 0:hd]
        sb = sb * dec_ref[cb, :, hd:2 * hd] + ds_ref[cb, :, hd:2 * hd]
        return sf, sb

    zero = jnp.zeros((hd, hd), F32)
    lax.fori_loop(0, n_chunks, scan_b, (zero, zero))

    def chunk_c(ci):
        r0 = pl.multiple_of(ci * c, c)
        oi_ref[pl.ds(r0, c), :] += lax.dot_general(
            qd_ref[pl.ds(r0, c), :], st_ref[ci], nt, preferred_element_type=F32)

    def loop_c(i, carry):
        for u in range(HG_UNROLL):
            chunk_c(i * HG_UNROLL + u)
        return carry

    lax.fori_loop(0, n_chunks // HG_UNROLL, loop_c, 0)

    gn = ng_ref[...].astype(F32)
    for r0 in range(0, n_tok, HG_NORM_ROWS):
        o = oi_ref[r0:r0 + HG_NORM_ROWS, :]
        o = o * lax.rsqrt(jnp.mean(o * o, axis=-1, keepdims=True) + NORM_EPS) * gn
        o_ref[0, r0:r0 + HG_NORM_ROWS, :] = o.astype(o_ref.dtype)


def _hgrn(pz, hg_lb, norm_g, layer, n_ctx, e):
    bsz, t, _ = pz.shape
    n_h = e // HG_HEAD
    n_chunks = t // HG_CHUNK
    depth = hg_lb.shape[0]
    hd = HG_HEAD
    assert n_chunks % HG_UNROLL == 0 and t % HG_NORM_ROWS == 0

    def col(k):
        return pl.BlockSpec((1, t, hd), lambda b, h: (b, 0, (k + 1) * n_h + h))

    return pl.pallas_call(
        functools.partial(_hgrn_kernel, layer=layer, n_chunks=n_chunks, nc_ctx=n_ctx // HG_CHUNK, n_tok=t),
        grid=(bsz, n_h),
        in_specs=[col(0), col(1), col(2), col(3),
                  pl.BlockSpec((depth, hd), lambda b, h: (0, h)),
                  pl.BlockSpec((1, hd), lambda b, h: (0, h))],
        out_specs=pl.BlockSpec((1, t, hd), lambda b, h: (b, 0, h)),
        out_shape=jax.ShapeDtypeStruct((bsz, t, e), BF16),
        scratch_shapes=[pltpu.VMEM((n_chunks, hd, 2 * hd), F32),
                        pltpu.VMEM((n_chunks, 1, 2 * hd), F32),
                        pltpu.VMEM((n_chunks, hd, 2 * hd), BF16),
                        pltpu.VMEM((t, 2 * hd), BF16),
                        pltpu.VMEM((t, hd), F32)],
        compiler_params=_cparams(2),
    )(pz, pz, pz, pz, hg_lb.astype(F32), norm_g.astype(F32).reshape(1, e))


def _cumsum_rows_multi(xs, reverse):
    n = xs[0].shape[0]
    row = lax.broadcasted_iota(jnp.int32, xs[0].shape, 0)
    s = 1
    while s < n:
        nxt = []
        for x, rev in zip(xs, reverse):
            if rev:
                nxt.append(x + jnp.where(row < n - s, pltpu.roll(x, n - s, 0), 0.0))
            else:
                nxt.append(x + jnp.where(row >= s, pltpu.roll(x, s, 0), 0.0))
        xs = nxt
        s *= 2
    return xs


def _hgrn2_kernel(x_ref, lb_ref, ng_ref, o_ref,
                  g_ref, k_ref, qe_ref, ke_ref, qd_ref, kd_ref, ds_ref, dec_ref, st_ref, oi_ref,
                  *, layer, n_chunks, nc_ctx, n_tok):
    c = HG_CHUNK
    hd = HG_HEAD
    nt = (((1,), (1,)), ((), ()))
    tn = (((0,), (0,)), ((), ()))

    lbw = lb_ref[...].astype(F32)
    ew = jnp.exp(lbw - jnp.max(lbw, axis=0, keepdims=True))
    lb = jnp.sum(ew[1:layer + 1], axis=0, keepdims=True) / jnp.sum(ew, axis=0, keepdims=True)

    def gates(ci):
        r0 = pl.multiple_of(ci * c, c)
        for d in range(2):
            fl = x_ref[0, pl.ds(r0, c), (1 + d) * hd:(2 + d) * hd].astype(F32)
            f = lb + (1.0 - lb) * _sigmoid(fl)
            g_ref[d, pl.ds(r0, c), :] = jnp.log2(f)
            k_ref[d, pl.ds(r0, c), :] = 1.0 - f

    def cumdecay(cis):
        r0s = [pl.multiple_of(ci * c, c) for ci in cis]
        xs = [g_ref[d, pl.ds(r0, c), :] for r0 in r0s for d in range(2)]
        xs = _cumsum_rows_multi(xs, [False, True] * len(cis))
        k = 0
        for r0 in r0s:
            for d in range(2):
                g_ref[d, pl.ds(r0, c), :] = xs[k]
                k += 1

    def decayed(ci):
        r0 = pl.multiple_of(ci * c, c)
        q = x_ref[0, pl.ds(r0, c), 0:hd].astype(F32)
        for d in range(2):
            mid = c // 2 if d == 0 else c // 2 - 1
            last = c - 1 if d == 0 else 0
            b = g_ref[d, pl.ds(r0, c), :]
            ref = g_ref[d, pl.ds(r0 + mid, 1), :]
            b_last = g_ref[d, pl.ds(r0 + last, 1), :]
            qe = q * jnp.exp2(b - ref)
            ke = k_ref[d, pl.ds(r0, c), :] * jnp.exp2(ref - b)
            qe_ref[d, pl.ds(r0, c), :] = qe.astype(BF16)
            ke_ref[d, pl.ds(r0, c), :] = ke.astype(BF16)
            qd_ref[pl.ds(r0, c), d * hd:(d + 1) * hd] = (qe * jnp.exp2(ref)).astype(BF16)
            kd_ref[pl.ds(r0, c), d * hd:(d + 1) * hd] = (ke * jnp.exp2(b_last - ref)).astype(BF16)
            dec_ref[ci, :, d * hd:(d + 1) * hd] = jnp.exp2(b_last)

    def prepare(cis):
        for ci in cis:
            gates(ci)
        cumdecay(cis)
        for ci in cis:
            decayed(ci)

    ti = lax.broadcasted_iota(jnp.int32, (c, 2 * c), 0)
    si = lax.broadcasted_iota(jnp.int32, (c, 2 * c), 1)
    m_fwd = si <= ti
    m_bwd = si - c >= ti

    def logits(cis):
        r0s = [pl.multiple_of(ci * c, c) for ci in cis]
        vs = [x_ref[0, pl.ds(r0, c), 3 * hd:4 * hd] for r0 in r0s]
        aa = [lax.dot_general(
            jnp.concatenate([qe_ref[0, pl.ds(r0, c), :], qe_ref[1, pl.ds(r0, c), :]], axis=0),
            jnp.concatenate([ke_ref[0, pl.ds(r0, c), :], ke_ref[1, pl.ds(r0, c), :]], axis=0),
            nt, preferred_element_type=F32) for r0 in r0s]
        dss = [lax.dot_general(v, kd_ref[pl.ds(r0, c), :], tn, preferred_element_type=F32)
               for v, r0 in zip(vs, r0s)]
        return r0s, vs, aa, dss

    def values(cis, vs, aa, dss):
        atts = [jnp.where(m_fwd, a[:c], jnp.where(m_bwd, a[c:], 0.0)).astype(BF16) for a in aa]
        ois = [jnp.dot(att, jnp.concatenate([v, v], axis=0), preferred_element_type=F32)
               for att, v in zip(atts, vs)]
        for ci, ds in zip(cis, dss):
            ds_ref[ci] = ds
        return tuple(ois)

    def store_intra(cis, ois):
        for ci, oi in zip(cis, ois):
            oi_ref[pl.ds(pl.multiple_of(ci * c, c), c), :] = oi

    n_pairs = n_chunks // HG_PAIR

    def pair(t):
        return [t * HG_PAIR + u for u in range(HG_PAIR)]

    def stage(t, pending, with_next):
        cur = pair(t)
        _, vs, aa, dss = logits(cur)
        if pending is not None:
            store_intra(pair(t - 1), pending)
        if with_next:
            nxt = pair(t + 1)
            for ci in nxt:
                gates(ci)
            cumdecay(nxt)
        ois = values(cur, vs, aa, dss)
        if with_next:
            for ci in nxt:
                decayed(ci)
        return ois

    prepare(pair(0))
    pending = stage(0, None, True)
    for t in range(1, n_pairs - 1):
        pending = stage(t, pending, True)
    pending = stage(n_pairs - 1, pending, False)
    store_intra(pair(n_pairs - 1), pending)

    def scan_b(i, carry):
        sf, sb = carry
        cb = jnp.where(i < nc_ctx, nc_ctx - 1 - i, n_chunks - 1 - (i - nc_ctx))
        st_ref[i, :, 0:hd] = sf.astype(BF16)
        st_ref[cb, :, hd:2 * hd] = sb.astype(BF16)
        sf = sf * dec_ref[i, :, 0:hd] + ds_ref[i, :, 0:hd]
        sb = sb * dec_ref[cb, :, hd:2 * hd] + ds_ref[cb, :, hd:2 * hd]
        return sf, sb

    zero = jnp.zeros((hd, hd), F32)
    lax.fori_loop(0, n_chunks, scan_b, (zero, zero))

    def inter(cis):
        os_ = [lax.dot_general(qd_ref[ci * c:(ci + 1) * c, :], st_ref[ci], nt, preferred_element_type=F32)
               for ci in cis]
        for ci, o in zip(cis, os_):
            oi_ref[ci * c:(ci + 1) * c, :] += o

    gn = ng_ref[...].astype(F32)

    def head_norm(r0):
        o = oi_ref[r0:r0 + HG_NORM_ROWS, :]
        o = o * lax.rsqrt(jnp.mean(o * o, axis=-1, keepdims=True) + NORM_EPS) * gn
        o_ref[0, r0:r0 + HG_NORM_ROWS, :] = o.astype(o_ref.dtype)

    group_rows = HG_UNROLL_INTER * c
    n_groups = n_chunks // HG_UNROLL_INTER
    for gi in range(n_groups + 1):
        if gi < n_groups:
            inter(range(gi * HG_UNROLL_INTER, (gi + 1) * HG_UNROLL_INTER))
        if gi >= 1:
            for r0 in range((gi - 1) * group_rows, gi * group_rows, HG_NORM_ROWS):
                head_norm(r0)


def _hgrn2(pz, hg_lb, norm_g, layer, n_ctx, e):
    bsz, t, _ = pz.shape
    n_h = e // HG_HEAD
    n_chunks = t // HG_CHUNK
    depth = hg_lb.shape[0]
    hd = HG_HEAD
    assert n_chunks % HG_PAIR == 0 and n_chunks // HG_PAIR >= 3 and n_chunks % HG_UNROLL_INTER == 0
    assert t % HG_NORM_ROWS == 0

    return pl.pallas_call(
        functools.partial(_hgrn2_kernel, layer=layer, n_chunks=n_chunks, nc_ctx=n_ctx // HG_CHUNK, n_tok=t),
        grid=(bsz, n_h),
        in_specs=[pl.BlockSpec((1, t, 4 * hd), lambda b, h: (b, 0, e // (4 * hd) + h)),
                  pl.BlockSpec((depth, hd), lambda b, h: (0, h)),
                  pl.BlockSpec((1, hd), lambda b, h: (0, h))],
        out_specs=pl.BlockSpec((1, t, hd), lambda b, h: (b, 0, h)),
        out_shape=jax.ShapeDtypeStruct((bsz, t, e), BF16),
        scratch_shapes=[pltpu.VMEM((2, t, hd), F32),
                        pltpu.VMEM((2, t, hd), F32),
                        pltpu.VMEM((2, t, hd), BF16),
                        pltpu.VMEM((2, t, hd), BF16),
                        pltpu.VMEM((t, 2 * hd), BF16),
                        pltpu.VMEM((t, 2 * hd), BF16),
                        pltpu.VMEM((n_chunks, hd, 2 * hd), F32),
                        pltpu.VMEM((n_chunks, 1, 2 * hd), F32),
                        pltpu.VMEM((n_chunks, hd, 2 * hd), BF16),
                        pltpu.VMEM((t, hd), F32)],
        compiler_params=_cparams(2),
    )(pz, hg_lb.astype(F32), norm_g.astype(F32).reshape(1, e))


def _head_major(w, e):
    d = w.shape[0]
    n_h = e // HG_HEAD
    rest = w[:, e:].reshape(d, 4, n_h, HG_HEAD).transpose(0, 2, 1, 3).reshape(d, 4 * e)
    return jnp.concatenate([w[:, :e], rest], axis=1)


def _z_first(w, e):
    return jnp.concatenate([w[..., -e:], w[..., :-e]], axis=-1)


def kernel(x, c, ctx, c_ctx, ada_w, ada_b, ln_g, ln_b, w_out, attn_w_in, attn_sink, s5_w_in, s5_lam_re, s5_lam_im, s5_log_step, s5_b_re, s5_b_im, s5_c_re, s5_c_im, s5_d, s5_glu_w, s5_glu_b, hg_w_in, hg_lb, hg_norm_g):
    bsz, n_lat, d = x.shape
    n_ctx = ctx.shape[1]
    depth = ada_w.shape[0]
    e = w_out.shape[1]
    alpha = (2.0 * depth) ** 0.25
    t = n_ctx + n_lat
    tm = 768 if t % 768 == 0 else 256

    mod_rows = -(-(bsz + 1) // MOD_ROWS_PAD) * MOD_ROWS_PAD
    c_all = jnp.concatenate([c.astype(F32), c_ctx.astype(F32)[None],
                             jnp.zeros((mod_rows - bsz - 1, d), F32)], axis=0)
    mod = _ada(c_all, ada_w.astype(F32), ada_b.astype(F32))
    mod3 = mod.reshape(depth * mod_rows, 1, 3 * d)

    xa = jnp.concatenate([ctx.astype(F32), x.astype(F32)], axis=1)
    for i in range(depth):
        kind, j = i % N_MIXERS, i // N_MIXERS
        w_in = _z_first((attn_w_in, s5_w_in, hg_w_in)[kind][j], e)
        w_o = w_out[i]
        if kind == 1:
            w_in = jnp.concatenate([_cg_order(w_in[:, :e], 1), _cg_order(w_in[:, e:], 1)], axis=1)
            w_o = _cg_order(w_o, 0)
        elif kind == 2:
            w_in = _head_major(w_in, e)
        pz = _inproj(xa, mod3, i, mod_rows, w_in.astype(BF16), n_ctx, tn=INPROJ_TN)
        if kind == 0:
            y = _attention(pz, attn_sink[j], n_ctx, e)
        elif kind == 1:
            ops = _s5_operators(s5_lam_re[j], s5_lam_im[j], s5_log_step[j], s5_b_re[j], s5_b_im[j],
                                s5_c_re[j], s5_c_im[j], s5_d[j])
            glu_w = _cg_order(_cg_order(s5_glu_w[j], 0), 1).astype(BF16)
            y = _s5_mixer(pz, n_ctx, e, ops, glu_w, _cg_order(s5_glu_b[j], 0))
        else:
            y = _hgrn2(pz, hg_lb, hg_norm_g[j], i, n_ctx, e)
        xa = _outproj(y, pz, xa, mod3, i, mod_rows, w_o.astype(BF16), ln_g[i].astype(F32),
                      ln_b[i].astype(F32), n_ctx, alpha, tm)
    return xa[:, n_ctx:].astype(x.dtype)
```

```python
import functools
import math

import jax
import jax.numpy as jnp
from jax import lax
from jax.experimental import pallas as pl
from jax.experimental.pallas import tpu as pltpu

F32 = jnp.float32
BF16 = jnp.bfloat16

N_MIXERS = 3
HEAD_DIM = 128
Q_PER_KV = 4
WINDOW = 128
ATTN_BLOCK = 128
GRID_W = 64
ROPE_THETA = 10000.0
S5_GROUP = 16
S5_STATE = 64
S5_CHUNK = 16
HG_HEAD = 128
HG_CHUNK = 64
HG_UNROLL = 6
HG_UNROLL_INTER = 12
HG_PAIR = 2
HG_NORM_ROWS = 256
NORM_EPS = 1e-5
NEG_INF = -1e30
LOG2E = math.log2(math.e)
ATTN_ROWS = 32
MOD_ROWS_PAD = 8
INPROJ_TN = 1024
GLU_PIECES = 8
HG_PROJ_PIECES = 3
VMEM_LIMIT = 56 * 1024 * 1024


def _cparams(n_axes):
    return pltpu.CompilerParams(dimension_semantics=("arbitrary",) * n_axes,
                                vmem_limit_bytes=VMEM_LIMIT)


def _sigmoid(x):
    return 1.0 / (1.0 + jnp.exp(-x))


def _ada_kernel(c_ref, w_ref, b_ref, o_ref):
    cv = c_ref[...]
    s = cv * _sigmoid(cv)
    o_ref[0] = jnp.dot(s, w_ref[0], preferred_element_type=F32,
                       precision=lax.Precision.HIGHEST) + b_ref[0]


def _ada(c_all, ada_w, ada_b):
    depth, d, d3 = ada_w.shape
    rows = c_all.shape[0]
    return pl.pallas_call(
        _ada_kernel,
        grid=(depth, d3 // d),
        in_specs=[pl.BlockSpec((rows, d), lambda i, j: (0, 0)),
                  pl.BlockSpec((1, d, d), lambda i, j: (i, 0, j)),
                  pl.BlockSpec((1, 1, d), lambda i, j: (i, 0, j))],
        out_specs=pl.BlockSpec((1, rows, d), lambda i, j: (i, 0, j)),
        out_shape=jax.ShapeDtypeStruct((depth, rows, d3), F32),
        compiler_params=_cparams(2),
    )(c_all, ada_w, ada_b.reshape(depth, 1, d3))


def _inproj_kernel(x_ref, mb_ref, mc_ref, w_ref, o_ref, h_ref, *, n_ctx, d):
    @pl.when(pl.program_id(1) == 0)
    def _():
        mb = mb_ref[0]
        mc = mc_ref[0]
        h_ref[:n_ctx] = (x_ref[0, :n_ctx] * (1.0 + mc[:, d:2 * d]) + mc[:, :d]).astype(BF16)
        h_ref[n_ctx:] = (x_ref[0, n_ctx:] * (1.0 + mb[:, d:2 * d]) + mb[:, :d]).astype(BF16)

    o_ref[0] = jnp.dot(h_ref[...], w_ref[...], preferred_element_type=F32).astype(o_ref.dtype)


def _inproj(xa, mod3, layer, mod_rows, w_bf, n_ctx, tn):
    bsz, t, d = xa.shape
    n = w_bf.shape[1]
    base = layer * mod_rows
    return pl.pallas_call(
        functools.partial(_inproj_kernel, n_ctx=n_ctx, d=d),
        grid=(bsz, n // tn),
        in_specs=[pl.BlockSpec((1, t, d), lambda b, j: (b, 0, 0)),
                  pl.BlockSpec((1, 1, 3 * d), lambda b, j: (base + b, 0, 0)),
                  pl.BlockSpec((1, 1, 3 * d), lambda b, j: (base + bsz, 0, 0)),
                  pl.BlockSpec((d, tn), lambda b, j: (0, j))],
        out_specs=pl.BlockSpec((1, t, tn), lambda b, j: (b, 0, j)),
        out_shape=jax.ShapeDtypeStruct((bsz, t, n), BF16),
        scratch_shapes=[pltpu.VMEM((t, d), BF16)],
        compiler_params=_cparams(2),
    )(xa, mod3, mod3, w_bf)


def _outproj_kernel(y_ref, z_ref, x_ref, mb_ref, mc_ref, w_ref, g_ref, b_ref, o_ref,
                    *, n_ctx, d, alpha, tm):
    z = z_ref[0]
    a = y_ref[0] * (z * _sigmoid(z))
    br = jnp.dot(a, w_ref[...], preferred_element_type=F32)
    rows = pl.program_id(1) * tm + lax.broadcasted_iota(jnp.int32, (tm, 1), 0)
    gate = jnp.where(rows < n_ctx, mc_ref[0][:, 2 * d:], mb_ref[0][:, 2 * d:])
    v = alpha * x_ref[0] + gate * br
    mu = jnp.mean(v, axis=-1, keepdims=True)
    vc = v - mu
    var = jnp.mean(vc * vc, axis=-1, keepdims=True)
    o_ref[0] = vc * lax.rsqrt(var + NORM_EPS) * g_ref[...] + b_ref[...]


def _outproj(y, pz, xa, mod3, layer, mod_rows, w_bf, ln_g, ln_b, n_ctx, alpha, tm):
    bsz, t, d = xa.shape
    e = w_bf.shape[0]
    base = layer * mod_rows
    return pl.pallas_call(
        functools.partial(_outproj_kernel, n_ctx=n_ctx, d=d, alpha=alpha, tm=tm),
        grid=(bsz, t // tm),
        in_specs=[pl.BlockSpec((1, tm, e), lambda b, i: (b, i, 0)),
                  pl.BlockSpec((1, tm, e), lambda b, i: (b, i, 0)),
                  pl.BlockSpec((1, tm, d), lambda b, i: (b, i, 0)),
                  pl.BlockSpec((1, 1, 3 * d), lambda b, i: (base + b, 0, 0)),
                  pl.BlockSpec((1, 1, 3 * d), lambda b, i: (base + bsz, 0, 0)),
                  pl.BlockSpec((e, d), lambda b, i: (0, 0)),
                  pl.BlockSpec((1, d), lambda b, i: (0, 0)),
                  pl.BlockSpec((1, d), lambda b, i: (0, 0))],
        out_specs=pl.BlockSpec((1, tm, d), lambda b, i: (b, i, 0)),
        out_shape=jax.ShapeDtypeStruct((bsz, t, d), F32),
        compiler_params=_cparams(2),
    )(y, pz, xa, mod3, mod3, w_bf, ln_g.reshape(1, d), ln_b.reshape(1, d))


def _rope_tables(n_lat):
    pos = jnp.arange(n_lat, dtype=jnp.int32)
    row = (pos // GRID_W).astype(F32)
    col = (pos % GRID_W).astype(F32)
    nf = HEAD_DIM // 4
    inv_freq = jnp.power(ROPE_THETA, -jnp.arange(nf, dtype=F32) / nf)
    ang_r = row[:, None] * inv_freq[None, :]
    ang_c = col[:, None] * inv_freq[None, :]
    zeros = jnp.zeros_like(ang_r)
    cos = jnp.concatenate([jnp.cos(ang_r)] * 2 + [jnp.cos(ang_c)] * 2, axis=-1)
    s1 = jnp.concatenate([-jnp.sin(ang_r), zeros, -jnp.sin(ang_c), zeros], axis=-1)
    s2 = jnp.concatenate([zeros, jnp.sin(ang_r), zeros, jnp.sin(ang_c)], axis=-1)
    return cos, s1, s2


def _rope(x, cos, s1, s2):
    quarter = HEAD_DIM // 4
    return (x * cos + pltpu.roll(x, HEAD_DIM - quarter, 1) * s1
            + pltpu.roll(x, quarter, 1) * s2)


def _attn_kernel(sink_ref, q_ref, kv_ref, cq_ref, s1q_ref, s2q_ref, ck_ref, s1k_ref, s2k_ref,
                 o_ref, kr_ref, vx_ref, s_ref, p_ref, e_ref, bias_ref, *, n_ctx, n_lat, n_kv):
    j = pl.program_id(1)
    nb_ctx = n_ctx // ATTN_BLOCK
    kvw = n_kv * HEAD_DIM
    n_win = ATTN_BLOCK + 2 * WINDOW
    qscale = HEAD_DIM ** -0.5 * LOG2E
    rows_q = Q_PER_KV * ATTN_BLOCK
    vxw = 2 * HEAD_DIM
    nt = (((1,), (1,)), ((), ()))

    @pl.when(j == 0)
    def _():
        for h in range(n_kv):
            sl = slice(h * HEAD_DIM, (h + 1) * HEAD_DIM)
            k = kv_ref[0, n_ctx:, sl].astype(F32)
            kr_ref[:, sl] = _rope(k, ck_ref[...], s1k_ref[...], s2k_ref[...]).astype(BF16)
            vx_ref[:, h * vxw:h * vxw + HEAD_DIM] = kv_ref[0, :, kvw + h * HEAD_DIM:kvw + (h + 1) * HEAD_DIM]
            vx_ref[:, h * vxw + HEAD_DIM:(h + 1) * vxw] = jnp.ones((n_ctx + n_lat, HEAD_DIM), BF16)

    def logits(h, qh, kparts):
        cols = []
        off = 0
        for kp in kparts:
            n = kp.shape[0]
            s_ref[h % 2, :, off:off + n] = lax.dot_general(qh, kp, nt, preferred_element_type=F32)
            cols.append((off, n))
            off += n
        return cols

    def softmax_rows(h, cols, use_bias, blocks):
        for rb in blocks:
            r0 = rb * ATTN_ROWS
            sk = sink_ref[h * Q_PER_KV + r0 // ATTN_BLOCK] * LOG2E
            parts = []
            for idx, (o_, n) in enumerate(cols):
                s = s_ref[h % 2, r0:r0 + ATTN_ROWS, o_:o_ + n]
                if use_bias and idx == 0:
                    ql0 = r0 % ATTN_BLOCK
                    s = s + bias_ref[ql0:ql0 + ATTN_ROWS, :]
                parts.append(s)
            m = jnp.max(parts[0], axis=-1, keepdims=True)
            for s in parts[1:]:
                m = jnp.maximum(m, jnp.max(s, axis=-1, keepdims=True))
            m = jnp.maximum(m, sk)
            for (o_, n), s in zip(cols, parts):
                p_ref[h % 2, r0:r0 + ATTN_ROWS, o_:o_ + n] = jnp.exp2(s - m).astype(BF16)
            e_ref[h % 2, r0:r0 + ATTN_ROWS, :] = jnp.exp2(sk - m)

    def weighted_values(h, cols, vparts):
        ox = None
        for (o_, n), vp in zip(cols, vparts):
            part = jnp.dot(p_ref[h % 2, :, o_:o_ + n], vp, preferred_element_type=F32)
            ox = part if ox is None else ox + part
        l = ox[:, HEAD_DIM:HEAD_DIM + 1] + e_ref[h % 2]
        o = ox[:, :HEAD_DIM] * (1.0 / l)
        for g in range(Q_PER_KV):
            hq = h * Q_PER_KV + g
            o_ref[0, :, hq * HEAD_DIM:(hq + 1) * HEAD_DIM] = (
                o[g * ATTN_BLOCK:(g + 1) * ATTN_BLOCK].astype(o_ref.dtype))

    def attend_all(q_fn, k_fn, v_fn, use_bias):
        n_blocks = rows_q // ATTN_ROWS
        cols = [logits(0, q_fn(0), k_fn(0))]
        for h in range(n_kv):
            softmax_rows(h, cols[h], use_bias, range(0, n_blocks // 2))
            if h + 1 < n_kv:
                cols.append(logits(h + 1, q_fn(h + 1), k_fn(h + 1)))
            softmax_rows(h, cols[h], use_bias, range(n_blocks // 2, n_blocks))
            if h >= 1:
                weighted_values(h - 1, cols[h - 1], v_fn(h - 1))
        weighted_values(n_kv - 1, cols[n_kv - 1], v_fn(n_kv - 1))

    def q_head(hq):
        return q_ref[0, :, hq * HEAD_DIM:(hq + 1) * HEAD_DIM].astype(F32)

    def head_slice(h):
        return slice(h * HEAD_DIM, (h + 1) * HEAD_DIM)

    @pl.when(j < nb_ctx)
    def _():
        attend_all(
            lambda h: jnp.concatenate([(q_head(h * Q_PER_KV + g) * qscale).astype(BF16)
                                       for g in range(Q_PER_KV)], axis=0),
            lambda h: [kv_ref[0, :n_ctx, head_slice(h)]],
            lambda h: [vx_ref[:n_ctx, h * vxw:(h + 1) * vxw]], False)

    @pl.when(j >= nb_ctx)
    def _():
        jl = j - nb_ctx
        start = jnp.clip((jl - 1) * ATTN_BLOCK, 0, n_lat - n_win)
        start = pl.multiple_of(start, ATTN_BLOCK)
        delta = jl * ATTN_BLOCK - start
        ql = lax.broadcasted_iota(jnp.int32, (ATTN_BLOCK, n_win), 0)
        kl = lax.broadcasted_iota(jnp.int32, (ATTN_BLOCK, n_win), 1)
        bias_ref[...] = jnp.where(jnp.abs(delta + ql - kl) <= WINDOW, 0.0, NEG_INF)
        cq = cq_ref[...] * qscale
        s1q = s1q_ref[...] * qscale
        s2q = s2q_ref[...] * qscale
        attend_all(
            lambda h: jnp.concatenate([_rope(q_head(h * Q_PER_KV + g), cq, s1q, s2q).astype(BF16)
                                       for g in range(Q_PER_KV)], axis=0),
            lambda h: [kr_ref[pl.ds(start, n_win), head_slice(h)], kv_ref[0, :n_ctx, head_slice(h)]],
            lambda h: [vx_ref[pl.ds(n_ctx + start, n_win), h * vxw:(h + 1) * vxw],
                       vx_ref[:n_ctx, h * vxw:(h + 1) * vxw]], True)


def _attention(pz, sink, n_ctx, e):
    bsz, t, _ = pz.shape
    n_lat = t - n_ctx
    n_q = e // HEAD_DIM
    n_kv = n_q // Q_PER_KV
    kvw = n_kv * HEAD_DIM
    n_win = ATTN_BLOCK + 2 * WINDOW
    rows_q = Q_PER_KV * ATTN_BLOCK
    assert (2 * e) % (2 * kvw) == 0 and n_lat >= n_win
    cos, s1, s2 = _rope_tables(n_lat)
    nb_ctx = n_ctx // ATTN_BLOCK
    qtab = pl.BlockSpec((ATTN_BLOCK, HEAD_DIM), lambda b, j: (jnp.maximum(j - nb_ctx, 0), 0))
    ktab = pl.BlockSpec((n_lat, HEAD_DIM), lambda b, j: (0, 0))
    return pl.pallas_call(
        functools.partial(_attn_kernel, n_ctx=n_ctx, n_lat=n_lat, n_kv=n_kv),
        grid=(bsz, t // ATTN_BLOCK),
        in_specs=[pl.BlockSpec(memory_space=pltpu.SMEM),
                  pl.BlockSpec((1, ATTN_BLOCK, e), lambda b, j: (b, j, 1)),
                  pl.BlockSpec((1, t, 2 * kvw), lambda b, j: (b, 0, 2 * e // (2 * kvw))),
                  qtab, qtab, qtab, ktab, ktab, ktab],
        out_specs=pl.BlockSpec((1, ATTN_BLOCK, e), lambda b, j: (b, j, 0)),
        out_shape=jax.ShapeDtypeStruct((bsz, t, e), BF16),
        scratch_shapes=[pltpu.VMEM((n_lat, kvw), BF16),
                        pltpu.VMEM((t, n_kv * 2 * HEAD_DIM), BF16),
                        pltpu.VMEM((2, rows_q, n_win + n_ctx), F32),
                        pltpu.VMEM((2, rows_q, n_win + n_ctx), BF16),
                        pltpu.VMEM((2, rows_q, 1), F32),
                        pltpu.VMEM((ATTN_BLOCK, n_win), F32)],
        compiler_params=_cparams(2),
    )(sink.astype(F32), pz, pz, cos, s1, s2, cos, s1, s2)


def _cg_order(w, axis):
    axis = axis % w.ndim
    n = w.shape[axis]
    shp = w.shape[:axis] + (n // S5_GROUP, S5_GROUP) + w.shape[axis + 1:]
    return jnp.swapaxes(w.reshape(shp), axis, axis + 1).reshape(w.shape)


def _s5_operators(lam_re, lam_im, log_step, b_re, b_im, c_re, c_im, d_skip):
    n_g = lam_re.shape[1]
    tc = S5_CHUNK
    dt = jnp.exp(log_step.astype(F32))[..., None]
    lr, li = lam_re.astype(F32) * dt, lam_im.astype(F32) * dt
    k = jnp.arange(tc + 1, dtype=F32)[:, None, None, None]
    mag = jnp.exp(lr[None] * k)
    pr, pi = mag * jnp.cos(li[None] * k), mag * jnp.sin(li[None] * k)
    ar1, ai1 = pr[1] - 1.0, pi[1]
    den = lam_re.astype(F32) ** 2 + lam_im.astype(F32) ** 2
    cr = (ar1 * lam_re + ai1 * lam_im) / den
    ci = (ai1 * lam_re - ar1 * lam_im) / den
    br, bi = b_re.astype(F32), b_im.astype(F32)
    bbr = cr[..., None] * br - ci[..., None] * bi
    bbi = cr[..., None] * bi + ci[..., None] * br
    ccr, cci = c_re.astype(F32), c_im.astype(F32)

    def kern(d):
        xr = pr[:tc, d][..., None] * bbr[d][None] - pi[:tc, d][..., None] * bbi[d][None]
        xi = pr[:tc, d][..., None] * bbi[d][None] + pi[:tc, d][..., None] * bbr[d][None]
        return (jnp.einsum('gnp,tgpm->tgnm', ccr[d], xr) - jnp.einsum('gnp,tgpm->tgnm', cci[d], xi))

    kf, kb = kern(0), kern(1)
    s_idx = jnp.arange(tc)[:, None]
    t_idx = jnp.arange(tc)[None, :]
    lag = t_idx - s_idx
    mf = jnp.where((lag >= 0)[:, :, None, None, None], kf[jnp.clip(lag, 0, tc - 1)], 0.0)
    mb = jnp.where((lag <= 0)[:, :, None, None, None], kb[jnp.clip(-lag, 0, tc - 1)], 0.0)
    m = (mf + mb).transpose(2, 4, 0, 3, 1)
    eye = (jnp.eye(S5_GROUP, dtype=F32)[:, None, :, None] * jnp.eye(tc, dtype=F32)[None, :, None, :])
    m = m + eye[None] * d_skip.astype(F32).reshape(n_g, S5_GROUP, 1, 1, 1)
    m = m.reshape(n_g, tc * S5_GROUP, tc * S5_GROUP)

    def qpart(d, pw_r, pw_i):
        qr = pw_r[..., None] * bbr[d][None] - pw_i[..., None] * bbi[d][None]
        qi = pw_r[..., None] * bbi[d][None] + pw_i[..., None] * bbr[d][None]
        return qr.transpose(1, 3, 0, 2), qi.transpose(1, 3, 0, 2)

    qfr, qfi = qpart(0, pr[tc - 1 - jnp.arange(tc), 0], pi[tc - 1 - jnp.arange(tc), 0])
    qbr, qbi = qpart(1, pr[jnp.arange(tc), 1], pi[jnp.arange(tc), 1])
    q = jnp.concatenate([qfr, qbr, qfi, qbi], axis=-1).reshape(n_g, tc * S5_GROUP, 4 * S5_STATE)

    def ppart(d, pw_r, pw_i):
        xr = ccr[d][None] * pw_r[:, :, None, :] - cci[d][None] * pw_i[:, :, None, :]
        xi = ccr[d][None] * pw_i[:, :, None, :] + cci[d][None] * pw_r[:, :, None, :]
        return xr.transpose(1, 3, 2, 0), (-xi).transpose(1, 3, 2, 0)

    pfr, pfi = ppart(0, pr[1 + jnp.arange(tc), 0], pi[1 + jnp.arange(tc), 0])
    pbr, pbi = ppart(1, pr[tc - jnp.arange(tc), 1], pi[tc - jnp.arange(tc), 1])
    p = jnp.concatenate([pfr, pbr, pfi, pbi], axis=1).reshape(n_g, 4 * S5_STATE, tc * S5_GROUP)
    a = jnp.stack([jnp.concatenate([pr[tc, 0], pr[tc, 1]], axis=-1),
                   jnp.concatenate([pi[tc, 0], pi[tc, 1]], axis=-1)], axis=1)
    a = jnp.pad(a, ((0, 0), (0, 6), (0, 0)))
    return m.astype(BF16), q.astype(BF16), p.astype(BF16), a


def _s5_pack_kernel(x_ref, o_ref, f_ref, *, nb, n_g):
    half = S5_GROUP // 2
    for b in range(nb):
        xb = x_ref[b].astype(F32)
        for k in range(2):
            a = jnp.concatenate([xb[:, cc * n_g:(cc + 1) * n_g]
                                 for cc in range(k * half, (k + 1) * half)], axis=0)
            f_ref[k, pl.ds(b, n_g, stride=nb), :] = a.T
    full = jnp.concatenate([f_ref[0], f_ref[1]], axis=1)
    o_ref[...] = full.reshape(n_g, nb, S5_CHUNK * S5_GROUP).astype(o_ref.dtype)


def _s5_pack(pz, e):
    bsz, t, _ = pz.shape
    n_g = e // S5_GROUP
    n_chunks = t // S5_CHUNK
    w = S5_CHUNK * S5_GROUP
    return pl.pallas_call(
        functools.partial(_s5_pack_kernel, nb=bsz, n_g=n_g),
        grid=(n_chunks,),
        in_specs=[pl.BlockSpec((bsz, S5_CHUNK, e), lambda i: (0, i, 1))],
        out_specs=pl.BlockSpec((n_g, bsz, w), lambda i: (0, i, 0)),
        out_shape=jax.ShapeDtypeStruct((n_g, n_chunks * bsz, w), BF16),
        scratch_shapes=[pltpu.VMEM((2, n_g * bsz, w // 2), F32)],
        compiler_params=_cparams(1),
    )(pz)


def _s5_kernel(u_ref, m_ref, q_ref, p_ref, a_ref, y_ref, s_ref, h_ref, *, nb, n_chunks, nc_ctx):
    ns = S5_STATE
    u = u_ref[0]
    s_ref[...] = jnp.dot(u, q_ref[0], preferred_element_type=F32)
    ar = a_ref[0, 0:1, :]
    ai = a_ref[0, 1:2, :]
    is_fwd = lax.broadcasted_iota(jnp.int32, (nb, 2 * ns), 1) < ns

    def step(i, carry):
        hr, hi = carry
        cb = jnp.where(i < nc_ctx, nc_ctx - 1 - i, n_chunks - 1 - (i - nc_ctx))
        rf = pl.multiple_of(i * nb, nb)
        rb = pl.multiple_of(cb * nb, nb)
        h_ref[pl.ds(rf, nb), 0:ns] = hr[:, 0:ns]
        h_ref[pl.ds(rb, nb), ns:2 * ns] = hr[:, ns:2 * ns]
        h_ref[pl.ds(rf, nb), 2 * ns:3 * ns] = hi[:, 0:ns]
        h_ref[pl.ds(rb, nb), 3 * ns:4 * ns] = hi[:, ns:2 * ns]
        sf = s_ref[pl.ds(rf, nb), :]
        sb = s_ref[pl.ds(rb, nb), :]
        sr = jnp.where(is_fwd, sf[:, :2 * ns], sb[:, :2 * ns])
        si = jnp.where(is_fwd, sf[:, 2 * ns:], sb[:, 2 * ns:])
        return ar * hr - ai * hi + sr, ar * hi + ai * hr + si

    zero = jnp.zeros((nb, 2 * ns), F32)
    lax.fori_loop(0, n_chunks, step, (zero, zero), unroll=2)
    y = jnp.dot(u, m_ref[0], preferred_element_type=F32)
    y = y + jnp.dot(h_ref[...].astype(BF16), p_ref[0], preferred_element_type=F32)
    y_ref[0] = y.astype(y_ref.dtype)


def _s5_scan(uf, m, q, p, a, nb, nc_ctx):
    n_g, rows, w = uf.shape
    n_chunks = rows // nb
    wspec = pl.BlockSpec((1, w, w), lambda g: (g, 0, 0))
    return pl.pallas_call(
        functools.partial(_s5_kernel, nb=nb, n_chunks=n_chunks, nc_ctx=nc_ctx),
        grid=(n_g,),
        in_specs=[pl.BlockSpec((1, rows, w), lambda g: (g, 0, 0)), wspec, wspec, wspec,
                  pl.BlockSpec((1, 8, 2 * S5_STATE), lambda g: (g, 0, 0))],
        out_specs=pl.BlockSpec((1, rows, w), lambda g: (g, 0, 0)),
        out_shape=jax.ShapeDtypeStruct((n_g, rows, w), BF16),
        scratch_shapes=[pltpu.VMEM((rows, w), F32), pltpu.VMEM((rows, w), F32)],
        compiler_params=_cparams(1),
    )(uf, m, q, p, a)


def _s5_glu_kernel(y0_ref, yn_ref, w_ref, b_ref, o_ref, f_ref, g_ref, *, nb, n_g):
    i = pl.program_id(0)
    w = S5_CHUNK * S5_GROUP
    half = S5_GROUP // 2
    e = n_g * S5_GROUP

    def widen(y_ref):
        yv = y_ref[...].astype(F32).reshape(n_g * nb, w)
        for k in range(2):
            f_ref[k] = yv[:, k * (w // 2):(k + 1) * (w // 2)]

    def build(slot, batches):
        for b in batches:
            for k in range(2):
                t = f_ref[k, pl.ds(b, n_g, stride=nb), :].T
                g = 0.5 * t * (1.0 + jnp.tanh(math.sqrt(2.0 / math.pi) * (t + 0.044715 * t * t * t)))
                g = g.astype(BF16)
                for cc in range(half):
                    ch = k * half + cc
                    g_ref[slot, b * S5_CHUNK:(b + 1) * S5_CHUNK, ch * n_g:(ch + 1) * n_g] = (
                        g[cc * S5_CHUNK:(cc + 1) * S5_CHUNK, :])

    @pl.when(i == 0)
    def _():
        widen(y0_ref)
        build(0, range(nb))

    cur = i % 2
    nxt = 1 - cur
    n_pieces = GLU_PIECES
    tn = e // n_pieces
    per = nb // n_pieces
    widen(yn_ref)
    for p in range(n_pieces):
        cols = slice(p * tn, (p + 1) * tn)
        g = g_ref[cur]
        lin = jnp.dot(g, w_ref[:, cols], preferred_element_type=F32) + b_ref[:, cols]
        out = (g[:, cols].astype(F32) * _sigmoid(lin)).astype(o_ref.dtype)
        o_ref[:, :, cols] = out.reshape(nb, S5_CHUNK, tn)
        build(nxt, range(p * per, (p + 1) * per))


def _s5_glu(yf, w_bf, bias, bsz):
    n_g, rows, w = yf.shape
    e = n_g * S5_GROUP
    n_chunks = rows // bsz
    assert bsz % GLU_PIECES == 0 and e % GLU_PIECES == 0
    return pl.pallas_call(
        functools.partial(_s5_glu_kernel, nb=bsz, n_g=n_g),
        grid=(n_chunks,),
        in_specs=[pl.BlockSpec((n_g, bsz, w), lambda i: (0, 0, 0)),
                  pl.BlockSpec((n_g, bsz, w), lambda i: (0, jnp.minimum(i + 1, n_chunks - 1), 0)),
                  pl.BlockSpec((e, e), lambda i: (0, 0)),
                  pl.BlockSpec((1, e), lambda i: (0, 0))],
        out_specs=pl.BlockSpec((bsz, S5_CHUNK, e), lambda i: (0, i, 0)),
        out_shape=jax.ShapeDtypeStruct((bsz, n_chunks * S5_CHUNK, e), BF16),
        scratch_shapes=[pltpu.VMEM((2, n_g * bsz, w // 2), F32),
                        pltpu.VMEM((2, bsz * S5_CHUNK, e), BF16)],
        compiler_params=_cparams(1),
    )(yf, yf, w_bf, bias.astype(F32).reshape(1, e))


def _s5_mixer(pz, n_ctx, e, ops, glu_w_bf, glu_b):
    bsz = pz.shape[0]
    m, q, p, a = ops
    uf = _s5_pack(pz, e)
    yf = _s5_scan(uf, m, q, p, a, bsz, n_ctx // S5_CHUNK)
    return _s5_glu(yf, glu_w_bf, glu_b, bsz)


def _cumsum_rows(x, reverse):
    n = x.shape[0]
    row = lax.broadcasted_iota(jnp.int32, x.shape, 0)
    s = 1
    while s < n:
        if reverse:
            x = x + jnp.where(row < n - s, pltpu.roll(x, n - s, 0), 0.0)
        else:
            x = x + jnp.where(row >= s, pltpu.roll(x, s, 0), 0.0)
        s *= 2
    return x


def _hgrn_kernel(q_ref, ff_ref, fb_ref, v_ref, lb_ref, ng_ref, o_ref,
                 ds_ref, dec_ref, st_ref, qd_ref, oi_ref, *, layer, n_chunks, nc_ctx, n_tok):
    c = HG_CHUNK
    hd = HG_HEAD
    nt = (((1,), (1,)), ((), ()))
    tn = (((0,), (0,)), ((), ()))

    lbw = lb_ref[...].astype(F32)
    ew = jnp.exp(lbw - jnp.max(lbw, axis=0, keepdims=True))
    lb = jnp.sum(ew[1:layer + 1], axis=0, keepdims=True) / jnp.sum(ew, axis=0, keepdims=True)

    ti = lax.broadcasted_iota(jnp.int32, (c, 2 * c), 0)
    si = lax.broadcasted_iota(jnp.int32, (c, 2 * c), 1)
    m_fwd = si <= ti
    m_bwd = si - c >= ti

    def chunk_a(ci):
        r0 = pl.multiple_of(ci * c, c)
        q = q_ref[0, pl.ds(r0, c), :].astype(F32)
        v = v_ref[0, pl.ds(r0, c), :]
        qes, kes, kds = [], [], []
        for d, f_ref in enumerate((ff_ref, fb_ref)):
            fl = f_ref[0, pl.ds(r0, c), :].astype(F32)
            f = lb + (1.0 - lb) * _sigmoid(fl)
            b = _cumsum_rows(jnp.log(f), reverse=bool(d))
            mid = c // 2 if d == 0 else c // 2 - 1
            last = c - 1 if d == 0 else 0
            ref = b[mid:mid + 1]
            b_last = b[last:last + 1]
            qe = q * jnp.exp(b - ref)
            ke = (1.0 - f) * jnp.exp(ref - b)
            qes.append(qe.astype(BF16))
            kes.append(ke.astype(BF16))
            qd_ref[pl.ds(r0, c), d * hd:(d + 1) * hd] = (qe * jnp.exp(ref)).astype(BF16)
            kds.append((ke * jnp.exp(b_last - ref)).astype(BF16))
            dec_ref[ci, :, d * hd:(d + 1) * hd] = jnp.exp(b_last)
        a = lax.dot_general(jnp.concatenate(qes, axis=0), jnp.concatenate(kes, axis=0), nt,
                            preferred_element_type=F32)
        att = jnp.where(m_fwd, a[:c], jnp.where(m_bwd, a[c:], 0.0))
        oi_ref[pl.ds(r0, c), :] = jnp.dot(att.astype(BF16), jnp.concatenate([v, v], axis=0),
                                          preferred_element_type=F32)
        ds_ref[ci] = lax.dot_general(v, jnp.concatenate(kds, axis=1), tn,
                                     preferred_element_type=F32)

    def loop_a(i, carry):
        for u in range(HG_UNROLL):
            chunk_a(i * HG_UNROLL + u)
        return carry

    lax.fori_loop(0, n_chunks // HG_UNROLL, loop_a, 0)

    def scan_b(i, carry):
        sf, sb = carry
        cb = jnp.where(i < nc_ctx, nc_ctx - 1 - i, n_chunks - 1 - (i - nc_ctx))
        st_ref[i, :, 0:hd] = sf.astype(BF16)
        st_ref[cb, :, hd:2 * hd] = sb.astype(BF16)
        sf = sf * dec_ref[i, :, 0:hd] + ds_ref[i, :, 0:hd]
        sb = sb * dec_ref[cb, :, hd:2 * hd] + ds_ref[cb, :, hd:2 * hd]
        return sf, sb

    zero = jnp.zeros((hd, hd), F32)
    lax.fori_loop(0, n_chunks, scan_b, (zero, zero))

    def chunk_c(ci):
        r0 = pl.multiple_of(ci * c, c)
        oi_ref[pl.ds(r0, c), :] += lax.dot_general(
            qd_ref[pl.ds(r0, c), :], st_ref[ci], nt, preferred_element_type=F32)

    def loop_c(i, carry):
        for u in range(HG_UNROLL):
            chunk_c(i * HG_UNROLL + u)
        return carry

    lax.fori_loop(0, n_chunks // HG_UNROLL, loop_c, 0)

    gn = ng_ref[...].astype(F32)
    for r0 in range(0, n_tok, HG_NORM_ROWS):
        o = oi_ref[r0:r0 + HG_NORM_ROWS, :]
        o = o * lax.rsqrt(jnp.mean(o * o, axis=-1, keepdims=True) + NORM_EPS) * gn
        o_ref[0, r0:r0 + HG_NORM_ROWS, :] = o.astype(o_ref.dtype)


def _hgrn(pz, hg_lb, norm_g, layer, n_ctx, e):
    bsz, t, _ = pz.shape
    n_h = e // HG_HEAD
    n_chunks = t // HG_CHUNK
    depth = hg_lb.shape[0]
    hd = HG_HEAD
    assert n_chunks % HG_UNROLL == 0 and t % HG_NORM_ROWS == 0

    def col(k):
        return pl.BlockSpec((1, t, hd), lambda b, h: (b, 0, (k + 1) * n_h + h))

    return pl.pallas_call(
        functools.partial(_hgrn_kernel, layer=layer, n_chunks=n_chunks, nc_ctx=n_ctx // HG_CHUNK, n_tok=t),
        grid=(bsz, n_h),
        in_specs=[col(0), col(1), col(2), col(3),
                  pl.BlockSpec((depth, hd), lambda b, h: (0, h)),
                  pl.BlockSpec((1, hd), lambda b, h: (0, h))],
        out_specs=pl.BlockSpec((1, t, hd), lambda b, h: (b, 0, h)),
        out_shape=jax.ShapeDtypeStruct((bsz, t, e), BF16),
        scratch_shapes=[pltpu.VMEM((n_chunks, hd, 2 * hd), F32),
                        pltpu.VMEM((n_chunks, 1, 2 * hd), F32),
                        pltpu.VMEM((n_chunks, hd, 2 * hd), BF16),
                        pltpu.VMEM((t, 2 * hd), BF16),
                        pltpu.VMEM((t, hd), F32)],
        compiler_params=_cparams(2),
    )(pz, pz, pz, pz, hg_lb.astype(F32), norm_g.astype(F32).reshape(1, e))


def _cumsum_rows_multi(xs, reverse):
    n = xs[0].shape[0]
    row = lax.broadcasted_iota(jnp.int32, xs[0].shape, 0)
    s = 1
    while s < n:
        nxt = []
        for x, rev in zip(xs, reverse):
            if rev:
                nxt.append(x + jnp.where(row < n - s, pltpu.roll(x, n - s, 0), 0.0))
            else:
                nxt.append(x + jnp.where(row >= s, pltpu.roll(x, s, 0), 0.0))
        xs = nxt
        s *= 2
    return xs


def _hgrn2_kernel(xa_ref, mb_ref, mc_ref, w0_ref, wn_ref, lb_ref, ng_ref, o_ref,
                  h_ref, p_ref, g_ref, k_ref, qe_ref, ke_ref, qd_ref, kd_ref, ds_ref, dec_ref, st_ref,
                  oi_ref, *, layer, n_chunks, nc_ctx, n_tok, n_ctx, d_model):
    c = HG_CHUNK
    hd = HG_HEAD
    nt = (((1,), (1,)), ((), ()))
    tn = (((0,), (0,)), ((), ()))
    head = pl.program_id(1)
    cur = head % 2
    nxt = 1 - cur

    @pl.when(head == 0)
    def _():
        mb = mb_ref[0]
        mc = mc_ref[0]
        dm = d_model
        h_ref[:n_ctx] = (xa_ref[0, :n_ctx] * (1.0 + mc[:, dm:2 * dm]) + mc[:, :dm]).astype(BF16)
        h_ref[n_ctx:] = (xa_ref[0, n_ctx:] * (1.0 + mb[:, dm:2 * dm]) + mb[:, :dm]).astype(BF16)
        p_ref[0] = jnp.dot(h_ref[...], w0_ref[...], preferred_element_type=F32).astype(BF16)

    piece_rows = n_tok // HG_PROJ_PIECES

    def project_next(piece):
        rows = slice(piece * piece_rows, (piece + 1) * piece_rows)
        p_ref[nxt, rows, :] = jnp.dot(h_ref[rows, :], wn_ref[...],
                                      preferred_element_type=F32).astype(BF16)

    lbw = lb_ref[...].astype(F32)
    ew = jnp.exp(lbw - jnp.max(lbw, axis=0, keepdims=True))
    lb = jnp.sum(ew[1:layer + 1], axis=0, keepdims=True) / jnp.sum(ew, axis=0, keepdims=True)

    def gates(ci):
        r0 = pl.multiple_of(ci * c, c)
        for d in range(2):
            fl = p_ref[cur, pl.ds(r0, c), (1 + d) * hd:(2 + d) * hd].astype(F32)
            f = lb + (1.0 - lb) * _sigmoid(fl)
            g_ref[d, pl.ds(r0, c), :] = jnp.log2(f)
            k_ref[d, pl.ds(r0, c), :] = 1.0 - f

    def cumdecay(cis):
        r0s = [pl.multiple_of(ci * c, c) for ci in cis]
        xs = [g_ref[d, pl.ds(r0, c), :] for r0 in r0s for d in range(2)]
        xs = _cumsum_rows_multi(xs, [False, True] * len(cis))
        k = 0
        for r0 in r0s:
            for d in range(2):
                g_ref[d, pl.ds(r0, c), :] = xs[k]
                k += 1

    def decayed(ci):
        r0 = pl.multiple_of(ci * c, c)
        q = p_ref[cur, pl.ds(r0, c), 0:hd].astype(F32)
        for d in range(2):
            mid = c // 2 if d == 0 else c // 2 - 1
            last = c - 1 if d == 0 else 0
            b = g_ref[d, pl.ds(r0, c), :]
            ref = g_ref[d, pl.ds(r0 + mid, 1), :]
            b_last = g_ref[d, pl.ds(r0 + last, 1), :]
            qe = q * jnp.exp2(b - ref)
            ke = k_ref[d, pl.ds(r0, c), :] * jnp.exp2(ref - b)
            qe_ref[d, pl.ds(r0, c), :] = qe.astype(BF16)
            ke_ref[d, pl.ds(r0, c), :] = ke.astype(BF16)
            qd_ref[pl.ds(r0, c), d * hd:(d + 1) * hd] = (qe * jnp.exp2(ref)).astype(BF16)
            kd_ref[pl.ds(r0, c), d * hd:(d + 1) * hd] = (ke * jnp.exp2(b_last - ref)).astype(BF16)
            dec_ref[ci, :, d * hd:(d + 1) * hd] = jnp.exp2(b_last)

    def prepare(cis):
        for ci in cis:
            gates(ci)
        cumdecay(cis)
        for ci in cis:
            decayed(ci)

    ti = lax.broadcasted_iota(jnp.int32, (c, 2 * c), 0)
    si = lax.broadcasted_iota(jnp.int32, (c, 2 * c), 1)
    m_fwd = si <= ti
    m_bwd = si - c >= ti

    def logits(cis):
        r0s = [pl.multiple_of(ci * c, c) for ci in cis]
        vs = [p_ref[cur, pl.ds(r0, c), 3 * hd:4 * hd] for r0 in r0s]
        aa = [lax.dot_general(
            jnp.concatenate([qe_ref[0, pl.ds(r0, c), :], qe_ref[1, pl.ds(r0, c), :]], axis=0),
            jnp.concatenate([ke_ref[0, pl.ds(r0, c), :], ke_ref[1, pl.ds(r0, c), :]], axis=0),
            nt, preferred_element_type=F32) for r0 in r0s]
        dss = [lax.dot_general(v, kd_ref[pl.ds(r0, c), :], tn, preferred_element_type=F32)
               for v, r0 in zip(vs, r0s)]
        return r0s, vs, aa, dss

    def values(cis, vs, aa, dss):
        atts = [jnp.where(m_fwd, a[:c], jnp.where(m_bwd, a[c:], 0.0)).astype(BF16) for a in aa]
        ois = [jnp.dot(att, jnp.concatenate([v, v], axis=0), preferred_element_type=F32)
               for att, v in zip(atts, vs)]
        for ci, ds in zip(cis, dss):
            ds_ref[ci] = ds
        return tuple(ois)

    def store_intra(cis, ois):
        for ci, oi in zip(cis, ois):
            oi_ref[pl.ds(pl.multiple_of(ci * c, c), c), :] = oi

    n_pairs = n_chunks // HG_PAIR

    def pair(t):
        return [t * HG_PAIR + u for u in range(HG_PAIR)]

    def stage(t, pending, with_next):
        cur = pair(t)
        _, vs, aa, dss = logits(cur)
        if pending is not None:
            store_intra(pair(t - 1), pending)
        if with_next:
            nxt = pair(t + 1)
            for ci in nxt:
                gates(ci)
            cumdecay(nxt)
        ois = values(cur, vs, aa, dss)
        if with_next:
            for ci in nxt:
                decayed(ci)
        return ois

    prepare(pair(0))
    pending = stage(0, None, True)
    proj_at = {}
    for k in range(HG_PROJ_PIECES):
        proj_at.setdefault(1 + (k * (n_pairs - 2)) // HG_PROJ_PIECES, []).append(k)
    for t in range(1, n_pairs - 1):
        for k in proj_at.get(t, []):
            project_next(k)
        pending = stage(t, pending, True)
    pending = stage(n_pairs - 1, pending, False)
    store_intra(pair(n_pairs - 1), pending)

    def scan_b(i, carry):
        sf, sb = carry
        cb = jnp.where(i < nc_ctx, nc_ctx - 1 - i, n_chunks - 1 - (i - nc_ctx))
        st_ref[i, :, 0:hd] = sf.astype(BF16)
        st_ref[cb, :, hd:2 * hd] = sb.astype(BF16)
        sf = sf * dec_ref[i, :, 0:hd] + ds_ref[i, :, 0:hd]
        sb = sb * dec_ref[cb, :, hd:2 * hd] + ds_ref[cb, :, hd:2 * hd]
        return sf, sb

    zero = jnp.zeros((hd, hd), F32)
    lax.fori_loop(0, n_chunks, scan_b, (zero, zero))

    def inter(cis):
        os_ = [lax.dot_general(qd_ref[ci * c:(ci + 1) * c, :], st_ref[ci], nt, preferred_element_type=F32)
               for ci in cis]
        for ci, o in zip(cis, os_):
            oi_ref[ci * c:(ci + 1) * c, :] += o

    gn = ng_ref[...].astype(F32)

    def head_norm(r0):
        o = oi_ref[r0:r0 + HG_NORM_ROWS, :]
        o = o * lax.rsqrt(jnp.mean(o * o, axis=-1, keepdims=True) + NORM_EPS) * gn
        o_ref[0, r0:r0 + HG_NORM_ROWS, :] = o.astype(o_ref.dtype)

    group_rows = HG_UNROLL_INTER * c
    n_groups = n_chunks // HG_UNROLL_INTER
    for gi in range(n_groups + 1):
        if gi < n_groups:
            inter(range(gi * HG_UNROLL_INTER, (gi + 1) * HG_UNROLL_INTER))
        if gi >= 1:
            for r0 in range((gi - 1) * group_rows, gi * group_rows, HG_NORM_ROWS):
                head_norm(r0)


def _hgrn2(xa, mod3, mod_rows, w_heads, hg_lb, norm_g, layer, n_ctx, e):
    bsz, t, d = xa.shape
    n_h = e // HG_HEAD
    n_chunks = t // HG_CHUNK
    depth = hg_lb.shape[0]
    hd = HG_HEAD
    base = layer * mod_rows
    assert n_chunks % HG_PAIR == 0 and n_chunks // HG_PAIR >= 3
    assert n_chunks % HG_UNROLL_INTER == 0 and t % HG_NORM_ROWS == 0 and t % (16 * HG_PROJ_PIECES) == 0

    return pl.pallas_call(
        functools.partial(_hgrn2_kernel, layer=layer, n_chunks=n_chunks, nc_ctx=n_ctx // HG_CHUNK,
                          n_tok=t, n_ctx=n_ctx, d_model=d),
        grid=(bsz, n_h),
        in_specs=[pl.BlockSpec((1, t, d), lambda b, h: (b, 0, 0)),
                  pl.BlockSpec((1, 1, 3 * d), lambda b, h: (base + b, 0, 0)),
                  pl.BlockSpec((1, 1, 3 * d), lambda b, h: (base + bsz, 0, 0)),
                  pl.BlockSpec((d, 4 * hd), lambda b, h: (0, 0)),
                  pl.BlockSpec((d, 4 * hd), lambda b, h: (0, jnp.minimum(h + 1, n_h - 1))),
                  pl.BlockSpec((depth, hd), lambda b, h: (0, h)),
                  pl.BlockSpec((1, hd), lambda b, h: (0, h))],
        out_specs=pl.BlockSpec((1, t, hd), lambda b, h: (b, 0, h)),
        out_shape=jax.ShapeDtypeStruct((bsz, t, e), BF16),
        scratch_shapes=[pltpu.VMEM((t, d), BF16),
                        pltpu.VMEM((2, t, 4 * hd), BF16),
                        pltpu.VMEM((2, t, hd), F32),
                        pltpu.VMEM((2, t, hd), F32),
                        pltpu.VMEM((2, t, hd), BF16),
                        pltpu.VMEM((2, t, hd), BF16),
                        pltpu.VMEM((t, 2 * hd), BF16),
                        pltpu.VMEM((t, 2 * hd), BF16),
                        pltpu.VMEM((n_chunks, hd, 2 * hd), F32),
                        pltpu.VMEM((n_chunks, 1, 2 * hd), F32),
                        pltpu.VMEM((n_chunks, hd, 2 * hd), BF16),
                        pltpu.VMEM((t, hd), F32)],
        compiler_params=_cparams(2),
    )(xa, mod3, mod3, w_heads, w_heads, hg_lb.astype(F32), norm_g.astype(F32).reshape(1, e))


def _head_major(w, e):
    d = w.shape[0]
    n_h = e // HG_HEAD
    rest = w[:, e:].reshape(d, 4, n_h, HG_HEAD).transpose(0, 2, 1, 3).reshape(d, 4 * e)
    return jnp.concatenate([w[:, :e], rest], axis=1)


def _z_first(w, e):
    return jnp.concatenate([w[..., -e:], w[..., :-e]], axis=-1)


def kernel(x, c, ctx, c_ctx, ada_w, ada_b, ln_g, ln_b, w_out, attn_w_in, attn_sink, s5_w_in, s5_lam_re, s5_lam_im, s5_log_step, s5_b_re, s5_b_im, s5_c_re, s5_c_im, s5_d, s5_glu_w, s5_glu_b, hg_w_in, hg_lb, hg_norm_g):
    bsz, n_lat, d = x.shape
    n_ctx = ctx.shape[1]
    depth = ada_w.shape[0]
    e = w_out.shape[1]
    alpha = (2.0 * depth) ** 0.25
    t = n_ctx + n_lat
    tm = 768 if t % 768 == 0 else 256

    mod_rows = -(-(bsz + 1) // MOD_ROWS_PAD) * MOD_ROWS_PAD
    c_all = jnp.concatenate([c.astype(F32), c_ctx.astype(F32)[None],
                             jnp.zeros((mod_rows - bsz - 1, d), F32)], axis=0)
    mod = _ada(c_all, ada_w.astype(F32), ada_b.astype(F32))
    mod3 = mod.reshape(depth * mod_rows, 1, 3 * d)

    xa = jnp.concatenate([ctx.astype(F32), x.astype(F32)], axis=1)
    for i in range(depth):
        kind, j = i % N_MIXERS, i // N_MIXERS
        w_in = _z_first((attn_w_in, s5_w_in, hg_w_in)[kind][j], e)
        w_o = w_out[i]
        if kind == 1:
            w_in = jnp.concatenate([_cg_order(w_in[:, :e], 1), _cg_order(w_in[:, e:], 1)], axis=1)
            w_o = _cg_order(w_o, 0)
        elif kind == 2:
            w_heads = _head_major(w_in, e)[:, e:].astype(BF16)
            w_in = w_in[:, :e]
        pz = _inproj(xa, mod3, i, mod_rows, w_in.astype(BF16), n_ctx, tn=INPROJ_TN)
        if kind == 0:
            y = _attention(pz, attn_sink[j], n_ctx, e)
        elif kind == 1:
            ops = _s5_operators(s5_lam_re[j], s5_lam_im[j], s5_log_step[j], s5_b_re[j], s5_b_im[j],
                                s5_c_re[j], s5_c_im[j], s5_d[j])
            glu_w = _cg_order(_cg_order(s5_glu_w[j], 0), 1).astype(BF16)
            y = _s5_mixer(pz, n_ctx, e, ops, glu_w, _cg_order(s5_glu_b[j], 0))
        else:
            y = _hgrn2(xa, mod3, mod_rows, w_heads, hg_lb, hg_norm_g[j], i, n_ctx, e)
        xa = _outproj(y, pz, xa, mod3, i, mod_rows, w_o.astype(BF16), ln_g[i].astype(F32),
                      ln_b[i].astype(F32), n_ctx, alpha, tm)
    return xa[:, n_ctx:].astype(x.dtype)
```

```python
import functools
import math

import jax
import jax.numpy as jnp
from jax import lax
from jax.experimental import pallas as pl
from jax.experimental.pallas import tpu as pltpu

F32 = jnp.float32
BF16 = jnp.bfloat16

N_MIXERS = 3
HEAD_DIM = 128
Q_PER_KV = 4
WINDOW = 128
ATTN_BLOCK = 128
GRID_W = 64
ROPE_THETA = 10000.0
S5_GROUP = 16
S5_STATE = 64
S5_CHUNK = 16
HG_HEAD = 128
HG_CHUNK = 64
HG_UNROLL_INTER = 12
HG_PAIR = 2
HG_NORM_ROWS = 256
NORM_EPS = 1e-5
NEG_INF = -1e30
LOG2E = math.log2(math.e)
ATTN_ROWS = 32
MOD_ROWS_PAD = 8
INPROJ_TN = 1024
LAST_TM = 256
GLU_PIECES = 8
VMEM_LIMIT = 56 * 1024 * 1024


def _cparams(n_axes):
    return pltpu.CompilerParams(dimension_semantics=("arbitrary",) * n_axes,
                                vmem_limit_bytes=VMEM_LIMIT)


def _sigmoid(x):
    return 1.0 / (1.0 + jnp.exp(-x))


def _ada_kernel(c_ref, w_ref, b_ref, o_ref):
    cv = c_ref[...]
    s = cv * _sigmoid(cv)
    o_ref[0] = jnp.dot(s, w_ref[0], preferred_element_type=F32,
                       precision=lax.Precision.HIGHEST) + b_ref[0]


def _ada(c_all, ada_w, ada_b):
    depth, d, d3 = ada_w.shape
    rows = c_all.shape[0]
    return pl.pallas_call(
        _ada_kernel,
        grid=(depth, d3 // d),
        in_specs=[pl.BlockSpec((rows, d), lambda i, j: (0, 0)),
                  pl.BlockSpec((1, d, d), lambda i, j: (i, 0, j)),
                  pl.BlockSpec((1, 1, d), lambda i, j: (i, 0, j))],
        out_specs=pl.BlockSpec((1, rows, d), lambda i, j: (i, 0, j)),
        out_shape=jax.ShapeDtypeStruct((depth, rows, d3), F32),
        compiler_params=_cparams(2),
    )(c_all, ada_w, ada_b.reshape(depth, 1, d3))


def _inproj_kernel(x_ref, mb_ref, mc_ref, w_ref, o_ref, h_ref, *, n_ctx, d):
    @pl.when(pl.program_id(1) == 0)
    def _():
        mb = mb_ref[0]
        mc = mc_ref[0]
        h_ref[:n_ctx] = (x_ref[0, :n_ctx] * (1.0 + mc[:, d:2 * d]) + mc[:, :d]).astype(BF16)
        h_ref[n_ctx:] = (x_ref[0, n_ctx:] * (1.0 + mb[:, d:2 * d]) + mb[:, :d]).astype(BF16)

    o_ref[0] = jnp.dot(h_ref[...], w_ref[...], preferred_element_type=F32).astype(o_ref.dtype)


def _inproj(xa, mod3, layer, mod_rows, w_bf, n_ctx, tn):
    bsz, t, d = xa.shape
    n = w_bf.shape[1]
    base = layer * mod_rows
    return pl.pallas_call(
        functools.partial(_inproj_kernel, n_ctx=n_ctx, d=d),
        grid=(bsz, n // tn),
        in_specs=[pl.BlockSpec((1, t, d), lambda b, j: (b, 0, 0)),
                  pl.BlockSpec((1, 1, 3 * d), lambda b, j: (base + b, 0, 0)),
                  pl.BlockSpec((1, 1, 3 * d), lambda b, j: (base + bsz, 0, 0)),
                  pl.BlockSpec((d, tn), lambda b, j: (0, j))],
        out_specs=pl.BlockSpec((1, t, tn), lambda b, j: (b, 0, j)),
        out_shape=jax.ShapeDtypeStruct((bsz, t, n), BF16),
        scratch_shapes=[pltpu.VMEM((t, d), BF16)],
        compiler_params=_cparams(2),
    )(xa, mod3, mod3, w_bf)


def _outproj_kernel(y_ref, z_ref, x_ref, mb_ref, mc_ref, w_ref, g_ref, b_ref, o_ref,
                    *, n_ctx, d, alpha, tm, first_tile):
    z = z_ref[0]
    a = y_ref[0] * (z * _sigmoid(z))
    br = jnp.dot(a, w_ref[...], preferred_element_type=F32)
    rows = (pl.program_id(1) + first_tile) * tm + lax.broadcasted_iota(jnp.int32, (tm, 1), 0)
    gate = jnp.where(rows < n_ctx, mc_ref[0][:, 2 * d:], mb_ref[0][:, 2 * d:])
    v = alpha * x_ref[0] + gate * br
    mu = jnp.mean(v, axis=-1, keepdims=True)
    vc = v - mu
    var = jnp.mean(vc * vc, axis=-1, keepdims=True)
    o_ref[0] = vc * lax.rsqrt(var + NORM_EPS) * g_ref[...] + b_ref[...]


def _outproj(y, pz, xa, mod3, layer, mod_rows, w_bf, ln_g, ln_b, n_ctx, alpha, tm, latent_only=False):
    bsz, t, d = xa.shape
    e = w_bf.shape[0]
    base = layer * mod_rows
    off = n_ctx // tm if latent_only else 0
    assert n_ctx % tm == 0 or not latent_only
    return pl.pallas_call(
        functools.partial(_outproj_kernel, n_ctx=n_ctx, d=d, alpha=alpha, tm=tm, first_tile=off),
        grid=(bsz, t // tm - off),
        in_specs=[pl.BlockSpec((1, tm, e), lambda b, i: (b, i, 0)),
                  pl.BlockSpec((1, tm, e), lambda b, i: (b, i + off, 0)),
                  pl.BlockSpec((1, tm, d), lambda b, i: (b, i + off, 0)),
                  pl.BlockSpec((1, 1, 3 * d), lambda b, i: (base + b, 0, 0)),
                  pl.BlockSpec((1, 1, 3 * d), lambda b, i: (base + bsz, 0, 0)),
                  pl.BlockSpec((e, d), lambda b, i: (0, 0)),
                  pl.BlockSpec((1, d), lambda b, i: (0, 0)),
                  pl.BlockSpec((1, d), lambda b, i: (0, 0))],
        out_specs=pl.BlockSpec((1, tm, d), lambda b, i: (b, i, 0)),
        out_shape=jax.ShapeDtypeStruct((bsz, t - off * tm, d), F32),
        compiler_params=_cparams(2),
    )(y, pz, xa, mod3, mod3, w_bf, ln_g.reshape(1, d), ln_b.reshape(1, d))


def _rope_tables(n_lat):
    pos = jnp.arange(n_lat, dtype=jnp.int32)
    row = (pos // GRID_W).astype(F32)
    col = (pos % GRID_W).astype(F32)
    nf = HEAD_DIM // 4
    inv_freq = jnp.power(ROPE_THETA, -jnp.arange(nf, dtype=F32) / nf)
    ang_r = row[:, None] * inv_freq[None, :]
    ang_c = col[:, None] * inv_freq[None, :]
    zeros = jnp.zeros_like(ang_r)
    cos = jnp.concatenate([jnp.cos(ang_r)] * 2 + [jnp.cos(ang_c)] * 2, axis=-1)
    s1 = jnp.concatenate([-jnp.sin(ang_r), zeros, -jnp.sin(ang_c), zeros], axis=-1)
    s2 = jnp.concatenate([zeros, jnp.sin(ang_r), zeros, jnp.sin(ang_c)], axis=-1)
    return cos, s1, s2


def _rope(x, cos, s1, s2):
    quarter = HEAD_DIM // 4
    return (x * cos + pltpu.roll(x, HEAD_DIM - quarter, 1) * s1
            + pltpu.roll(x, quarter, 1) * s2)


def _attn_kernel(sink_ref, q_ref, kv_ref, cq_ref, s1q_ref, s2q_ref, ck_ref, s1k_ref, s2k_ref,
                 o_ref, kr_ref, vx_ref, s_ref, p_ref, e_ref, bias_ref, *, n_ctx, n_lat, n_kv, first_block):
    j = pl.program_id(1) + first_block
    nb_ctx = n_ctx // ATTN_BLOCK
    kvw = n_kv * HEAD_DIM
    n_win = ATTN_BLOCK + 2 * WINDOW
    qscale = HEAD_DIM ** -0.5 * LOG2E
    rows_q = Q_PER_KV * ATTN_BLOCK
    vxw = 2 * HEAD_DIM
    nt = (((1,), (1,)), ((), ()))

    @pl.when(pl.program_id(1) == 0)
    def _():
        for h in range(n_kv):
            sl = slice(h * HEAD_DIM, (h + 1) * HEAD_DIM)
            k = kv_ref[0, n_ctx:, sl].astype(F32)
            kr_ref[:, sl] = _rope(k, ck_ref[...], s1k_ref[...], s2k_ref[...]).astype(BF16)
            vx_ref[:, h * vxw:h * vxw + HEAD_DIM] = kv_ref[0, :, kvw + h * HEAD_DIM:kvw + (h + 1) * HEAD_DIM]
            vx_ref[:, h * vxw + HEAD_DIM:(h + 1) * vxw] = jnp.ones((n_ctx + n_lat, HEAD_DIM), BF16)

    def key_cols(kparts):
        cols, off = [], 0
        for kp in kparts:
            cols.append((off, kp.shape[0]))
            off += kp.shape[0]
        return cols

    def logit_tasks(h, q_fn, kparts):
        box = {}

        def task(kp, o_, n):
            def run():
                if "q" not in box:
                    box["q"] = q_fn(h)
                s_ref[h % 2, :, o_:o_ + n] = lax.dot_general(box["q"], kp, nt, preferred_element_type=F32)
            return run

        return [task(kp, o_, n) for kp, (o_, n) in zip(kparts, key_cols(kparts))]

    def softmax_rows(h, cols, use_bias, blocks):
        for rb in blocks:
            r0 = rb * ATTN_ROWS
            sk = sink_ref[h * Q_PER_KV + r0 // ATTN_BLOCK] * LOG2E
            parts = []
            for idx, (o_, n) in enumerate(cols):
                s = s_ref[h % 2, r0:r0 + ATTN_ROWS, o_:o_ + n]
                if use_bias and idx == 0:
                    ql0 = r0 % ATTN_BLOCK
                    s = s + bias_ref[ql0:ql0 + ATTN_ROWS, :]
                parts.append(s)
            m = jnp.max(parts[0], axis=-1, keepdims=True)
            for s in parts[1:]:
                m = jnp.maximum(m, jnp.max(s, axis=-1, keepdims=True))
            m = jnp.maximum(m, sk)
            for (o_, n), s in zip(cols, parts):
                p_ref[h % 2, r0:r0 + ATTN_ROWS, o_:o_ + n] = jnp.exp2(s - m).astype(BF16)
            e_ref[h % 2, r0:r0 + ATTN_ROWS, :] = jnp.exp2(sk - m)

    def value_tasks(h, cols, vparts):
        box = {}

        def task(idx, o_, n, vp):
            def run():
                part = jnp.dot(p_ref[h % 2, :, o_:o_ + n], vp, preferred_element_type=F32)
                box["ox"] = part if idx == 0 else box["ox"] + part
                if idx == len(cols) - 1:
                    ox = box["ox"]
                    l = ox[:, HEAD_DIM:HEAD_DIM + 1] + e_ref[h % 2]
                    o = ox[:, :HEAD_DIM] * (1.0 / l)
                    for g in range(Q_PER_KV):
                        hq = h * Q_PER_KV + g
                        o_ref[0, :, hq * HEAD_DIM:(hq + 1) * HEAD_DIM] = (
                            o[g * ATTN_BLOCK:(g + 1) * ATTN_BLOCK].astype(o_ref.dtype))
            return run

        return [task(i, o_, n, vp) for i, ((o_, n), vp) in enumerate(zip(cols, vparts))]

    def attend_all(q_fn, k_fn, v_fn, use_bias):
        n_blocks = rows_q // ATTN_ROWS
        quarter = n_blocks // 4
        cols = key_cols(k_fn(0))
        for task in logit_tasks(0, q_fn, k_fn(0)):
            task()
        for h in range(n_kv + 1):
            side = []
            if h + 1 < n_kv:
                side += logit_tasks(h + 1, q_fn, k_fn(h + 1))
            if 1 <= h:
                side += value_tasks(h - 1, cols, v_fn(h - 1))
            for qi in range(4):
                if h < n_kv:
                    softmax_rows(h, cols, use_bias, range(qi * quarter, (qi + 1) * quarter))
                for task in side[qi::4]:
                    task()

    def q_head(hq):
        return q_ref[0, :, hq * HEAD_DIM:(hq + 1) * HEAD_DIM].astype(F32)

    def head_slice(h):
        return slice(h * HEAD_DIM, (h + 1) * HEAD_DIM)

    @pl.when(j < nb_ctx)
    def _():
        attend_all(
            lambda h: jnp.concatenate([(q_head(h * Q_PER_KV + g) * qscale).astype(BF16)
                                       for g in range(Q_PER_KV)], axis=0),
            lambda h: [kv_ref[0, :n_ctx, head_slice(h)]],
            lambda h: [vx_ref[:n_ctx, h * vxw:(h + 1) * vxw]], False)

    @pl.when(j >= nb_ctx)
    def _():
        jl = j - nb_ctx
        start = jnp.clip((jl - 1) * ATTN_BLOCK, 0, n_lat - n_win)
        start = pl.multiple_of(start, ATTN_BLOCK)
        delta = jl * ATTN_BLOCK - start
        ql = lax.broadcasted_iota(jnp.int32, (ATTN_BLOCK, n_win), 0)
        kl = lax.broadcasted_iota(jnp.int32, (ATTN_BLOCK, n_win), 1)
        bias_ref[...] = jnp.where(jnp.abs(delta + ql - kl) <= WINDOW, 0.0, NEG_INF)
        cq = cq_ref[...] * qscale
        s1q = s1q_ref[...] * qscale
        s2q = s2q_ref[...] * qscale
        attend_all(
            lambda h: jnp.concatenate([_rope(q_head(h * Q_PER_KV + g), cq, s1q, s2q).astype(BF16)
                                       for g in range(Q_PER_KV)], axis=0),
            lambda h: [kr_ref[pl.ds(start, n_win), head_slice(h)], kv_ref[0, :n_ctx, head_slice(h)]],
            lambda h: [vx_ref[pl.ds(n_ctx + start, n_win), h * vxw:(h + 1) * vxw],
                       vx_ref[:n_ctx, h * vxw:(h + 1) * vxw]], True)


def _attention(pz, sink, n_ctx, e, ctx_out):
    bsz, t, _ = pz.shape
    n_lat = t - n_ctx
    n_q = e // HEAD_DIM
    n_kv = n_q // Q_PER_KV
    kvw = n_kv * HEAD_DIM
    n_win = ATTN_BLOCK + 2 * WINDOW
    rows_q = Q_PER_KV * ATTN_BLOCK
    assert (2 * e) % (2 * kvw) == 0 and n_lat >= n_win
    cos, s1, s2 = _rope_tables(n_lat)
    nb_ctx = n_ctx // ATTN_BLOCK
    first = 0 if ctx_out else nb_ctx
    qtab = pl.BlockSpec((ATTN_BLOCK, HEAD_DIM), lambda b, j: (jnp.maximum(j + first - nb_ctx, 0), 0))
    ktab = pl.BlockSpec((n_lat, HEAD_DIM), lambda b, j: (0, 0))
    return pl.pallas_call(
        functools.partial(_attn_kernel, n_ctx=n_ctx, n_lat=n_lat, n_kv=n_kv, first_block=first),
        grid=(bsz, t // ATTN_BLOCK - first),
        in_specs=[pl.BlockSpec(memory_space=pltpu.SMEM),
                  pl.BlockSpec((1, ATTN_BLOCK, e), lambda b, j: (b, j + first, 1)),
                  pl.BlockSpec((1, t, 2 * kvw), lambda b, j: (b, 0, 2 * e // (2 * kvw))),
                  qtab, qtab, qtab, ktab, ktab, ktab],
        out_specs=pl.BlockSpec((1, ATTN_BLOCK, e), lambda b, j: (b, j, 0)),
        out_shape=jax.ShapeDtypeStruct((bsz, t - first * ATTN_BLOCK, e), BF16),
        scratch_shapes=[pltpu.VMEM((n_lat, kvw), BF16),
                        pltpu.VMEM((t, n_kv * 2 * HEAD_DIM), BF16),
                        pltpu.VMEM((2, rows_q, n_win + n_ctx), F32),
                        pltpu.VMEM((2, rows_q, n_win + n_ctx), BF16),
                        pltpu.VMEM((2, rows_q, 1), F32),
                        pltpu.VMEM((ATTN_BLOCK, n_win), F32)],
        compiler_params=_cparams(2),
    )(sink.astype(F32), pz, pz, cos, s1, s2, cos, s1, s2)


def _cg_order(w, axis):
    axis = axis % w.ndim
    n = w.shape[axis]
    shp = w.shape[:axis] + (n // S5_GROUP, S5_GROUP) + w.shape[axis + 1:]
    return jnp.swapaxes(w.reshape(shp), axis, axis + 1).reshape(w.shape)


def _s5_operators(lam_re, lam_im, log_step, b_re, b_im, c_re, c_im, d_skip):
    n_g = lam_re.shape[1]
    tc = S5_CHUNK
    dt = jnp.exp(log_step.astype(F32))[..., None]
    lr, li = lam_re.astype(F32) * dt, lam_im.astype(F32) * dt
    k = jnp.arange(tc + 1, dtype=F32)[:, None, None, None]
    mag = jnp.exp(lr[None] * k)
    pr, pi = mag * jnp.cos(li[None] * k), mag * jnp.sin(li[None] * k)
    ar1, ai1 = pr[1] - 1.0, pi[1]
    den = lam_re.astype(F32) ** 2 + lam_im.astype(F32) ** 2
    cr = (ar1 * lam_re + ai1 * lam_im) / den
    ci = (ai1 * lam_re - ar1 * lam_im) / den
    br, bi = b_re.astype(F32), b_im.astype(F32)
    bbr = cr[..., None] * br - ci[..., None] * bi
    bbi = cr[..., None] * bi + ci[..., None] * br
    ccr, cci = c_re.astype(F32), c_im.astype(F32)

    def kern(d):
        xr = pr[:tc, d][..., None] * bbr[d][None] - pi[:tc, d][..., None] * bbi[d][None]
        xi = pr[:tc, d][..., None] * bbi[d][None] + pi[:tc, d][..., None] * bbr[d][None]
        return (jnp.einsum('gnp,tgpm->tgnm', ccr[d], xr) - jnp.einsum('gnp,tgpm->tgnm', cci[d], xi))

    kf, kb = kern(0), kern(1)
    s_idx = jnp.arange(tc)[:, None]
    t_idx = jnp.arange(tc)[None, :]
    lag = t_idx - s_idx
    mf = jnp.where((lag >= 0)[:, :, None, None, None], kf[jnp.clip(lag, 0, tc - 1)], 0.0)
    mb = jnp.where((lag <= 0)[:, :, None, None, None], kb[jnp.clip(-lag, 0, tc - 1)], 0.0)
    m = (mf + mb).transpose(2, 4, 0, 3, 1)
    eye = (jnp.eye(S5_GROUP, dtype=F32)[:, None, :, None] * jnp.eye(tc, dtype=F32)[None, :, None, :])
    m = m + eye[None] * d_skip.astype(F32).reshape(n_g, S5_GROUP, 1, 1, 1)
    m = m.reshape(n_g, tc * S5_GROUP, tc * S5_GROUP)

    def qpart(d, pw_r, pw_i):
        qr = pw_r[..., None] * bbr[d][None] - pw_i[..., None] * bbi[d][None]
        qi = pw_r[..., None] * bbi[d][None] + pw_i[..., None] * bbr[d][None]
        return qr.transpose(1, 3, 0, 2), qi.transpose(1, 3, 0, 2)

    qfr, qfi = qpart(0, pr[tc - 1 - jnp.arange(tc), 0], pi[tc - 1 - jnp.arange(tc), 0])
    qbr, qbi = qpart(1, pr[jnp.arange(tc), 1], pi[jnp.arange(tc), 1])
    q = jnp.concatenate([qfr, qbr, qfi, qbi], axis=-1).reshape(n_g, tc * S5_GROUP, 4 * S5_STATE)

    def ppart(d, pw_r, pw_i):
        xr = ccr[d][None] * pw_r[:, :, None, :] - cci[d][None] * pw_i[:, :, None, :]
        xi = ccr[d][None] * pw_i[:, :, None, :] + cci[d][None] * pw_r[:, :, None, :]
        return xr.transpose(1, 3, 2, 0), (-xi).transpose(1, 3, 2, 0)

    pfr, pfi = ppart(0, pr[1 + jnp.arange(tc), 0], pi[1 + jnp.arange(tc), 0])
    pbr, pbi = ppart(1, pr[tc - jnp.arange(tc), 1], pi[tc - jnp.arange(tc), 1])
    p = jnp.concatenate([pfr, pbr, pfi, pbi], axis=1).reshape(n_g, 4 * S5_STATE, tc * S5_GROUP)
    a = jnp.stack([jnp.concatenate([pr[tc, 0], pr[tc, 1]], axis=-1),
                   jnp.concatenate([pi[tc, 0], pi[tc, 1]], axis=-1)], axis=1)
    a = jnp.pad(a, ((0, 0), (0, 6), (0, 0)))
    return m.astype(BF16), q.astype(BF16), p.astype(BF16), a


def _s5_pack_kernel(x_ref, o_ref, f_ref, *, nb, n_g):
    half = S5_GROUP // 2
    for b in range(nb):
        xb = x_ref[b].astype(F32)
        for k in range(2):
            a = jnp.concatenate([xb[:, cc * n_g:(cc + 1) * n_g]
                                 for cc in range(k * half, (k + 1) * half)], axis=0)
            f_ref[k, pl.ds(b, n_g, stride=nb), :] = a.T
    full = jnp.concatenate([f_ref[0], f_ref[1]], axis=1)
    o_ref[...] = full.reshape(n_g, nb, S5_CHUNK * S5_GROUP).astype(o_ref.dtype)


def _s5_pack(pz, e):
    bsz, t, _ = pz.shape
    n_g = e // S5_GROUP
    n_chunks = t // S5_CHUNK
    w = S5_CHUNK * S5_GROUP
    return pl.pallas_call(
        functools.partial(_s5_pack_kernel, nb=bsz, n_g=n_g),
        grid=(n_chunks,),
        in_specs=[pl.BlockSpec((bsz, S5_CHUNK, e), lambda i: (0, i, 1))],
        out_specs=pl.BlockSpec((n_g, bsz, w), lambda i: (0, i, 0)),
        out_shape=jax.ShapeDtypeStruct((n_g, n_chunks * bsz, w), BF16),
        scratch_shapes=[pltpu.VMEM((2, n_g * bsz, w // 2), F32)],
        compiler_params=_cparams(1),
    )(pz)


def _s5_kernel(u_ref, m_ref, q_ref, p_ref, a_ref, y_ref, s_ref, h_ref, *, nb, n_chunks, nc_ctx):
    ns = S5_STATE
    u = u_ref[0]
    s_ref[...] = jnp.dot(u, q_ref[0], preferred_element_type=F32)
    ar = a_ref[0, 0:1, :]
    ai = a_ref[0, 1:2, :]
    is_fwd = lax.broadcasted_iota(jnp.int32, (nb, 2 * ns), 1) < ns

    def step(i, carry):
        hr, hi = carry
        cb = jnp.where(i < nc_ctx, nc_ctx - 1 - i, n_chunks - 1 - (i - nc_ctx))
        rf = pl.multiple_of(i * nb, nb)
        rb = pl.multiple_of(cb * nb, nb)
        h_ref[pl.ds(rf, nb), 0:ns] = hr[:, 0:ns]
        h_ref[pl.ds(rb, nb), ns:2 * ns] = hr[:, ns:2 * ns]
        h_ref[pl.ds(rf, nb), 2 * ns:3 * ns] = hi[:, 0:ns]
        h_ref[pl.ds(rb, nb), 3 * ns:4 * ns] = hi[:, ns:2 * ns]
        sf = s_ref[pl.ds(rf, nb), :]
        sb = s_ref[pl.ds(rb, nb), :]
        sr = jnp.where(is_fwd, sf[:, :2 * ns], sb[:, :2 * ns])
        si = jnp.where(is_fwd, sf[:, 2 * ns:], sb[:, 2 * ns:])
        return ar * hr - ai * hi + sr, ar * hi + ai * hr + si

    zero = jnp.zeros((nb, 2 * ns), F32)
    lax.fori_loop(0, n_chunks, step, (zero, zero), unroll=2)
    y = jnp.dot(u, m_ref[0], preferred_element_type=F32)
    y = y + jnp.dot(h_ref[...].astype(BF16), p_ref[0], preferred_element_type=F32)
    y_ref[0] = y.astype(y_ref.dtype)


def _s5_scan(uf, m, q, p, a, nb, nc_ctx):
    n_g, rows, w = uf.shape
    n_chunks = rows // nb
    wspec = pl.BlockSpec((1, w, w), lambda g: (g, 0, 0))
    return pl.pallas_call(
        functools.partial(_s5_kernel, nb=nb, n_chunks=n_chunks, nc_ctx=nc_ctx),
        grid=(n_g,),
        in_specs=[pl.BlockSpec((1, rows, w), lambda g: (g, 0, 0)), wspec, wspec, wspec,
                  pl.BlockSpec((1, 8, 2 * S5_STATE), lambda g: (g, 0, 0))],
        out_specs=pl.BlockSpec((1, rows, w), lambda g: (g, 0, 0)),
        out_shape=jax.ShapeDtypeStruct((n_g, rows, w), BF16),
        scratch_shapes=[pltpu.VMEM((rows, w), F32), pltpu.VMEM((rows, w), F32)],
        compiler_params=_cparams(1),
    )(uf, m, q, p, a)


def _s5_glu_kernel(y0_ref, yn_ref, w_ref, b_ref, o_ref, f_ref, g_ref, *, nb, n_g):
    i = pl.program_id(0)
    w = S5_CHUNK * S5_GROUP
    half = S5_GROUP // 2
    e = n_g * S5_GROUP

    def widen(y_ref):
        yv = y_ref[...].astype(F32).reshape(n_g * nb, w)
        for k in range(2):
            f_ref[k] = yv[:, k * (w // 2):(k + 1) * (w // 2)]

    def build(slot, batches):
        for b in batches:
            for k in range(2):
                t = f_ref[k, pl.ds(b, n_g, stride=nb), :].T
                g = 0.5 * t * (1.0 + jnp.tanh(math.sqrt(2.0 / math.pi) * (t + 0.044715 * t * t * t)))
                g = g.astype(BF16)
                for cc in range(half):
                    ch = k * half + cc
                    g_ref[slot, b * S5_CHUNK:(b + 1) * S5_CHUNK, ch * n_g:(ch + 1) * n_g] = (
                        g[cc * S5_CHUNK:(cc + 1) * S5_CHUNK, :])

    @pl.when(i == 0)
    def _():
        widen(y0_ref)
        build(0, range(nb))

    cur = i % 2
    nxt = 1 - cur
    n_pieces = GLU_PIECES
    tn = e // n_pieces
    per = nb // n_pieces
    widen(yn_ref)
    for p in range(n_pieces):
        cols = slice(p * tn, (p + 1) * tn)
        g = g_ref[cur]
        lin = jnp.dot(g, w_ref[:, cols], preferred_element_type=F32) + b_ref[:, cols]
        out = (g[:, cols].astype(F32) * _sigmoid(lin)).astype(o_ref.dtype)
        o_ref[:, :, cols] = out.reshape(nb, S5_CHUNK, tn)
        build(nxt, range(p * per, (p + 1) * per))


def _s5_glu(yf, w_bf, bias, bsz):
    n_g, rows, w = yf.shape
    e = n_g * S5_GROUP
    n_chunks = rows // bsz
    assert bsz % GLU_PIECES == 0 and e % GLU_PIECES == 0
    return pl.pallas_call(
        functools.partial(_s5_glu_kernel, nb=bsz, n_g=n_g),
        grid=(n_chunks,),
        in_specs=[pl.BlockSpec((n_g, bsz, w), lambda i: (0, 0, 0)),
                  pl.BlockSpec((n_g, bsz, w), lambda i: (0, jnp.minimum(i + 1, n_chunks - 1), 0)),
                  pl.BlockSpec((e, e), lambda i: (0, 0)),
                  pl.BlockSpec((1, e), lambda i: (0, 0))],
        out_specs=pl.BlockSpec((bsz, S5_CHUNK, e), lambda i: (0, i, 0)),
        out_shape=jax.ShapeDtypeStruct((bsz, n_chunks * S5_CHUNK, e), BF16),
        scratch_shapes=[pltpu.VMEM((2, n_g * bsz, w // 2), F32),
                        pltpu.VMEM((2, bsz * S5_CHUNK, e), BF16)],
        compiler_params=_cparams(1),
    )(yf, yf, w_bf, bias.astype(F32).reshape(1, e))


def _s5_mixer(pz, n_ctx, e, ops, glu_w_bf, glu_b):
    bsz = pz.shape[0]
    m, q, p, a = ops
    uf = _s5_pack(pz, e)
    yf = _s5_scan(uf, m, q, p, a, bsz, n_ctx // S5_CHUNK)
    return _s5_glu(yf, glu_w_bf, glu_b, bsz)


def _cumsum_rows_multi(xs, reverse):
    n = xs[0].shape[0]
    sub = 8
    n_slabs = n // sub
    row = lax.broadcasted_iota(jnp.int32, (sub, xs[0].shape[1]), 0)
    slabs = [[x[i * sub:(i + 1) * sub] for i in range(n_slabs)] for x in xs]
    s = 1
    while s < sub:
        nxt = []
        for arr, rev in zip(slabs, reverse):
            if rev:
                nxt.append([sl + jnp.where(row < sub - s, pltpu.roll(sl, sub - s, 0), 0.0) for sl in arr])
            else:
                nxt.append([sl + jnp.where(row >= s, pltpu.roll(sl, s, 0), 0.0) for sl in arr])
        slabs = nxt
        s *= 2
    out = []
    for arr, rev in zip(slabs, reverse):
        order = range(n_slabs - 1, -1, -1) if rev else range(n_slabs)
        edge = 0 if rev else sub - 1
        done = [None] * n_slabs
        carry = None
        for i in order:
            sl = arr[i] if carry is None else arr[i] + carry
            done[i] = sl
            carry = jnp.broadcast_to(sl[edge:edge + 1], sl.shape)
        out.append(jnp.concatenate(done, axis=0))
    return out


def _hgrn2_kernel(x_ref, lb_ref, ng_ref, o_ref,
                  g_ref, k_ref, qe_ref, ke_ref, qd_ref, kd_ref, ds_ref, dec_ref, st_ref, oi_ref,
                  *, layer, n_chunks, nc_ctx, n_tok):
    c = HG_CHUNK
    hd = HG_HEAD
    nt = (((1,), (1,)), ((), ()))
    tn = (((0,), (0,)), ((), ()))

    lbw = lb_ref[...].astype(F32)
    ew = jnp.exp(lbw - jnp.max(lbw, axis=0, keepdims=True))
    lb = jnp.sum(ew[1:layer + 1], axis=0, keepdims=True) / jnp.sum(ew, axis=0, keepdims=True)

    def gates(ci):
        r0 = pl.multiple_of(ci * c, c)
        for d in range(2):
            fl = x_ref[0, pl.ds(r0, c), (1 + d) * hd:(2 + d) * hd].astype(F32)
            f = lb + (1.0 - lb) * _sigmoid(fl)
            g_ref[d, pl.ds(r0, c), :] = jnp.log2(f)
            k_ref[d, pl.ds(r0, c), :] = 1.0 - f

    def cumdecay(cis):
        r0s = [pl.multiple_of(ci * c, c) for ci in cis]
        xs = [g_ref[d, pl.ds(r0, c), :] for r0 in r0s for d in range(2)]
        xs = _cumsum_rows_multi(xs, [False, True] * len(cis))
        k = 0
        for r0 in r0s:
            for d in range(2):
                g_ref[d, pl.ds(r0, c), :] = xs[k]
                k += 1

    def decayed(ci):
        r0 = pl.multiple_of(ci * c, c)
        q = x_ref[0, pl.ds(r0, c), 0:hd].astype(F32)
        for d in range(2):
            mid = c // 2 if d == 0 else c // 2 - 1
            last = c - 1 if d == 0 else 0
            b = g_ref[d, pl.ds(r0, c), :]
            ref = g_ref[d, pl.ds(r0 + mid, 1), :]
            b_last = g_ref[d, pl.ds(r0 + last, 1), :]
            qe = q * jnp.exp2(b - ref)
            ke = k_ref[d, pl.ds(r0, c), :] * jnp.exp2(ref - b)
            qe_ref[d, pl.ds(r0, c), :] = qe.astype(BF16)
            ke_ref[d, pl.ds(r0, c), :] = ke.astype(BF16)
            qd_ref[pl.ds(r0, c), d * hd:(d + 1) * hd] = (qe * jnp.exp2(ref)).astype(BF16)
            kd_ref[pl.ds(r0, c), d * hd:(d + 1) * hd] = (ke * jnp.exp2(b_last - ref)).astype(BF16)
            dec_ref[ci, :, d * hd:(d + 1) * hd] = jnp.exp2(b_last)

    def prepare(cis):
        for ci in cis:
            gates(ci)
        cumdecay(cis)
        for ci in cis:
            decayed(ci)

    ti = lax.broadcasted_iota(jnp.int32, (c, 2 * c), 0)
    si = lax.broadcasted_iota(jnp.int32, (c, 2 * c), 1)
    m_fwd = si <= ti
    m_bwd = si - c >= ti

    def logits(cis):
        r0s = [pl.multiple_of(ci * c, c) for ci in cis]
        vs = [x_ref[0, pl.ds(r0, c), 3 * hd:4 * hd] for r0 in r0s]
        aa = [lax.dot_general(
            jnp.concatenate([qe_ref[0, pl.ds(r0, c), :], qe_ref[1, pl.ds(r0, c), :]], axis=0),
            jnp.concatenate([ke_ref[0, pl.ds(r0, c), :], ke_ref[1, pl.ds(r0, c), :]], axis=0),
            nt, preferred_element_type=F32) for r0 in r0s]
        dss = [lax.dot_general(v, kd_ref[pl.ds(r0, c), :], tn, preferred_element_type=F32)
               for v, r0 in zip(vs, r0s)]
        return r0s, vs, aa, dss

    def values(cis, vs, aa, dss):
        atts = [jnp.where(m_fwd, a[:c], jnp.where(m_bwd, a[c:], 0.0)).astype(BF16) for a in aa]
        ois = [jnp.dot(att, jnp.concatenate([v, v], axis=0), preferred_element_type=F32)
               for att, v in zip(atts, vs)]
        for ci, ds in zip(cis, dss):
            ds_ref[ci] = ds
        return tuple(ois)

    def store_intra(cis, ois):
        for ci, oi in zip(cis, ois):
            oi_ref[pl.ds(pl.multiple_of(ci * c, c), c), :] = oi

    n_pairs = n_chunks // HG_PAIR

    def pair(t):
        return [t * HG_PAIR + u for u in range(HG_PAIR)]

    def stage(t, pending, with_next):
        cur = pair(t)
        _, vs, aa, dss = logits(cur)
        if pending is not None:
            store_intra(pair(t - 1), pending)
        if with_next:
            nxt = pair(t + 1)
            for ci in nxt:
                gates(ci)
            cumdecay(nxt)
        ois = values(cur, vs, aa, dss)
        if with_next:
            for ci in nxt:
                decayed(ci)
        return ois

    prepare(pair(0))
    pending = stage(0, None, True)
    for t in range(1, n_pairs - 1):
        pending = stage(t, pending, True)
    pending = stage(n_pairs - 1, pending, False)
    store_intra(pair(n_pairs - 1), pending)

    def scan_b(i, carry):
        sf, sb = carry
        cb = jnp.where(i < nc_ctx, nc_ctx - 1 - i, n_chunks - 1 - (i - nc_ctx))
        st_ref[i, :, 0:hd] = sf.astype(BF16)
        st_ref[cb, :, hd:2 * hd] = sb.astype(BF16)
        sf = sf * dec_ref[i, :, 0:hd] + ds_ref[i, :, 0:hd]
        sb = sb * dec_ref[cb, :, hd:2 * hd] + ds_ref[cb, :, hd:2 * hd]
        return sf, sb

    zero = jnp.zeros((hd, hd), F32)
    lax.fori_loop(0, n_chunks, scan_b, (zero, zero))

    def inter(cis):
        os_ = [lax.dot_general(qd_ref[ci * c:(ci + 1) * c, :], st_ref[ci], nt, preferred_element_type=F32)
               for ci in cis]
        for ci, o in zip(cis, os_):
            oi_ref[ci * c:(ci + 1) * c, :] += o

    gn = ng_ref[...].astype(F32)

    def head_norm(r0):
        o = oi_ref[r0:r0 + HG_NORM_ROWS, :]
        o = o * lax.rsqrt(jnp.mean(o * o, axis=-1, keepdims=True) + NORM_EPS) * gn
        o_ref[0, r0:r0 + HG_NORM_ROWS, :] = o.astype(o_ref.dtype)

    group_rows = HG_UNROLL_INTER * c
    n_groups = n_chunks // HG_UNROLL_INTER
    for gi in range(n_groups + 1):
        if gi < n_groups:
            inter(range(gi * HG_UNROLL_INTER, (gi + 1) * HG_UNROLL_INTER))
        if gi >= 1:
            for r0 in range((gi - 1) * group_rows, gi * group_rows, HG_NORM_ROWS):
                head_norm(r0)


def _hgrn2(pz, hg_lb, norm_g, layer, n_ctx, e):
    bsz, t, _ = pz.shape
    n_h = e // HG_HEAD
    n_chunks = t // HG_CHUNK
    depth = hg_lb.shape[0]
    hd = HG_HEAD
    assert n_chunks % HG_PAIR == 0 and n_chunks // HG_PAIR >= 3 and n_chunks % HG_UNROLL_INTER == 0
    assert t % HG_NORM_ROWS == 0

    return pl.pallas_call(
        functools.partial(_hgrn2_kernel, layer=layer, n_chunks=n_chunks, nc_ctx=n_ctx // HG_CHUNK, n_tok=t),
        grid=(bsz, n_h),
        in_specs=[pl.BlockSpec((1, t, 4 * hd), lambda b, h: (b, 0, e // (4 * hd) + h)),
                  pl.BlockSpec((depth, hd), lambda b, h: (0, h)),
                  pl.BlockSpec((1, hd), lambda b, h: (0, h))],
        out_specs=pl.BlockSpec((1, t, hd), lambda b, h: (b, 0, h)),
        out_shape=jax.ShapeDtypeStruct((bsz, t, e), BF16),
        scratch_shapes=[pltpu.VMEM((2, t, hd), F32),
                        pltpu.VMEM((2, t, hd), F32),
                        pltpu.VMEM((2, t, hd), BF16),
                        pltpu.VMEM((2, t, hd), BF16),
                        pltpu.VMEM((t, 2 * hd), BF16),
                        pltpu.VMEM((t, 2 * hd), BF16),
                        pltpu.VMEM((n_chunks, hd, 2 * hd), F32),
                        pltpu.VMEM((n_chunks, 1, 2 * hd), F32),
                        pltpu.VMEM((n_chunks, hd, 2 * hd), BF16),
                        pltpu.VMEM((t, hd), F32)],
        compiler_params=_cparams(2),
    )(pz, hg_lb.astype(F32), norm_g.astype(F32).reshape(1, e))


def _head_major(w, e):
    d = w.shape[0]
    n_h = e // HG_HEAD
    rest = w[:, e:].reshape(d, 4, n_h, HG_HEAD).transpose(0, 2, 1, 3).reshape(d, 4 * e)
    return jnp.concatenate([w[:, :e], rest], axis=1)


def _z_first(w, e):
    return jnp.concatenate([w[..., -e:], w[..., :-e]], axis=-1)


def kernel(x, c, ctx, c_ctx, ada_w, ada_b, ln_g, ln_b, w_out, attn_w_in, attn_sink, s5_w_in, s5_lam_re, s5_lam_im, s5_log_step, s5_b_re, s5_b_im, s5_c_re, s5_c_im, s5_d, s5_glu_w, s5_glu_b, hg_w_in, hg_lb, hg_norm_g):
    bsz, n_lat, d = x.shape
    n_ctx = ctx.shape[1]
    depth = ada_w.shape[0]
    e = w_out.shape[1]
    alpha = (2.0 * depth) ** 0.25
    t = n_ctx + n_lat
    tm = 768 if t % 768 == 0 else 256

    mod_rows = -(-(bsz + 1) // MOD_ROWS_PAD) * MOD_ROWS_PAD
    c_all = jnp.concatenate([c.astype(F32), c_ctx.astype(F32)[None],
                             jnp.zeros((mod_rows - bsz - 1, d), F32)], axis=0)
    mod = _ada(c_all, ada_w.astype(F32), ada_b.astype(F32))
    mod3 = mod.reshape(depth * mod_rows, 1, 3 * d)

    xa = jnp.concatenate([ctx.astype(F32), x.astype(F32)], axis=1)
    for i in range(depth):
        kind, j = i % N_MIXERS, i // N_MIXERS
        w_in = _z_first((attn_w_in, s5_w_in, hg_w_in)[kind][j], e)
        w_o = w_out[i]
        if kind == 1:
            w_in = jnp.concatenate([_cg_order(w_in[:, :e], 1), _cg_order(w_in[:, e:], 1)], axis=1)
            w_o = _cg_order(w_o, 0)
        elif kind == 2:
            w_in = _head_major(w_in, e)
        pz = _inproj(xa, mod3, i, mod_rows, w_in.astype(BF16), n_ctx, tn=INPROJ_TN)
        last = i == depth - 1
        if kind == 0:
            y = _attention(pz, attn_sink[j], n_ctx, e, ctx_out=not last)
        elif kind == 1:
            ops = _s5_operators(s5_lam_re[j], s5_lam_im[j], s5_log_step[j], s5_b_re[j], s5_b_im[j],
                                s5_c_re[j], s5_c_im[j], s5_d[j])
            glu_w = _cg_order(_cg_order(s5_glu_w[j], 0), 1).astype(BF16)
            y = _s5_mixer(pz, n_ctx, e, ops, glu_w, _cg_order(s5_glu_b[j], 0))
        else:
            y = _hgrn2(pz, hg_lb, hg_norm_g[j], i, n_ctx, e)
        if last and kind == 0:
            return _outproj(y, pz, xa, mod3, i, mod_rows, w_o.astype(BF16), ln_g[i].astype(F32),
                            ln_b[i].astype(F32), n_ctx, alpha, LAST_TM, latent_only=True).astype(x.dtype)
        xa = _outproj(y, pz, xa, mod3, i, mod_rows, w_o.astype(BF16), ln_g[i].astype(F32),
                      ln_b[i].astype(F32), n_ctx, alpha, tm)
    return xa[:, n_ctx:].astype(x.dtype)
```

```python
import functools
import math

import jax
import jax.numpy as jnp
from jax import lax
from jax.experimental import pallas as pl
from jax.experimental.pallas import tpu as pltpu

F32 = jnp.float32
BF16 = jnp.bfloat16

N_MIXERS = 3
HEAD_DIM = 128
Q_PER_KV = 4
WINDOW = 128
ATTN_BLOCK = 128
GRID_W = 64
ROPE_THETA = 10000.0
S5_GROUP = 16
S5_STATE = 64
S5_CHUNK = 16
HG_HEAD = 128
HG_CHUNK = 64
HG_UNROLL_INTER = 12
HG_PAIR = 2
HG_NORM_ROWS = 256
NORM_EPS = 1e-5
NEG_INF = -1e30
LOG2E = math.log2(math.e)
ATTN_ROWS = 32
MOD_ROWS_PAD = 8
INPROJ_TN = 1024
LAST_TM = 256
GLU_PIECES = 8
VMEM_LIMIT = 56 * 1024 * 1024


def _cparams(n_axes):
    return pltpu.CompilerParams(dimension_semantics=("arbitrary",) * n_axes,
                                vmem_limit_bytes=VMEM_LIMIT)


def _sigmoid(x):
    return 1.0 / (1.0 + jnp.exp(-x))


def _ada_kernel(c_ref, w_ref, b_ref, o_ref):
    cv = c_ref[...]
    s = cv * _sigmoid(cv)
    o_ref[0] = jnp.dot(s, w_ref[0], preferred_element_type=F32,
                       precision=lax.Precision.HIGHEST) + b_ref[0]


def _ada(c_all, ada_w, ada_b):
    depth, d, d3 = ada_w.shape
    rows = c_all.shape[0]
    return pl.pallas_call(
        _ada_kernel,
        grid=(depth, d3 // d),
        in_specs=[pl.BlockSpec((rows, d), lambda i, j: (0, 0)),
                  pl.BlockSpec((1, d, d), lambda i, j: (i, 0, j)),
                  pl.BlockSpec((1, 1, d), lambda i, j: (i, 0, j))],
        out_specs=pl.BlockSpec((1, rows, d), lambda i, j: (i, 0, j)),
        out_shape=jax.ShapeDtypeStruct((depth, rows, d3), F32),
        compiler_params=_cparams(2),
    )(c_all, ada_w, ada_b.reshape(depth, 1, d3))


def _inproj_kernel(*refs, n_ctx, d, split):
    if split:
        xc_ref, xl_ref, mb_ref, mc_ref, w_ref, o_ref, h_ref = refs
    else:
        x_ref, mb_ref, mc_ref, w_ref, o_ref, h_ref = refs

    @pl.when(pl.program_id(1) == 0)
    def _():
        mb = mb_ref[0]
        mc = mc_ref[0]
        xc = xc_ref[0] if split else x_ref[0, :n_ctx]
        xl = xl_ref[0] if split else x_ref[0, n_ctx:]
        h_ref[:n_ctx] = (xc * (1.0 + mc[:, d:2 * d]) + mc[:, :d]).astype(BF16)
        h_ref[n_ctx:] = (xl * (1.0 + mb[:, d:2 * d]) + mb[:, :d]).astype(BF16)

    o_ref[0] = jnp.dot(h_ref[...], w_ref[...], preferred_element_type=F32).astype(o_ref.dtype)


def _inproj(xs, mod3, layer, mod_rows, w_bf, n_ctx, tn):
    split = len(xs) == 2
    bsz, _, d = xs[0].shape
    t = sum(a.shape[1] for a in xs)
    n = w_bf.shape[1]
    base = layer * mod_rows
    return pl.pallas_call(
        functools.partial(_inproj_kernel, n_ctx=n_ctx, d=d, split=split),
        grid=(bsz, n // tn),
        in_specs=[pl.BlockSpec((1, a.shape[1], d), lambda b, j: (b, 0, 0)) for a in xs] + [
                  pl.BlockSpec((1, 1, 3 * d), lambda b, j: (base + b, 0, 0)),
                  pl.BlockSpec((1, 1, 3 * d), lambda b, j: (base + bsz, 0, 0)),
                  pl.BlockSpec((d, tn), lambda b, j: (0, j))],
        out_specs=pl.BlockSpec((1, t, tn), lambda b, j: (b, 0, j)),
        out_shape=jax.ShapeDtypeStruct((bsz, t, n), BF16),
        scratch_shapes=[pltpu.VMEM((t, d), BF16)],
        compiler_params=_cparams(2),
    )(*xs, mod3, mod3, w_bf)


def _outproj_kernel(y_ref, z_ref, *refs, n_ctx, d, alpha, tm, first_tile, n_pieces):
    if n_pieces:
        x_refs, (c_ref, mb_ref, mc_ref, w_ref, g_ref, b_ref, o_ref) = refs[:n_pieces], refs[n_pieces:]
        first = jnp.where(pl.program_id(1) == 0, c_ref[0], x_refs[0][0])
        x_tile = jnp.concatenate([first] + [r[0] for r in x_refs[1:]], axis=0)
    else:
        x_ref, mb_ref, mc_ref, w_ref, g_ref, b_ref, o_ref = refs
        x_tile = x_ref[0]
    z = z_ref[0]
    a = y_ref[0] * (z * _sigmoid(z))
    br = jnp.dot(a, w_ref[...], preferred_element_type=F32)
    rows = (pl.program_id(1) + first_tile) * tm + lax.broadcasted_iota(jnp.int32, (tm, 1), 0)
    gate = jnp.where(rows < n_ctx, mc_ref[0][:, 2 * d:], mb_ref[0][:, 2 * d:])
    v = alpha * x_tile + gate * br
    mu = jnp.mean(v, axis=-1, keepdims=True)
    vc = v - mu
    var = jnp.mean(vc * vc, axis=-1, keepdims=True)
    o_ref[0] = vc * lax.rsqrt(var + NORM_EPS) * g_ref[...] + b_ref[...]


def _outproj(y, pz, xs, mod3, layer, mod_rows, w_bf, ln_g, ln_b, n_ctx, alpha, tm, latent_only=False):
    split = len(xs) == 2
    bsz, _, d = xs[0].shape
    t = sum(a.shape[1] for a in xs)
    e = w_bf.shape[0]
    base = layer * mod_rows
    off = n_ctx // tm if latent_only else 0
    assert n_ctx % tm == 0 or not latent_only
    n_pieces = tm // n_ctx if split else 0
    assert not split or (tm % n_ctx == 0 and not latent_only)
    if split:
        x_specs = [pl.BlockSpec((1, n_ctx, d), lambda b, i, p=p: (b, jnp.maximum(i * n_pieces + p - 1, 0), 0))
                   for p in range(n_pieces)] + [pl.BlockSpec((1, n_ctx, d), lambda b, i: (b, 0, 0))]
        x_args = [xs[1]] * n_pieces + [xs[0]]
    else:
        x_specs = [pl.BlockSpec((1, tm, d), lambda b, i: (b, i + off, 0))]
        x_args = [xs[0]]
    return pl.pallas_call(
        functools.partial(_outproj_kernel, n_ctx=n_ctx, d=d, alpha=alpha, tm=tm, first_tile=off,
                          n_pieces=n_pieces),
        grid=(bsz, t // tm - off),
        in_specs=[pl.BlockSpec((1, tm, e), lambda b, i: (b, i, 0)),
                  pl.BlockSpec((1, tm, e), lambda b, i: (b, i + off, 0))] + x_specs + [
                  pl.BlockSpec((1, 1, 3 * d), lambda b, i: (base + b, 0, 0)),
                  pl.BlockSpec((1, 1, 3 * d), lambda b, i: (base + bsz, 0, 0)),
                  pl.BlockSpec((e, d), lambda b, i: (0, 0)),
                  pl.BlockSpec((1, d), lambda b, i: (0, 0)),
                  pl.BlockSpec((1, d), lambda b, i: (0, 0))],
        out_specs=pl.BlockSpec((1, tm, d), lambda b, i: (b, i, 0)),
        out_shape=jax.ShapeDtypeStruct((bsz, t - off * tm, d), F32),
        compiler_params=_cparams(2),
    )(y, pz, *x_args, mod3, mod3, w_bf, ln_g.reshape(1, d), ln_b.reshape(1, d))


def _rope_tables(n_lat):
    pos = jnp.arange(n_lat, dtype=jnp.int32)
    row = (pos // GRID_W).astype(F32)
    col = (pos % GRID_W).astype(F32)
    nf = HEAD_DIM // 4
    inv_freq = jnp.power(ROPE_THETA, -jnp.arange(nf, dtype=F32) / nf)
    ang_r = row[:, None] * inv_freq[None, :]
    ang_c = col[:, None] * inv_freq[None, :]
    zeros = jnp.zeros_like(ang_r)
    cos = jnp.concatenate([jnp.cos(ang_r)] * 2 + [jnp.cos(ang_c)] * 2, axis=-1)
    s1 = jnp.concatenate([-jnp.sin(ang_r), zeros, -jnp.sin(ang_c), zeros], axis=-1)
    s2 = jnp.concatenate([zeros, jnp.sin(ang_r), zeros, jnp.sin(ang_c)], axis=-1)
    return cos, s1, s2


def _rope(x, cos, s1, s2):
    quarter = HEAD_DIM // 4
    return (x * cos + pltpu.roll(x, HEAD_DIM - quarter, 1) * s1
            + pltpu.roll(x, quarter, 1) * s2)


def _attn_kernel(sink_ref, q_ref, kv_ref, cq_ref, s1q_ref, s2q_ref, ck_ref, s1k_ref, s2k_ref,
                 o_ref, kr_ref, vx_ref, s_ref, p_ref, e_ref, bias_ref, *, n_ctx, n_lat, n_kv, first_block):
    j = pl.program_id(1) + first_block
    nb_ctx = n_ctx // ATTN_BLOCK
    kvw = n_kv * HEAD_DIM
    n_win = ATTN_BLOCK + 2 * WINDOW
    qscale = HEAD_DIM ** -0.5 * LOG2E
    rows_q = Q_PER_KV * ATTN_BLOCK
    vxw = 2 * HEAD_DIM
    nt = (((1,), (1,)), ((), ()))

    @pl.when(pl.program_id(1) == 0)
    def _():
        for h in range(n_kv):
            sl = slice(h * HEAD_DIM, (h + 1) * HEAD_DIM)
            k = kv_ref[0, n_ctx:, sl].astype(F32)
            kr_ref[:, sl] = _rope(k, ck_ref[...], s1k_ref[...], s2k_ref[...]).astype(BF16)
            vx_ref[:, h * vxw:h * vxw + HEAD_DIM] = kv_ref[0, :, kvw + h * HEAD_DIM:kvw + (h + 1) * HEAD_DIM]
            vx_ref[:, h * vxw + HEAD_DIM:(h + 1) * vxw] = jnp.ones((n_ctx + n_lat, HEAD_DIM), BF16)

    def key_cols(kparts):
        cols, off = [], 0
        for kp in kparts:
            cols.append((off, kp.shape[0]))
            off += kp.shape[0]
        return cols

    def logit_tasks(h, q_fn, kparts):
        box = {}

        def task(kp, o_, n):
            def run():
                if "q" not in box:
                    box["q"] = q_fn(h)
                s_ref[h % 2, :, o_:o_ + n] = lax.dot_general(box["q"], kp, nt, preferred_element_type=F32)
            return run

        return [task(kp, o_, n) for kp, (o_, n) in zip(kparts, key_cols(kparts))]

    def softmax_rows(h, cols, use_bias, blocks):
        for rb in blocks:
            r0 = rb * ATTN_ROWS
            sk = sink_ref[h * Q_PER_KV + r0 // ATTN_BLOCK] * LOG2E
            parts = []
            for idx, (o_, n) in enumerate(cols):
                s = s_ref[h % 2, r0:r0 + ATTN_ROWS, o_:o_ + n]
                if use_bias and idx == 0:
                    ql0 = r0 % ATTN_BLOCK
                    s = s + bias_ref[ql0:ql0 + ATTN_ROWS, :]
                parts.append(s)
            m = jnp.max(parts[0], axis=-1, keepdims=True)
            for s in parts[1:]:
                m = jnp.maximum(m, jnp.max(s, axis=-1, keepdims=True))
            m = jnp.maximum(m, sk)
            for (o_, n), s in zip(cols, parts):
                p_ref[h % 2, r0:r0 + ATTN_ROWS, o_:o_ + n] = jnp.exp2(s - m).astype(BF16)
            e_ref[h % 2, r0:r0 + ATTN_ROWS, :] = jnp.exp2(sk - m)

    def value_tasks(h, cols, vparts):
        box = {}

        def task(idx, o_, n, vp):
            def run():
                part = jnp.dot(p_ref[h % 2, :, o_:o_ + n], vp, preferred_element_type=F32)
                box["ox"] = part if idx == 0 else box["ox"] + part
                if idx == len(cols) - 1:
                    ox = box["ox"]
                    l = ox[:, HEAD_DIM:HEAD_DIM + 1] + e_ref[h % 2]
                    o = ox[:, :HEAD_DIM] * (1.0 / l)
                    for g in range(Q_PER_KV):
                        hq = h * Q_PER_KV + g
                        o_ref[0, :, hq * HEAD_DIM:(hq + 1) * HEAD_DIM] = (
                            o[g * ATTN_BLOCK:(g + 1) * ATTN_BLOCK].astype(o_ref.dtype))
            return run

        return [task(i, o_, n, vp) for i, ((o_, n), vp) in enumerate(zip(cols, vparts))]

    def attend_all(q_fn, k_fn, v_fn, use_bias):
        n_blocks = rows_q // ATTN_ROWS
        quarter = n_blocks // 4
        cols = key_cols(k_fn(0))
        for task in logit_tasks(0, q_fn, k_fn(0)):
            task()
        for h in range(n_kv + 1):
            side = []
            if h + 1 < n_kv:
                side += logit_tasks(h + 1, q_fn, k_fn(h + 1))
            if 1 <= h:
                side += value_tasks(h - 1, cols, v_fn(h - 1))
            for qi in range(4):
                if h < n_kv:
                    softmax_rows(h, cols, use_bias, range(qi * quarter, (qi + 1) * quarter))
                for task in side[qi::4]:
                    task()

    def q_head(hq):
        return q_ref[0, :, hq * HEAD_DIM:(hq + 1) * HEAD_DIM].astype(F32)

    def head_slice(h):
        return slice(h * HEAD_DIM, (h + 1) * HEAD_DIM)

    @pl.when(j < nb_ctx)
    def _():
        attend_all(
            lambda h: jnp.concatenate([(q_head(h * Q_PER_KV + g) * qscale).astype(BF16)
                                       for g in range(Q_PER_KV)], axis=0),
            lambda h: [kv_ref[0, :n_ctx, head_slice(h)]],
            lambda h: [vx_ref[:n_ctx, h * vxw:(h + 1) * vxw]], False)

    @pl.when(j >= nb_ctx)
    def _():
        jl = j - nb_ctx
        start = jnp.clip((jl - 1) * ATTN_BLOCK, 0, n_lat - n_win)
        start = pl.multiple_of(start, ATTN_BLOCK)
        delta = jl * ATTN_BLOCK - start
        ql = lax.broadcasted_iota(jnp.int32, (ATTN_BLOCK, n_win), 0)
        kl = lax.broadcasted_iota(jnp.int32, (ATTN_BLOCK, n_win), 1)
        bias_ref[...] = jnp.where(jnp.abs(delta + ql - kl) <= WINDOW, 0.0, NEG_INF)
        cq = cq_ref[...] * qscale
        s1q = s1q_ref[...] * qscale
        s2q = s2q_ref[...] * qscale
        attend_all(
            lambda h: jnp.concatenate([_rope(q_head(h * Q_PER_KV + g), cq, s1q, s2q).astype(BF16)
                                       for g in range(Q_PER_KV)], axis=0),
            lambda h: [kr_ref[pl.ds(start, n_win), head_slice(h)], kv_ref[0, :n_ctx, head_slice(h)]],
            lambda h: [vx_ref[pl.ds(n_ctx + start, n_win), h * vxw:(h + 1) * vxw],
                       vx_ref[:n_ctx, h * vxw:(h + 1) * vxw]], True)


def _attention(pz, sink, n_ctx, e, ctx_out):
    bsz, t, _ = pz.shape
    n_lat = t - n_ctx
    n_q = e // HEAD_DIM
    n_kv = n_q // Q_PER_KV
    kvw = n_kv * HEAD_DIM
    n_win = ATTN_BLOCK + 2 * WINDOW
    rows_q = Q_PER_KV * ATTN_BLOCK
    assert (2 * e) % (2 * kvw) == 0 and n_lat >= n_win
    cos, s1, s2 = _rope_tables(n_lat)
    nb_ctx = n_ctx // ATTN_BLOCK
    first = 0 if ctx_out else nb_ctx
    qtab = pl.BlockSpec((ATTN_BLOCK, HEAD_DIM), lambda b, j: (jnp.maximum(j + first - nb_ctx, 0), 0))
    ktab = pl.BlockSpec((n_lat, HEAD_DIM), lambda b, j: (0, 0))
    return pl.pallas_call(
        functools.partial(_attn_kernel, n_ctx=n_ctx, n_lat=n_lat, n_kv=n_kv, first_block=first),
        grid=(bsz, t // ATTN_BLOCK - first),
        in_specs=[pl.BlockSpec(memory_space=pltpu.SMEM),
                  pl.BlockSpec((1, ATTN_BLOCK, e), lambda b, j: (b, j + first, 1)),
                  pl.BlockSpec((1, t, 2 * kvw), lambda b, j: (b, 0, 2 * e // (2 * kvw))),
                  qtab, qtab, qtab, ktab, ktab, ktab],
        out_specs=pl.BlockSpec((1, ATTN_BLOCK, e), lambda b, j: (b, j, 0)),
        out_shape=jax.ShapeDtypeStruct((bsz, t - first * ATTN_BLOCK, e), BF16),
        scratch_shapes=[pltpu.VMEM((n_lat, kvw), BF16),
                        pltpu.VMEM((t, n_kv * 2 * HEAD_DIM), BF16),
                        pltpu.VMEM((2, rows_q, n_win + n_ctx), F32),
                        pltpu.VMEM((2, rows_q, n_win + n_ctx), BF16),
                        pltpu.VMEM((2, rows_q, 1), F32),
                        pltpu.VMEM((ATTN_BLOCK, n_win), F32)],
        compiler_params=_cparams(2),
    )(sink.astype(F32), pz, pz, cos, s1, s2, cos, s1, s2)


def _cg_order(w, axis):
    axis = axis % w.ndim
    n = w.shape[axis]
    shp = w.shape[:axis] + (n // S5_GROUP, S5_GROUP) + w.shape[axis + 1:]
    return jnp.swapaxes(w.reshape(shp), axis, axis + 1).reshape(w.shape)


def _s5_operators(lam_re, lam_im, log_step, b_re, b_im, c_re, c_im, d_skip):
    n_g = lam_re.shape[1]
    tc = S5_CHUNK
    dt = jnp.exp(log_step.astype(F32))[..., None]
    lr, li = lam_re.astype(F32) * dt, lam_im.astype(F32) * dt
    k = jnp.arange(tc + 1, dtype=F32)[:, None, None, None]
    mag = jnp.exp(lr[None] * k)
    pr, pi = mag * jnp.cos(li[None] * k), mag * jnp.sin(li[None] * k)
    ar1, ai1 = pr[1] - 1.0, pi[1]
    den = lam_re.astype(F32) ** 2 + lam_im.astype(F32) ** 2
    cr = (ar1 * lam_re + ai1 * lam_im) / den
    ci = (ai1 * lam_re - ar1 * lam_im) / den
    br, bi = b_re.astype(F32), b_im.astype(F32)
    bbr = cr[..., None] * br - ci[..., None] * bi
    bbi = cr[..., None] * bi + ci[..., None] * br
    ccr, cci = c_re.astype(F32), c_im.astype(F32)

    def kern(d):
        xr = pr[:tc, d][..., None] * bbr[d][None] - pi[:tc, d][..., None] * bbi[d][None]
        xi = pr[:tc, d][..., None] * bbi[d][None] + pi[:tc, d][..., None] * bbr[d][None]
        return (jnp.einsum('gnp,tgpm->tgnm', ccr[d], xr) - jnp.einsum('gnp,tgpm->tgnm', cci[d], xi))

    kf, kb = kern(0), kern(1)
    s_idx = jnp.arange(tc)[:, None]
    t_idx = jnp.arange(tc)[None, :]
    lag = t_idx - s_idx
    mf = jnp.where((lag >= 0)[:, :, None, None, None], kf[jnp.clip(lag, 0, tc - 1)], 0.0)
    mb = jnp.where((lag <= 0)[:, :, None, None, None], kb[jnp.clip(-lag, 0, tc - 1)], 0.0)
    m = (mf + mb).transpose(2, 4, 0, 3, 1)
    eye = (jnp.eye(S5_GROUP, dtype=F32)[:, None, :, None] * jnp.eye(tc, dtype=F32)[None, :, None, :])
    m = m + eye[None] * d_skip.astype(F32).reshape(n_g, S5_GROUP, 1, 1, 1)
    m = m.reshape(n_g, tc * S5_GROUP, tc * S5_GROUP)

    def qpart(d, pw_r, pw_i):
        qr = pw_r[..., None] * bbr[d][None] - pw_i[..., None] * bbi[d][None]
        qi = pw_r[..., None] * bbi[d][None] + pw_i[..., None] * bbr[d][None]
        return qr.transpose(1, 3, 0, 2), qi.transpose(1, 3, 0, 2)

    qfr, qfi = qpart(0, pr[tc - 1 - jnp.arange(tc), 0], pi[tc - 1 - jnp.arange(tc), 0])
    qbr, qbi = qpart(1, pr[jnp.arange(tc), 1], pi[jnp.arange(tc), 1])
    q = jnp.concatenate([qfr, qbr, qfi, qbi], axis=-1).reshape(n_g, tc * S5_GROUP, 4 * S5_STATE)

    def ppart(d, pw_r, pw_i):
        xr = ccr[d][None] * pw_r[:, :, None, :] - cci[d][None] * pw_i[:, :, None, :]
        xi = ccr[d][None] * pw_i[:, :, None, :] + cci[d][None] * pw_r[:, :, None, :]
        return xr.transpose(1, 3, 2, 0), (-xi).transpose(1, 3, 2, 0)

    pfr, pfi = ppart(0, pr[1 + jnp.arange(tc), 0], pi[1 + jnp.arange(tc), 0])
    pbr, pbi = ppart(1, pr[tc - jnp.arange(tc), 1], pi[tc - jnp.arange(tc), 1])
    p = jnp.concatenate([pfr, pbr, pfi, pbi], axis=1).reshape(n_g, 4 * S5_STATE, tc * S5_GROUP)
    a = jnp.stack([jnp.concatenate([pr[tc, 0], pr[tc, 1]], axis=-1),
                   jnp.concatenate([pi[tc, 0], pi[tc, 1]], axis=-1)], axis=1)
    a = jnp.pad(a, ((0, 0), (0, 6), (0, 0)))
    return m.astype(BF16), q.astype(BF16), p.astype(BF16), a


def _s5_pack_kernel(x_ref, o_ref, f_ref, *, nb, n_g):
    half = S5_GROUP // 2
    for b in range(nb):
        xb = x_ref[b].astype(F32)
        for k in range(2):
            a = jnp.concatenate([xb[:, cc * n_g:(cc + 1) * n_g]
                                 for cc in range(k * half, (k + 1) * half)], axis=0)
            f_ref[k, pl.ds(b, n_g, stride=nb), :] = a.T
    full = jnp.concatenate([f_ref[0], f_ref[1]], axis=1)
    o_ref[...] = full.reshape(n_g, nb, S5_CHUNK * S5_GROUP).astype(o_ref.dtype)


def _s5_pack(pz, e):
    bsz, t, _ = pz.shape
    n_g = e // S5_GROUP
    n_chunks = t // S5_CHUNK
    w = S5_CHUNK * S5_GROUP
    return pl.pallas_call(
        functools.partial(_s5_pack_kernel, nb=bsz, n_g=n_g),
        grid=(n_chunks,),
        in_specs=[pl.BlockSpec((bsz, S5_CHUNK, e), lambda i: (0, i, 1))],
        out_specs=pl.BlockSpec((n_g, bsz, w), lambda i: (0, i, 0)),
        out_shape=jax.ShapeDtypeStruct((n_g, n_chunks * bsz, w), BF16),
        scratch_shapes=[pltpu.VMEM((2, n_g * bsz, w // 2), F32)],
        compiler_params=_cparams(1),
    )(pz)


def _s5_kernel(u_ref, m_ref, q_ref, p_ref, a_ref, y_ref, s_ref, h_ref, *, nb, n_chunks, nc_ctx):
    ns = S5_STATE
    u = u_ref[0]
    s_ref[...] = jnp.dot(u, q_ref[0], preferred_element_type=F32)
    ar = a_ref[0, 0:1, :]
    ai = a_ref[0, 1:2, :]
    is_fwd = lax.broadcasted_iota(jnp.int32, (nb, 2 * ns), 1) < ns

    def step(i, carry):
        hr, hi = carry
        cb = jnp.where(i < nc_ctx, nc_ctx - 1 - i, n_chunks - 1 - (i - nc_ctx))
        rf = pl.multiple_of(i * nb, nb)
        rb = pl.multiple_of(cb * nb, nb)
        h_ref[pl.ds(rf, nb), 0:ns] = hr[:, 0:ns]
        h_ref[pl.ds(rb, nb), ns:2 * ns] = hr[:, ns:2 * ns]
        h_ref[pl.ds(rf, nb), 2 * ns:3 * ns] = hi[:, 0:ns]
        h_ref[pl.ds(rb, nb), 3 * ns:4 * ns] = hi[:, ns:2 * ns]
        sf = s_ref[pl.ds(rf, nb), :]
        sb = s_ref[pl.ds(rb, nb), :]
        sr = jnp.where(is_fwd, sf[:, :2 * ns], sb[:, :2 * ns])
        si = jnp.where(is_fwd, sf[:, 2 * ns:], sb[:, 2 * ns:])
        return ar * hr - ai * hi + sr, ar * hi + ai * hr + si

    zero = jnp.zeros((nb, 2 * ns), F32)
    lax.fori_loop(0, n_chunks, step, (zero, zero), unroll=2)
    y = jnp.dot(u, m_ref[0], preferred_element_type=F32)
    y = y + jnp.dot(h_ref[...].astype(BF16), p_ref[0], preferred_element_type=F32)
    y_ref[0] = y.astype(y_ref.dtype)


def _s5_scan(uf, m, q, p, a, nb, nc_ctx):
    n_g, rows, w = uf.shape
    n_chunks = rows // nb
    wspec = pl.BlockSpec((1, w, w), lambda g: (g, 0, 0))
    return pl.pallas_call(
        functools.partial(_s5_kernel, nb=nb, n_chunks=n_chunks, nc_ctx=nc_ctx),
        grid=(n_g,),
        in_specs=[pl.BlockSpec((1, rows, w), lambda g: (g, 0, 0)), wspec, wspec, wspec,
                  pl.BlockSpec((1, 8, 2 * S5_STATE), lambda g: (g, 0, 0))],
        out_specs=pl.BlockSpec((1, rows, w), lambda g: (g, 0, 0)),
        out_shape=jax.ShapeDtypeStruct((n_g, rows, w), BF16),
        scratch_shapes=[pltpu.VMEM((rows, w), F32), pltpu.VMEM((rows, w), F32)],
        compiler_params=_cparams(1),
    )(uf, m, q, p, a)


def _s5_glu_kernel(y0_ref, yn_ref, w_ref, b_ref, o_ref, f_ref, g_ref, *, nb, n_g):
    i = pl.program_id(0)
    w = S5_CHUNK * S5_GROUP
    half = S5_GROUP // 2
    e = n_g * S5_GROUP

    def widen(y_ref):
        yv = y_ref[...].astype(F32).reshape(n_g * nb, w)
        for k in range(2):
            f_ref[k] = yv[:, k * (w // 2):(k + 1) * (w // 2)]

    def build(slot, batches):
        for b in batches:
            for k in range(2):
                t = f_ref[k, pl.ds(b, n_g, stride=nb), :].T
                g = 0.5 * t * (1.0 + jnp.tanh(math.sqrt(2.0 / math.pi) * (t + 0.044715 * t * t * t)))
                g = g.astype(BF16)
                for cc in range(half):
                    ch = k * half + cc
                    g_ref[slot, b * S5_CHUNK:(b + 1) * S5_CHUNK, ch * n_g:(ch + 1) * n_g] = (
                        g[cc * S5_CHUNK:(cc + 1) * S5_CHUNK, :])

    @pl.when(i == 0)
    def _():
        widen(y0_ref)
        build(0, range(nb))

    cur = i % 2
    nxt = 1 - cur
    n_pieces = GLU_PIECES
    tn = e // n_pieces
    per = nb // n_pieces
    widen(yn_ref)
    for p in range(n_pieces):
        cols = slice(p * tn, (p + 1) * tn)
        g = g_ref[cur]
        lin = jnp.dot(g, w_ref[:, cols], preferred_element_type=F32) + b_ref[:, cols]
        out = (g[:, cols].astype(F32) * _sigmoid(lin)).astype(o_ref.dtype)
        o_ref[:, :, cols] = out.reshape(nb, S5_CHUNK, tn)
        build(nxt, range(p * per, (p + 1) * per))


def _s5_glu(yf, w_bf, bias, bsz):
    n_g, rows, w = yf.shape
    e = n_g * S5_GROUP
    n_chunks = rows // bsz
    assert bsz % GLU_PIECES == 0 and e % GLU_PIECES == 0
    return pl.pallas_call(
        functools.partial(_s5_glu_kernel, nb=bsz, n_g=n_g),
        grid=(n_chunks,),
        in_specs=[pl.BlockSpec((n_g, bsz, w), lambda i: (0, 0, 0)),
                  pl.BlockSpec((n_g, bsz, w), lambda i: (0, jnp.minimum(i + 1, n_chunks - 1), 0)),
                  pl.BlockSpec((e, e), lambda i: (0, 0)),
                  pl.BlockSpec((1, e), lambda i: (0, 0))],
        out_specs=pl.BlockSpec((bsz, S5_CHUNK, e), lambda i: (0, i, 0)),
        out_shape=jax.ShapeDtypeStruct((bsz, n_chunks * S5_CHUNK, e), BF16),
        scratch_shapes=[pltpu.VMEM((2, n_g * bsz, w // 2), F32),
                        pltpu.VMEM((2, bsz * S5_CHUNK, e), BF16)],
        compiler_params=_cparams(1),
    )(yf, yf, w_bf, bias.astype(F32).reshape(1, e))


def _s5_mixer(pz, n_ctx, e, ops, glu_w_bf, glu_b):
    bsz = pz.shape[0]
    m, q, p, a = ops
    uf = _s5_pack(pz, e)
    yf = _s5_scan(uf, m, q, p, a, bsz, n_ctx // S5_CHUNK)
    return _s5_glu(yf, glu_w_bf, glu_b, bsz)


def _cumsum_rows_multi(xs, reverse):
    n = xs[0].shape[0]
    sub = 8
    n_slabs = n // sub
    row = lax.broadcasted_iota(jnp.int32, (sub, xs[0].shape[1]), 0)
    slabs = [[x[i * sub:(i + 1) * sub] for i in range(n_slabs)] for x in xs]
    s = 1
    while s < sub:
        nxt = []
        for arr, rev in zip(slabs, reverse):
            if rev:
                nxt.append([sl + jnp.where(row < sub - s, pltpu.roll(sl, sub - s, 0), 0.0) for sl in arr])
            else:
                nxt.append([sl + jnp.where(row >= s, pltpu.roll(sl, s, 0), 0.0) for sl in arr])
        slabs = nxt
        s *= 2
    out = []
    for arr, rev in zip(slabs, reverse):
        order = range(n_slabs - 1, -1, -1) if rev else range(n_slabs)
        edge = 0 if rev else sub - 1
        done = [None] * n_slabs
        carry = None
        for i in order:
            sl = arr[i] if carry is None else arr[i] + carry
            done[i] = sl
            carry = jnp.broadcast_to(sl[edge:edge + 1], sl.shape)
        out.append(jnp.concatenate(done, axis=0))
    return out


def _hgrn2_kernel(x_ref, lb_ref, ng_ref, o_ref,
                  g_ref, k_ref, qe_ref, ke_ref, qd_ref, kd_ref, ds_ref, dec_ref, st_ref, oi_ref,
                  *, layer, n_chunks, nc_ctx, n_tok):
    c = HG_CHUNK
    hd = HG_HEAD
    nt = (((1,), (1,)), ((), ()))
    tn = (((0,), (0,)), ((), ()))

    lbw = lb_ref[...].astype(F32)
    ew = jnp.exp(lbw - jnp.max(lbw, axis=0, keepdims=True))
    lb = jnp.sum(ew[1:layer + 1], axis=0, keepdims=True) / jnp.sum(ew, axis=0, keepdims=True)

    def gates(ci):
        r0 = pl.multiple_of(ci * c, c)
        for d in range(2):
            fl = x_ref[0, pl.ds(r0, c), (1 + d) * hd:(2 + d) * hd].astype(F32)
            f = lb + (1.0 - lb) * _sigmoid(fl)
            g_ref[d, pl.ds(r0, c), :] = jnp.log2(f)
            k_ref[d, pl.ds(r0, c), :] = 1.0 - f

    def cumdecay(cis):
        r0s = [pl.multiple_of(ci * c, c) for ci in cis]
        xs = [g_ref[d, pl.ds(r0, c), :] for r0 in r0s for d in range(2)]
        xs = _cumsum_rows_multi(xs, [False, True] * len(cis))
        k = 0
        for r0 in r0s:
            for d in range(2):
                g_ref[d, pl.ds(r0, c), :] = xs[k]
                k += 1

    def decayed(ci):
        r0 = pl.multiple_of(ci * c, c)
        q = x_ref[0, pl.ds(r0, c), 0:hd].astype(F32)
        for d in range(2):
            mid = c // 2 if d == 0 else c // 2 - 1
            last = c - 1 if d == 0 else 0
            b = g_ref[d, pl.ds(r0, c), :]
            ref = g_ref[d, pl.ds(r0 + mid, 1), :]
            b_last = g_ref[d, pl.ds(r0 + last, 1), :]
            qe = q * jnp.exp2(b - ref)
            ke = k_ref[d, pl.ds(r0, c), :] * jnp.exp2(ref - b)
            qe_ref[d, pl.ds(r0, c), :] = qe.astype(BF16)
            ke_ref[d, pl.ds(r0, c), :] = ke.astype(BF16)
            qd_ref[pl.ds(r0, c), d * hd:(d + 1) * hd] = (qe * jnp.exp2(ref)).astype(BF16)
            kd_ref[pl.ds(r0, c), d * hd:(d + 1) * hd] = (ke * jnp.exp2(b_last - ref)).astype(BF16)
            dec_ref[ci, :, d * hd:(d + 1) * hd] = jnp.exp2(b_last)

    def prepare(cis):
        for ci in cis:
            gates(ci)
        cumdecay(cis)
        for ci in cis:
            decayed(ci)

    ti = lax.broadcasted_iota(jnp.int32, (c, 2 * c), 0)
    si = lax.broadcasted_iota(jnp.int32, (c, 2 * c), 1)
    m_fwd = si <= ti
    m_bwd = si - c >= ti

    def logits(cis):
        r0s = [pl.multiple_of(ci * c, c) for ci in cis]
        vs = [x_ref[0, pl.ds(r0, c), 3 * hd:4 * hd] for r0 in r0s]
        aa = [lax.dot_general(
            jnp.concatenate([qe_ref[0, pl.ds(r0, c), :], qe_ref[1, pl.ds(r0, c), :]], axis=0),
            jnp.concatenate([ke_ref[0, pl.ds(r0, c), :], ke_ref[1, pl.ds(r0, c), :]], axis=0),
            nt, preferred_element_type=F32) for r0 in r0s]
        dss = [lax.dot_general(v, kd_ref[pl.ds(r0, c), :], tn, preferred_element_type=F32)
               for v, r0 in zip(vs, r0s)]
        return r0s, vs, aa, dss

    def values(cis, vs, aa, dss):
        atts = [jnp.where(m_fwd, a[:c], jnp.where(m_bwd, a[c:], 0.0)).astype(BF16) for a in aa]
        ois = [jnp.dot(att, jnp.concatenate([v, v], axis=0), preferred_element_type=F32)
               for att, v in zip(atts, vs)]
        for ci, ds in zip(cis, dss):
            ds_ref[ci] = ds
        return tuple(ois)

    def store_intra(cis, ois):
        for ci, oi in zip(cis, ois):
            oi_ref[pl.ds(pl.multiple_of(ci * c, c), c), :] = oi

    n_pairs = n_chunks // HG_PAIR

    def pair(t):
        return [t * HG_PAIR + u for u in range(HG_PAIR)]

    def stage(t, pending, with_next):
        cur = pair(t)
        _, vs, aa, dss = logits(cur)
        if pending is not None:
            store_intra(pair(t - 1), pending)
        if with_next:
            nxt = pair(t + 1)
            for ci in nxt:
                gates(ci)
            cumdecay(nxt)
        ois = values(cur, vs, aa, dss)
        if with_next:
            for ci in nxt:
                decayed(ci)
        return ois

    prepare(pair(0))
    pending = stage(0, None, True)
    for t in range(1, n_pairs - 1):
        pending = stage(t, pending, True)
    pending = stage(n_pairs - 1, pending, False)
    store_intra(pair(n_pairs - 1), pending)

    def scan_b(i, carry):
        sf, sb = carry
        cb = jnp.where(i < nc_ctx, nc_ctx - 1 - i, n_chunks - 1 - (i - nc_ctx))
        st_ref[i, :, 0:hd] = sf.astype(BF16)
        st_ref[cb, :, hd:2 * hd] = sb.astype(BF16)
        sf = sf * dec_ref[i, :, 0:hd] + ds_ref[i, :, 0:hd]
        sb = sb * dec_ref[cb, :, hd:2 * hd] + ds_ref[cb, :, hd:2 * hd]
        return sf, sb

    zero = jnp.zeros((hd, hd), F32)
    lax.fori_loop(0, n_chunks, scan_b, (zero, zero))

    def inter(cis):
        os_ = [lax.dot_general(qd_ref[ci * c:(ci + 1) * c, :], st_ref[ci], nt, preferred_element_type=F32)
               for ci in cis]
        for ci, o in zip(cis, os_):
            oi_ref[ci * c:(ci + 1) * c, :] += o

    gn = ng_ref[...].astype(F32)

    def head_norm(r0):
        o = oi_ref[r0:r0 + HG_NORM_ROWS, :]
        o = o * lax.rsqrt(jnp.mean(o * o, axis=-1, keepdims=True) + NORM_EPS) * gn
        o_ref[0, r0:r0 + HG_NORM_ROWS, :] = o.astype(o_ref.dtype)

    group_rows = HG_UNROLL_INTER * c
    n_groups = n_chunks // HG_UNROLL_INTER
    for gi in range(n_groups + 1):
        if gi < n_groups:
            inter(range(gi * HG_UNROLL_INTER, (gi + 1) * HG_UNROLL_INTER))
        if gi >= 1:
            for r0 in range((gi - 1) * group_rows, gi * group_rows, HG_NORM_ROWS):
                head_norm(r0)


def _hgrn2(pz, hg_lb, norm_g, layer, n_ctx, e):
    bsz, t, _ = pz.shape
    n_h = e // HG_HEAD
    n_chunks = t // HG_CHUNK
    depth = hg_lb.shape[0]
    hd = HG_HEAD
    assert n_chunks % HG_PAIR == 0 and n_chunks // HG_PAIR >= 3 and n_chunks % HG_UNROLL_INTER == 0
    assert t % HG_NORM_ROWS == 0

    return pl.pallas_call(
        functools.partial(_hgrn2_kernel, layer=layer, n_chunks=n_chunks, nc_ctx=n_ctx // HG_CHUNK, n_tok=t),
        grid=(bsz, n_h),
        in_specs=[pl.BlockSpec((1, t, 4 * hd), lambda b, h: (b, 0, e // (4 * hd) + h)),
                  pl.BlockSpec((depth, hd), lambda b, h: (0, h)),
                  pl.BlockSpec((1, hd), lambda b, h: (0, h))],
        out_specs=pl.BlockSpec((1, t, hd), lambda b, h: (b, 0, h)),
        out_shape=jax.ShapeDtypeStruct((bsz, t, e), BF16),
        scratch_shapes=[pltpu.VMEM((2, t, hd), F32),
                        pltpu.VMEM((2, t, hd), F32),
                        pltpu.VMEM((2, t, hd), BF16),
                        pltpu.VMEM((2, t, hd), BF16),
                        pltpu.VMEM((t, 2 * hd), BF16),
                        pltpu.VMEM((t, 2 * hd), BF16),
                        pltpu.VMEM((n_chunks, hd, 2 * hd), F32),
                        pltpu.VMEM((n_chunks, 1, 2 * hd), F32),
                        pltpu.VMEM((n_chunks, hd, 2 * hd), BF16),
                        pltpu.VMEM((t, hd), F32)],
        compiler_params=_cparams(2),
    )(pz, hg_lb.astype(F32), norm_g.astype(F32).reshape(1, e))


def _head_major(w, e):
    d = w.shape[0]
    n_h = e // HG_HEAD
    rest = w[:, e:].reshape(d, 4, n_h, HG_HEAD).transpose(0, 2, 1, 3).reshape(d, 4 * e)
    return jnp.concatenate([w[:, :e], rest], axis=1)


def _z_first(w, e):
    return jnp.concatenate([w[..., -e:], w[..., :-e]], axis=-1)


def kernel(x, c, ctx, c_ctx, ada_w, ada_b, ln_g, ln_b, w_out, attn_w_in, attn_sink, s5_w_in, s5_lam_re, s5_lam_im, s5_log_step, s5_b_re, s5_b_im, s5_c_re, s5_c_im, s5_d, s5_glu_w, s5_glu_b, hg_w_in, hg_lb, hg_norm_g):
    bsz, n_lat, d = x.shape
    n_ctx = ctx.shape[1]
    depth = ada_w.shape[0]
    e = w_out.shape[1]
    alpha = (2.0 * depth) ** 0.25
    t = n_ctx + n_lat
    tm = 768 if t % 768 == 0 else 256

    mod_rows = -(-(bsz + 1) // MOD_ROWS_PAD) * MOD_ROWS_PAD
    c_all = jnp.concatenate([c.astype(F32), c_ctx.astype(F32)[None],
                             jnp.zeros((mod_rows - bsz - 1, d), F32)], axis=0)
    mod = _ada(c_all, ada_w.astype(F32), ada_b.astype(F32))
    mod3 = mod.reshape(depth * mod_rows, 1, 3 * d)

    xs = (ctx.astype(F32), x.astype(F32))
    for i in range(depth):
        kind, j = i % N_MIXERS, i // N_MIXERS
        w_in = _z_first((attn_w_in, s5_w_in, hg_w_in)[kind][j].astype(BF16), e)
        w_o = w_out[i].astype(BF16)
        if kind == 1:
            w_in = jnp.concatenate([_cg_order(w_in[:, :e], 1), _cg_order(w_in[:, e:], 1)], axis=1)
            w_o = _cg_order(w_o, 0)
        elif kind == 2:
            w_in = _head_major(w_in, e)
        pz = _inproj(xs, mod3, i, mod_rows, w_in.astype(BF16), n_ctx, tn=INPROJ_TN)
        last = i == depth - 1
        if kind == 0:
            y = _attention(pz, attn_sink[j], n_ctx, e, ctx_out=not last)
        elif kind == 1:
            ops = _s5_operators(s5_lam_re[j], s5_lam_im[j], s5_log_step[j], s5_b_re[j], s5_b_im[j],
                                s5_c_re[j], s5_c_im[j], s5_d[j])
            glu_w = _cg_order(_cg_order(s5_glu_w[j].astype(BF16), 0), 1)
            y = _s5_mixer(pz, n_ctx, e, ops, glu_w, _cg_order(s5_glu_b[j], 0))
        else:
            y = _hgrn2(pz, hg_lb, hg_norm_g[j], i, n_ctx, e)
        if last and kind == 0:
            return _outproj(y, pz, xs, mod3, i, mod_rows, w_o.astype(BF16), ln_g[i].astype(F32),
                            ln_b[i].astype(F32), n_ctx, alpha, LAST_TM, latent_only=True).astype(x.dtype)
        xs = (_outproj(y, pz, xs, mod3, i, mod_rows, w_o.astype(BF16), ln_g[i].astype(F32),
                       ln_b[i].astype(F32), n_ctx, alpha, tm),)
    return xs[0][:, n_ctx:].astype(x.dtype)
```

```python
import functools
import math

import jax
import jax.numpy as jnp
from jax import lax
from jax.experimental import pallas as pl
from jax.experimental.pallas import tpu as pltpu

F32 = jnp.float32
BF16 = jnp.bfloat16

N_MIXERS = 3
HEAD_DIM = 128
Q_PER_KV = 4
WINDOW = 128
ATTN_BLOCK = 128
GRID_W = 64
ROPE_THETA = 10000.0
S5_GROUP = 16
S5_STATE = 64
S5_CHUNK = 16
HG_HEAD = 128
HG_CHUNK = 64
HG_UNROLL_INTER = 12
HG_PAIR = 4
HG_NORM_ROWS = 256
NORM_EPS = 1e-5
NEG_INF = -1e30
LOG2E = math.log2(math.e)
ATTN_ROWS = 32
MOD_ROWS_PAD = 8
INPROJ_TN = 1024
LAST_TM = 256
GLU_PIECES = 8
VMEM_LIMIT = 56 * 1024 * 1024


def _cparams(n_axes):
    return pltpu.CompilerParams(dimension_semantics=("arbitrary",) * n_axes,
                                vmem_limit_bytes=VMEM_LIMIT)


def _sigmoid(x):
    return 1.0 / (1.0 + jnp.exp(-x))


def _ada_kernel(c_ref, w_ref, b_ref, o_ref):
    cv = c_ref[...]
    s = cv * _sigmoid(cv)
    o_ref[0] = jnp.dot(s, w_ref[0], preferred_element_type=F32,
                       precision=lax.Precision.HIGHEST) + b_ref[0]


def _ada(c_all, ada_w, ada_b):
    depth, d, d3 = ada_w.shape
    rows = c_all.shape[0]
    return pl.pallas_call(
        _ada_kernel,
        grid=(depth, d3 // d),
        in_specs=[pl.BlockSpec((rows, d), lambda i, j: (0, 0)),
                  pl.BlockSpec((1, d, d), lambda i, j: (i, 0, j)),
                  pl.BlockSpec((1, 1, d), lambda i, j: (i, 0, j))],
        out_specs=pl.BlockSpec((1, rows, d), lambda i, j: (i, 0, j)),
        out_shape=jax.ShapeDtypeStruct((depth, rows, d3), F32),
        compiler_params=_cparams(2),
    )(c_all, ada_w, ada_b.reshape(depth, 1, d3))


def _inproj_kernel(*refs, n_ctx, d, split):
    if split:
        xc_ref, xl_ref, mb_ref, mc_ref, w_ref, o_ref, h_ref = refs
    else:
        x_ref, mb_ref, mc_ref, w_ref, o_ref, h_ref = refs

    @pl.when(pl.program_id(1) == 0)
    def _():
        mb = mb_ref[0]
        mc = mc_ref[0]
        xc = xc_ref[0] if split else x_ref[0, :n_ctx]
        xl = xl_ref[0] if split else x_ref[0, n_ctx:]
        h_ref[:n_ctx] = (xc * (1.0 + mc[:, d:2 * d]) + mc[:, :d]).astype(BF16)
        h_ref[n_ctx:] = (xl * (1.0 + mb[:, d:2 * d]) + mb[:, :d]).astype(BF16)

    o_ref[0] = jnp.dot(h_ref[...], w_ref[...], preferred_element_type=F32).astype(o_ref.dtype)


def _inproj(xs, mod3, layer, mod_rows, w_bf, n_ctx, tn):
    split = len(xs) == 2
    bsz, _, d = xs[0].shape
    t = sum(a.shape[1] for a in xs)
    n = w_bf.shape[1]
    base = layer * mod_rows
    return pl.pallas_call(
        functools.partial(_inproj_kernel, n_ctx=n_ctx, d=d, split=split),
        grid=(bsz, n // tn),
        in_specs=[pl.BlockSpec((1, a.shape[1], d), lambda b, j: (b, 0, 0)) for a in xs] + [
                  pl.BlockSpec((1, 1, 3 * d), lambda b, j: (base + b, 0, 0)),
                  pl.BlockSpec((1, 1, 3 * d), lambda b, j: (base + bsz, 0, 0)),
                  pl.BlockSpec((d, tn), lambda b, j: (0, j))],
        out_specs=pl.BlockSpec((1, t, tn), lambda b, j: (b, 0, j)),
        out_shape=jax.ShapeDtypeStruct((bsz, t, n), BF16),
        scratch_shapes=[pltpu.VMEM((t, d), BF16)],
        compiler_params=_cparams(2),
    )(*xs, mod3, mod3, w_bf)


def _outproj_kernel(y_ref, z_ref, *refs, n_ctx, d, alpha, tm, first_tile, n_pieces):
    if n_pieces:
        x_refs, (c_ref, mb_ref, mc_ref, w_ref, g_ref, b_ref, o_ref) = refs[:n_pieces], refs[n_pieces:]
        first = jnp.where(pl.program_id(1) == 0, c_ref[0], x_refs[0][0])
        x_tile = jnp.concatenate([first] + [r[0] for r in x_refs[1:]], axis=0)
    else:
        x_ref, mb_ref, mc_ref, w_ref, g_ref, b_ref, o_ref = refs
        x_tile = x_ref[0]
    z = z_ref[0]
    a = y_ref[0] * (z * _sigmoid(z))
    br = jnp.dot(a, w_ref[...], preferred_element_type=F32)
    rows = (pl.program_id(1) + first_tile) * tm + lax.broadcasted_iota(jnp.int32, (tm, 1), 0)
    gate = jnp.where(rows < n_ctx, mc_ref[0][:, 2 * d:], mb_ref[0][:, 2 * d:])
    v = alpha * x_tile + gate * br
    mu = jnp.mean(v, axis=-1, keepdims=True)
    vc = v - mu
    var = jnp.mean(vc * vc, axis=-1, keepdims=True)
    o_ref[0] = vc * lax.rsqrt(var + NORM_EPS) * g_ref[...] + b_ref[...]


def _outproj(y, pz, xs, mod3, layer, mod_rows, w_bf, ln_g, ln_b, n_ctx, alpha, tm, latent_only=False):
    split = len(xs) == 2
    bsz, _, d = xs[0].shape
    t = sum(a.shape[1] for a in xs)
    e = w_bf.shape[0]
    base = layer * mod_rows
    off = n_ctx // tm if latent_only else 0
    assert n_ctx % tm == 0 or not latent_only
    n_pieces = tm // n_ctx if split else 0
    assert not split or (tm % n_ctx == 0 and not latent_only)
    if split:
        x_specs = [pl.BlockSpec((1, n_ctx, d), lambda b, i, p=p: (b, jnp.maximum(i * n_pieces + p - 1, 0), 0))
                   for p in range(n_pieces)] + [pl.BlockSpec((1, n_ctx, d), lambda b, i: (b, 0, 0))]
        x_args = [xs[1]] * n_pieces + [xs[0]]
    else:
        x_specs = [pl.BlockSpec((1, tm, d), lambda b, i: (b, i + off, 0))]
        x_args = [xs[0]]
    return pl.pallas_call(
        functools.partial(_outproj_kernel, n_ctx=n_ctx, d=d, alpha=alpha, tm=tm, first_tile=off,
                          n_pieces=n_pieces),
        grid=(bsz, t // tm - off),
        in_specs=[pl.BlockSpec((1, tm, e), lambda b, i: (b, i, 0)),
                  pl.BlockSpec((1, tm, e), lambda b, i: (b, i + off, 0))] + x_specs + [
                  pl.BlockSpec((1, 1, 3 * d), lambda b, i: (base + b, 0, 0)),
                  pl.BlockSpec((1, 1, 3 * d), lambda b, i: (base + bsz, 0, 0)),
                  pl.BlockSpec((e, d), lambda b, i: (0, 0)),
                  pl.BlockSpec((1, d), lambda b, i: (0, 0)),
                  pl.BlockSpec((1, d), lambda b, i: (0, 0))],
        out_specs=pl.BlockSpec((1, tm, d), lambda b, i: (b, i, 0)),
        out_shape=jax.ShapeDtypeStruct((bsz, t - off * tm, d), F32),
        compiler_params=_cparams(2),
    )(y, pz, *x_args, mod3, mod3, w_bf, ln_g.reshape(1, d), ln_b.reshape(1, d))


def _rope_tables(n_lat):
    pos = jnp.arange(n_lat, dtype=jnp.int32)
    row = (pos // GRID_W).astype(F32)
    col = (pos % GRID_W).astype(F32)
    nf = HEAD_DIM // 4
    inv_freq = jnp.power(ROPE_THETA, -jnp.arange(nf, dtype=F32) / nf)
    ang_r = row[:, None] * inv_freq[None, :]
    ang_c = col[:, None] * inv_freq[None, :]
    zeros = jnp.zeros_like(ang_r)
    cos = jnp.concatenate([jnp.cos(ang_r)] * 2 + [jnp.cos(ang_c)] * 2, axis=-1)
    s1 = jnp.concatenate([-jnp.sin(ang_r), zeros, -jnp.sin(ang_c), zeros], axis=-1)
    s2 = jnp.concatenate([zeros, jnp.sin(ang_r), zeros, jnp.sin(ang_c)], axis=-1)
    return cos, s1, s2


def _rope(x, cos, s1, s2):
    quarter = HEAD_DIM // 4
    return (x * cos + pltpu.roll(x, HEAD_DIM - quarter, 1) * s1
            + pltpu.roll(x, quarter, 1) * s2)


def _attn_kernel(sink_ref, q_ref, kv_ref, cq_ref, s1q_ref, s2q_ref, ck_ref, s1k_ref, s2k_ref,
                 o_ref, kr_ref, vx_ref, s_ref, p_ref, e_ref, bias_ref, *, n_ctx, n_lat, n_kv, first_block):
    j = pl.program_id(1) + first_block
    nb_ctx = n_ctx // ATTN_BLOCK
    kvw = n_kv * HEAD_DIM
    n_win = ATTN_BLOCK + 2 * WINDOW
    qscale = HEAD_DIM ** -0.5 * LOG2E
    rows_q = Q_PER_KV * ATTN_BLOCK
    vxw = 2 * HEAD_DIM
    nt = (((1,), (1,)), ((), ()))

    @pl.when(pl.program_id(1) == 0)
    def _():
        for h in range(n_kv):
            sl = slice(h * HEAD_DIM, (h + 1) * HEAD_DIM)
            k = kv_ref[0, n_ctx:, sl].astype(F32)
            kr_ref[:, sl] = _rope(k, ck_ref[...], s1k_ref[...], s2k_ref[...]).astype(BF16)
            vx_ref[:, h * vxw:h * vxw + HEAD_DIM] = kv_ref[0, :, kvw + h * HEAD_DIM:kvw + (h + 1) * HEAD_DIM]
            vx_ref[:, h * vxw + HEAD_DIM:(h + 1) * vxw] = jnp.ones((n_ctx + n_lat, HEAD_DIM), BF16)

    def key_cols(kparts):
        cols, off = [], 0
        for kp in kparts:
            cols.append((off, kp.shape[0]))
            off += kp.shape[0]
        return cols

    def logit_tasks(h, q_fn, kparts):
        box = {}

        def task(kp, o_, n):
            def run():
                if "q" not in box:
                    box["q"] = q_fn(h)
                s_ref[h % 2, :, o_:o_ + n] = lax.dot_general(box["q"], kp, nt, preferred_element_type=F32)
            return run

        return [task(kp, o_, n) for kp, (o_, n) in zip(kparts, key_cols(kparts))]

    def softmax_rows(h, cols, use_bias, blocks):
        for rb in blocks:
            r0 = rb * ATTN_ROWS
            sk = sink_ref[h * Q_PER_KV + r0 // ATTN_BLOCK] * LOG2E
            parts = []
            for idx, (o_, n) in enumerate(cols):
                s = s_ref[h % 2, r0:r0 + ATTN_ROWS, o_:o_ + n]
                if use_bias and idx == 0:
                    ql0 = r0 % ATTN_BLOCK
                    s = s + bias_ref[ql0:ql0 + ATTN_ROWS, :]
                parts.append(s)
            m = jnp.max(parts[0], axis=-1, keepdims=True)
            for s in parts[1:]:
                m = jnp.maximum(m, jnp.max(s, axis=-1, keepdims=True))
            m = jnp.maximum(m, sk)
            for (o_, n), s in zip(cols, parts):
                p_ref[h % 2, r0:r0 + ATTN_ROWS, o_:o_ + n] = jnp.exp2(s - m).astype(BF16)
            e_ref[h % 2, r0:r0 + ATTN_ROWS, :] = jnp.exp2(sk - m)

    def value_tasks(h, cols, vparts):
        box = {}

        def task(idx, o_, n, vp):
            def run():
                part = jnp.dot(p_ref[h % 2, :, o_:o_ + n], vp, preferred_element_type=F32)
                box["ox"] = part if idx == 0 else box["ox"] + part
                if idx == len(cols) - 1:
                    ox = box["ox"]
                    l = ox[:, HEAD_DIM:HEAD_DIM + 1] + e_ref[h % 2]
                    o = ox[:, :HEAD_DIM] * (1.0 / l)
                    for g in range(Q_PER_KV):
                        hq = h * Q_PER_KV + g
                        o_ref[0, :, hq * HEAD_DIM:(hq + 1) * HEAD_DIM] = (
                            o[g * ATTN_BLOCK:(g + 1) * ATTN_BLOCK].astype(o_ref.dtype))
            return run

        return [task(i, o_, n, vp) for i, ((o_, n), vp) in enumerate(zip(cols, vparts))]

    def attend_all(q_fn, k_fn, v_fn, use_bias):
        n_blocks = rows_q // ATTN_ROWS
        quarter = n_blocks // 4
        cols = key_cols(k_fn(0))
        for task in logit_tasks(0, q_fn, k_fn(0)):
            task()
        for h in range(n_kv + 1):
            side = []
            if h + 1 < n_kv:
                side += logit_tasks(h + 1, q_fn, k_fn(h + 1))
            if 1 <= h:
                side += value_tasks(h - 1, cols, v_fn(h - 1))
            for qi in range(4):
                if h < n_kv:
                    softmax_rows(h, cols, use_bias, range(qi * quarter, (qi + 1) * quarter))
                for task in side[qi::4]:
                    task()

    def q_head(hq):
        return q_ref[0, :, hq * HEAD_DIM:(hq + 1) * HEAD_DIM].astype(F32)

    def head_slice(h):
        return slice(h * HEAD_DIM, (h + 1) * HEAD_DIM)

    @pl.when(j < nb_ctx)
    def _():
        attend_all(
            lambda h: jnp.concatenate([(q_head(h * Q_PER_KV + g) * qscale).astype(BF16)
                                       for g in range(Q_PER_KV)], axis=0),
            lambda h: [kv_ref[0, :n_ctx, head_slice(h)]],
            lambda h: [vx_ref[:n_ctx, h * vxw:(h + 1) * vxw]], False)

    @pl.when(j >= nb_ctx)
    def _():
        jl = j - nb_ctx
        start = jnp.clip((jl - 1) * ATTN_BLOCK, 0, n_lat - n_win)
        start = pl.multiple_of(start, ATTN_BLOCK)
        delta = jl * ATTN_BLOCK - start
        ql = lax.broadcasted_iota(jnp.int32, (ATTN_BLOCK, n_win), 0)
        kl = lax.broadcasted_iota(jnp.int32, (ATTN_BLOCK, n_win), 1)
        bias_ref[...] = jnp.where(jnp.abs(delta + ql - kl) <= WINDOW, 0.0, NEG_INF)
        cq = cq_ref[...] * qscale
        s1q = s1q_ref[...] * qscale
        s2q = s2q_ref[...] * qscale
        attend_all(
            lambda h: jnp.concatenate([_rope(q_head(h * Q_PER_KV + g), cq, s1q, s2q).astype(BF16)
                                       for g in range(Q_PER_KV)], axis=0),
            lambda h: [kr_ref[pl.ds(start, n_win), head_slice(h)], kv_ref[0, :n_ctx, head_slice(h)]],
            lambda h: [vx_ref[pl.ds(n_ctx + start, n_win), h * vxw:(h + 1) * vxw],
                       vx_ref[:n_ctx, h * vxw:(h + 1) * vxw]], True)


def _attention(pz, sink, n_ctx, e, ctx_out):
    bsz, t, _ = pz.shape
    n_lat = t - n_ctx
    n_q = e // HEAD_DIM
    n_kv = n_q // Q_PER_KV
    kvw = n_kv * HEAD_DIM
    n_win = ATTN_BLOCK + 2 * WINDOW
    rows_q = Q_PER_KV * ATTN_BLOCK
    assert (2 * e) % (2 * kvw) == 0 and n_lat >= n_win
    cos, s1, s2 = _rope_tables(n_lat)
    nb_ctx = n_ctx // ATTN_BLOCK
    first = 0 if ctx_out else nb_ctx
    qtab = pl.BlockSpec((ATTN_BLOCK, HEAD_DIM), lambda b, j: (jnp.maximum(j + first - nb_ctx, 0), 0))
    ktab = pl.BlockSpec((n_lat, HEAD_DIM), lambda b, j: (0, 0))
    return pl.pallas_call(
        functools.partial(_attn_kernel, n_ctx=n_ctx, n_lat=n_lat, n_kv=n_kv, first_block=first),
        grid=(bsz, t // ATTN_BLOCK - first),
        in_specs=[pl.BlockSpec(memory_space=pltpu.SMEM),
                  pl.BlockSpec((1, ATTN_BLOCK, e), lambda b, j: (b, j + first, 1)),
                  pl.BlockSpec((1, t, 2 * kvw), lambda b, j: (b, 0, 2 * e // (2 * kvw))),
                  qtab, qtab, qtab, ktab, ktab, ktab],
        out_specs=pl.BlockSpec((1, ATTN_BLOCK, e), lambda b, j: (b, j, 0)),
        out_shape=jax.ShapeDtypeStruct((bsz, t - first * ATTN_BLOCK, e), BF16),
        scratch_shapes=[pltpu.VMEM((n_lat, kvw), BF16),
                        pltpu.VMEM((t, n_kv * 2 * HEAD_DIM), BF16),
                        pltpu.VMEM((2, rows_q, n_win + n_ctx), F32),
                        pltpu.VMEM((2, rows_q, n_win + n_ctx), BF16),
                        pltpu.VMEM((2, rows_q, 1), F32),
                        pltpu.VMEM((ATTN_BLOCK, n_win), F32)],
        compiler_params=_cparams(2),
    )(sink.astype(F32), pz, pz, cos, s1, s2, cos, s1, s2)


def _cg_order(w, axis):
    axis = axis % w.ndim
    n = w.shape[axis]
    shp = w.shape[:axis] + (n // S5_GROUP, S5_GROUP) + w.shape[axis + 1:]
    return jnp.swapaxes(w.reshape(shp), axis, axis + 1).reshape(w.shape)


def _s5_operators(lam_re, lam_im, log_step, b_re, b_im, c_re, c_im, d_skip):
    n_g = lam_re.shape[1]
    tc = S5_CHUNK
    dt = jnp.exp(log_step.astype(F32))[..., None]
    lr, li = lam_re.astype(F32) * dt, lam_im.astype(F32) * dt
    k = jnp.arange(tc + 1, dtype=F32)[:, None, None, None]
    mag = jnp.exp(lr[None] * k)
    pr, pi = mag * jnp.cos(li[None] * k), mag * jnp.sin(li[None] * k)
    ar1, ai1 = pr[1] - 1.0, pi[1]
    den = lam_re.astype(F32) ** 2 + lam_im.astype(F32) ** 2
    cr = (ar1 * lam_re + ai1 * lam_im) / den
    ci = (ai1 * lam_re - ar1 * lam_im) / den
    br, bi = b_re.astype(F32), b_im.astype(F32)
    bbr = cr[..., None] * br - ci[..., None] * bi
    bbi = cr[..., None] * bi + ci[..., None] * br
    ccr, cci = c_re.astype(F32), c_im.astype(F32)

    def kern(d):
        xr = pr[:tc, d][..., None] * bbr[d][None] - pi[:tc, d][..., None] * bbi[d][None]
        xi = pr[:tc, d][..., None] * bbi[d][None] + pi[:tc, d][..., None] * bbr[d][None]
        return (jnp.einsum('gnp,tgpm->tgnm', ccr[d], xr) - jnp.einsum('gnp,tgpm->tgnm', cci[d], xi))

    kf, kb = kern(0), kern(1)
    s_idx = jnp.arange(tc)[:, None]
    t_idx = jnp.arange(tc)[None, :]
    lag = t_idx - s_idx
    mf = jnp.where((lag >= 0)[:, :, None, None, None], kf[jnp.clip(lag, 0, tc - 1)], 0.0)
    mb = jnp.where((lag <= 0)[:, :, None, None, None], kb[jnp.clip(-lag, 0, tc - 1)], 0.0)
    m = (mf + mb).transpose(2, 4, 0, 3, 1)
    eye = (jnp.eye(S5_GROUP, dtype=F32)[:, None, :, None] * jnp.eye(tc, dtype=F32)[None, :, None, :])
    m = m + eye[None] * d_skip.astype(F32).reshape(n_g, S5_GROUP, 1, 1, 1)
    m = m.reshape(n_g, tc * S5_GROUP, tc * S5_GROUP)

    def qpart(d, pw_r, pw_i):
        qr = pw_r[..., None] * bbr[d][None] - pw_i[..., None] * bbi[d][None]
        qi = pw_r[..., None] * bbi[d][None] + pw_i[..., None] * bbr[d][None]
        return qr.transpose(1, 3, 0, 2), qi.transpose(1, 3, 0, 2)

    qfr, qfi = qpart(0, pr[tc - 1 - jnp.arange(tc), 0], pi[tc - 1 - jnp.arange(tc), 0])
    qbr, qbi = qpart(1, pr[jnp.arange(tc), 1], pi[jnp.arange(tc), 1])
    q = jnp.concatenate([qfr, qbr, qfi, qbi], axis=-1).reshape(n_g, tc * S5_GROUP, 4 * S5_STATE)

    def ppart(d, pw_r, pw_i):
        xr = ccr[d][None] * pw_r[:, :, None, :] - cci[d][None] * pw_i[:, :, None, :]
        xi = ccr[d][None] * pw_i[:, :, None, :] + cci[d][None] * pw_r[:, :, None, :]
        return xr.transpose(1, 3, 2, 0), (-xi).transpose(1, 3, 2, 0)

    pfr, pfi = ppart(0, pr[1 + jnp.arange(tc), 0], pi[1 + jnp.arange(tc), 0])
    pbr, pbi = ppart(1, pr[tc - jnp.arange(tc), 1], pi[tc - jnp.arange(tc), 1])
    p = jnp.concatenate([pfr, pbr, pfi, pbi], axis=1).reshape(n_g, 4 * S5_STATE, tc * S5_GROUP)
    a = jnp.stack([jnp.concatenate([pr[tc, 0], pr[tc, 1]], axis=-1),
                   jnp.concatenate([pi[tc, 0], pi[tc, 1]], axis=-1)], axis=1)
    a = jnp.pad(a, ((0, 0), (0, 6), (0, 0)))
    return m.astype(BF16), q.astype(BF16), p.astype(BF16), a


def _s5_pack_kernel(x_ref, o_ref, f_ref, *, nb, n_g):
    half = S5_GROUP // 2
    for b in range(nb):
        xb = x_ref[b].astype(F32)
        for k in range(2):
            a = jnp.concatenate([xb[:, cc * n_g:(cc + 1) * n_g]
                                 for cc in range(k * half, (k + 1) * half)], axis=0)
            f_ref[k, pl.ds(b, n_g, stride=nb), :] = a.T
    full = jnp.concatenate([f_ref[0], f_ref[1]], axis=1)
    o_ref[...] = full.reshape(n_g, nb, S5_CHUNK * S5_GROUP).astype(o_ref.dtype)


def _s5_pack(pz, e):
    bsz, t, _ = pz.shape
    n_g = e // S5_GROUP
    n_chunks = t // S5_CHUNK
    w = S5_CHUNK * S5_GROUP
    return pl.pallas_call(
        functools.partial(_s5_pack_kernel, nb=bsz, n_g=n_g),
        grid=(n_chunks,),
        in_specs=[pl.BlockSpec((bsz, S5_CHUNK, e), lambda i: (0, i, 1))],
        out_specs=pl.BlockSpec((n_g, bsz, w), lambda i: (0, i, 0)),
        out_shape=jax.ShapeDtypeStruct((n_g, n_chunks * bsz, w), BF16),
        scratch_shapes=[pltpu.VMEM((2, n_g * bsz, w // 2), F32)],
        compiler_params=_cparams(1),
    )(pz)


def _s5_kernel(u_ref, m_ref, q_ref, p_ref, a_ref, y_ref, s_ref, h_ref, *, nb, n_chunks, nc_ctx):
    ns = S5_STATE
    u = u_ref[0]
    s_ref[...] = jnp.dot(u, q_ref[0], preferred_element_type=F32)
    ar = a_ref[0, 0:1, :]
    ai = a_ref[0, 1:2, :]
    is_fwd = lax.broadcasted_iota(jnp.int32, (nb, 2 * ns), 1) < ns

    def step(i, carry):
        hr, hi = carry
        cb = jnp.where(i < nc_ctx, nc_ctx - 1 - i, n_chunks - 1 - (i - nc_ctx))
        rf = pl.multiple_of(i * nb, nb)
        rb = pl.multiple_of(cb * nb, nb)
        h_ref[pl.ds(rf, nb), 0:ns] = hr[:, 0:ns]
        h_ref[pl.ds(rb, nb), ns:2 * ns] = hr[:, ns:2 * ns]
        h_ref[pl.ds(rf, nb), 2 * ns:3 * ns] = hi[:, 0:ns]
        h_ref[pl.ds(rb, nb), 3 * ns:4 * ns] = hi[:, ns:2 * ns]
        sf = s_ref[pl.ds(rf, nb), :]
        sb = s_ref[pl.ds(rb, nb), :]
        sr = jnp.where(is_fwd, sf[:, :2 * ns], sb[:, :2 * ns])
        si = jnp.where(is_fwd, sf[:, 2 * ns:], sb[:, 2 * ns:])
        return ar * hr - ai * hi + sr, ar * hi + ai * hr + si

    zero = jnp.zeros((nb, 2 * ns), F32)
    lax.fori_loop(0, n_chunks, step, (zero, zero), unroll=2)
    y = jnp.dot(u, m_ref[0], preferred_element_type=F32)
    y = y + jnp.dot(h_ref[...].astype(BF16), p_ref[0], preferred_element_type=F32)
    y_ref[0] = y.astype(y_ref.dtype)


def _s5_scan(uf, m, q, p, a, nb, nc_ctx):
    n_g, rows, w = uf.shape
    n_chunks = rows // nb
    wspec = pl.BlockSpec((1, w, w), lambda g: (g, 0, 0))
    return pl.pallas_call(
        functools.partial(_s5_kernel, nb=nb, n_chunks=n_chunks, nc_ctx=nc_ctx),
        grid=(n_g,),
        in_specs=[pl.BlockSpec((1, rows, w), lambda g: (g, 0, 0)), wspec, wspec, wspec,
                  pl.BlockSpec((1, 8, 2 * S5_STATE), lambda g: (g, 0, 0))],
        out_specs=pl.BlockSpec((1, rows, w), lambda g: (g, 0, 0)),
        out_shape=jax.ShapeDtypeStruct((n_g, rows, w), BF16),
        scratch_shapes=[pltpu.VMEM((rows, w), F32), pltpu.VMEM((rows, w), F32)],
        compiler_params=_cparams(1),
    )(uf, m, q, p, a)


def _s5_glu_kernel(y0_ref, yn_ref, w_ref, b_ref, o_ref, f_ref, g_ref, *, nb, n_g):
    i = pl.program_id(0)
    w = S5_CHUNK * S5_GROUP
    half = S5_GROUP // 2
    e = n_g * S5_GROUP

    def widen(y_ref):
        yv = y_ref[...].astype(F32).reshape(n_g * nb, w)
        for k in range(2):
            f_ref[k] = yv[:, k * (w // 2):(k + 1) * (w // 2)]

    def build(slot, batches):
        for b in batches:
            for k in range(2):
                t = f_ref[k, pl.ds(b, n_g, stride=nb), :].T
                g = 0.5 * t * (1.0 + jnp.tanh(math.sqrt(2.0 / math.pi) * (t + 0.044715 * t * t * t)))
                g = g.astype(BF16)
                for cc in range(half):
                    ch = k * half + cc
                    g_ref[slot, b * S5_CHUNK:(b + 1) * S5_CHUNK, ch * n_g:(ch + 1) * n_g] = (
                        g[cc * S5_CHUNK:(cc + 1) * S5_CHUNK, :])

    @pl.when(i == 0)
    def _():
        widen(y0_ref)
        build(0, range(nb))

    cur = i % 2
    nxt = 1 - cur
    n_pieces = GLU_PIECES
    tn = e // n_pieces
    per = nb // n_pieces
    widen(yn_ref)
    for p in range(n_pieces):
        cols = slice(p * tn, (p + 1) * tn)
        g = g_ref[cur]
        lin = jnp.dot(g, w_ref[:, cols], preferred_element_type=F32) + b_ref[:, cols]
        out = (g[:, cols].astype(F32) * _sigmoid(lin)).astype(o_ref.dtype)
        o_ref[:, :, cols] = out.reshape(nb, S5_CHUNK, tn)
        build(nxt, range(p * per, (p + 1) * per))


def _s5_glu(yf, w_bf, bias, bsz):
    n_g, rows, w = yf.shape
    e = n_g * S5_GROUP
    n_chunks = rows // bsz
    assert bsz % GLU_PIECES == 0 and e % GLU_PIECES == 0
    return pl.pallas_call(
        functools.partial(_s5_glu_kernel, nb=bsz, n_g=n_g),
        grid=(n_chunks,),
        in_specs=[pl.BlockSpec((n_g, bsz, w), lambda i: (0, 0, 0)),
                  pl.BlockSpec((n_g, bsz, w), lambda i: (0, jnp.minimum(i + 1, n_chunks - 1), 0)),
                  pl.BlockSpec((e, e), lambda i: (0, 0)),
                  pl.BlockSpec((1, e), lambda i: (0, 0))],
        out_specs=pl.BlockSpec((bsz, S5_CHUNK, e), lambda i: (0, i, 0)),
        out_shape=jax.ShapeDtypeStruct((bsz, n_chunks * S5_CHUNK, e), BF16),
        scratch_shapes=[pltpu.VMEM((2, n_g * bsz, w // 2), F32),
                        pltpu.VMEM((2, bsz * S5_CHUNK, e), BF16)],
        compiler_params=_cparams(1),
    )(yf, yf, w_bf, bias.astype(F32).reshape(1, e))


def _s5_mixer(pz, n_ctx, e, ops, glu_w_bf, glu_b):
    bsz = pz.shape[0]
    m, q, p, a = ops
    uf = _s5_pack(pz, e)
    yf = _s5_scan(uf, m, q, p, a, bsz, n_ctx // S5_CHUNK)
    return _s5_glu(yf, glu_w_bf, glu_b, bsz)


def _cumsum_rows_multi(xs, reverse):
    n = xs[0].shape[0]
    sub = 8
    n_slabs = n // sub
    row = lax.broadcasted_iota(jnp.int32, (sub, xs[0].shape[1]), 0)
    slabs = [[x[i * sub:(i + 1) * sub] for i in range(n_slabs)] for x in xs]
    s = 1
    while s < sub:
        nxt = []
        for arr, rev in zip(slabs, reverse):
            if rev:
                nxt.append([sl + jnp.where(row < sub - s, pltpu.roll(sl, sub - s, 0), 0.0) for sl in arr])
            else:
                nxt.append([sl + jnp.where(row >= s, pltpu.roll(sl, s, 0), 0.0) for sl in arr])
        slabs = nxt
        s *= 2
    out = []
    for arr, rev in zip(slabs, reverse):
        order = range(n_slabs - 1, -1, -1) if rev else range(n_slabs)
        edge = 0 if rev else sub - 1
        done = [None] * n_slabs
        carry = None
        for i in order:
            sl = arr[i] if carry is None else arr[i] + carry
            done[i] = sl
            carry = jnp.broadcast_to(sl[edge:edge + 1], sl.shape)
        out.append(jnp.concatenate(done, axis=0))
    return out


def _hgrn2_kernel(x_ref, lb_ref, ng_ref, o_ref,
                  g_ref, k_ref, qe_ref, ke_ref, qd_ref, kd_ref, ds_ref, dec_ref, st_ref, oi_ref,
                  *, layer, n_chunks, nc_ctx, n_tok):
    c = HG_CHUNK
    hd = HG_HEAD
    nt = (((1,), (1,)), ((), ()))
    tn = (((0,), (0,)), ((), ()))

    lbw = lb_ref[...].astype(F32)
    ew = jnp.exp(lbw - jnp.max(lbw, axis=0, keepdims=True))
    lb = jnp.sum(ew[1:layer + 1], axis=0, keepdims=True) / jnp.sum(ew, axis=0, keepdims=True)

    def gates(ci):
        r0 = pl.multiple_of(ci * c, c)
        for d in range(2):
            fl = x_ref[0, pl.ds(r0, c), (1 + d) * hd:(2 + d) * hd].astype(F32)
            f = lb + (1.0 - lb) * _sigmoid(fl)
            g_ref[d, pl.ds(r0, c), :] = jnp.log2(f)
            k_ref[d, pl.ds(r0, c), :] = 1.0 - f

    def cumdecay(cis):
        r0s = [pl.multiple_of(ci * c, c) for ci in cis]
        xs = [g_ref[d, pl.ds(r0, c), :] for r0 in r0s for d in range(2)]
        xs = _cumsum_rows_multi(xs, [False, True] * len(cis))
        k = 0
        for r0 in r0s:
            for d in range(2):
                g_ref[d, pl.ds(r0, c), :] = xs[k]
                k += 1

    def decayed(ci):
        r0 = pl.multiple_of(ci * c, c)
        q = x_ref[0, pl.ds(r0, c), 0:hd].astype(F32)
        for d in range(2):
            mid = c // 2 if d == 0 else c // 2 - 1
            last = c - 1 if d == 0 else 0
            b = g_ref[d, pl.ds(r0, c), :]
            ref = g_ref[d, pl.ds(r0 + mid, 1), :]
            b_last = g_ref[d, pl.ds(r0 + last, 1), :]
            qe = q * jnp.exp2(b - ref)
            ke = k_ref[d, pl.ds(r0, c), :] * jnp.exp2(ref - b)
            qe_ref[d, pl.ds(r0, c), :] = qe.astype(BF16)
            ke_ref[d, pl.ds(r0, c), :] = ke.astype(BF16)
            qd_ref[pl.ds(r0, c), d * hd:(d + 1) * hd] = (qe * jnp.exp2(ref)).astype(BF16)
            kd_ref[pl.ds(r0, c), d * hd:(d + 1) * hd] = (ke * jnp.exp2(b_last - ref)).astype(BF16)
            dec_ref[ci, :, d * hd:(d + 1) * hd] = jnp.exp2(b_last)

    def prepare(cis):
        for ci in cis:
            gates(ci)
        cumdecay(cis)
        for ci in cis:
            decayed(ci)

    ti = lax.broadcasted_iota(jnp.int32, (c, 2 * c), 0)
    si = lax.broadcasted_iota(jnp.int32, (c, 2 * c), 1)
    m_fwd = si <= ti
    m_bwd = si - c >= ti

    def logits(cis):
        r0s = [pl.multiple_of(ci * c, c) for ci in cis]
        vs = [x_ref[0, pl.ds(r0, c), 3 * hd:4 * hd] for r0 in r0s]
        aa = [lax.dot_general(
            jnp.concatenate([qe_ref[0, pl.ds(r0, c), :], qe_ref[1, pl.ds(r0, c), :]], axis=0),
            jnp.concatenate([ke_ref[0, pl.ds(r0, c), :], ke_ref[1, pl.ds(r0, c), :]], axis=0),
            nt, preferred_element_type=F32) for r0 in r0s]
        dss = [lax.dot_general(v, kd_ref[pl.ds(r0, c), :], tn, preferred_element_type=F32)
               for v, r0 in zip(vs, r0s)]
        return r0s, vs, aa, dss

    def values(cis, vs, aa, dss):
        atts = [jnp.where(m_fwd, a[:c], jnp.where(m_bwd, a[c:], 0.0)).astype(BF16) for a in aa]
        ois = [jnp.dot(att, jnp.concatenate([v, v], axis=0), preferred_element_type=F32)
               for att, v in zip(atts, vs)]
        for ci, ds in zip(cis, dss):
            ds_ref[ci] = ds
        return tuple(ois)

    def store_intra(cis, ois):
        for ci, oi in zip(cis, ois):
            oi_ref[pl.ds(pl.multiple_of(ci * c, c), c), :] = oi

    n_pairs = n_chunks // HG_PAIR

    def pair(t):
        return [t * HG_PAIR + u for u in range(HG_PAIR)]

    def stage(t, pending, with_next):
        cur = pair(t)
        _, vs, aa, dss = logits(cur)
        if pending is not None:
            store_intra(pair(t - 1), pending)
        if with_next:
            nxt = pair(t + 1)
            for ci in nxt:
                gates(ci)
            cumdecay(nxt)
        ois = values(cur, vs, aa, dss)
        if with_next:
            for ci in nxt:
                decayed(ci)
        return ois

    prepare(pair(0))
    pending = stage(0, None, True)
    for t in range(1, n_pairs - 1):
        pending = stage(t, pending, True)
    pending = stage(n_pairs - 1, pending, False)
    store_intra(pair(n_pairs - 1), pending)

    def scan_b(i, carry):
        sf, sb = carry
        cb = jnp.where(i < nc_ctx, nc_ctx - 1 - i, n_chunks - 1 - (i - nc_ctx))
        st_ref[i, :, 0:hd] = sf.astype(BF16)
        st_ref[cb, :, hd:2 * hd] = sb.astype(BF16)
        sf = sf * dec_ref[i, :, 0:hd] + ds_ref[i, :, 0:hd]
        sb = sb * dec_ref[cb, :, hd:2 * hd] + ds_ref[cb, :, hd:2 * hd]
        return sf, sb

    zero = jnp.zeros((hd, hd), F32)
    lax.fori_loop(0, n_chunks, scan_b, (zero, zero), unroll=2)

    def inter(cis):
        os_ = [lax.dot_general(qd_ref[ci * c:(ci + 1) * c, :], st_ref[ci], nt, preferred_element_type=F32)
               for ci in cis]
        for ci, o in zip(cis, os_):
            oi_ref[ci * c:(ci + 1) * c, :] += o

    gn = ng_ref[...].astype(F32)

    def head_norm(r0):
        o = oi_ref[r0:r0 + HG_NORM_ROWS, :]
        o = o * lax.rsqrt(jnp.mean(o * o, axis=-1, keepdims=True) + NORM_EPS) * gn
        o_ref[0, r0:r0 + HG_NORM_ROWS, :] = o.astype(o_ref.dtype)

    group_rows = HG_UNROLL_INTER * c
    n_groups = n_chunks // HG_UNROLL_INTER
    for gi in range(n_groups + 1):
        if gi < n_groups:
            inter(range(gi * HG_UNROLL_INTER, (gi + 1) * HG_UNROLL_INTER))
        if gi >= 1:
            for r0 in range((gi - 1) * group_rows, gi * group_rows, HG_NORM_ROWS):
                head_norm(r0)


def _hgrn2(pz, hg_lb, norm_g, layer, n_ctx, e):
    bsz, t, _ = pz.shape
    n_h = e // HG_HEAD
    n_chunks = t // HG_CHUNK
    depth = hg_lb.shape[0]
    hd = HG_HEAD
    assert n_chunks % HG_PAIR == 0 and n_chunks // HG_PAIR >= 3 and n_chunks % HG_UNROLL_INTER == 0
    assert t % HG_NORM_ROWS == 0

    return pl.pallas_call(
        functools.partial(_hgrn2_kernel, layer=layer, n_chunks=n_chunks, nc_ctx=n_ctx // HG_CHUNK, n_tok=t),
        grid=(bsz, n_h),
        in_specs=[pl.BlockSpec((1, t, 4 * hd), lambda b, h: (b, 0, e // (4 * hd) + h)),
                  pl.BlockSpec((depth, hd), lambda b, h: (0, h)),
                  pl.BlockSpec((1, hd), lambda b, h: (0, h))],
        out_specs=pl.BlockSpec((1, t, hd), lambda b, h: (b, 0, h)),
        out_shape=jax.ShapeDtypeStruct((bsz, t, e), BF16),
        scratch_shapes=[pltpu.VMEM((2, t, hd), F32),
                        pltpu.VMEM((2, t, hd), F32),
                        pltpu.VMEM((2, t, hd), BF16),
                        pltpu.VMEM((2, t, hd), BF16),
                        pltpu.VMEM((t, 2 * hd), BF16),
                        pltpu.VMEM((t, 2 * hd), BF16),
                        pltpu.VMEM((n_chunks, hd, 2 * hd), F32),
                        pltpu.VMEM((n_chunks, 1, 2 * hd), F32),
                        pltpu.VMEM((n_chunks, hd, 2 * hd), BF16),
                        pltpu.VMEM((t, hd), F32)],
        compiler_params=_cparams(2),
    )(pz, hg_lb.astype(F32), norm_g.astype(F32).reshape(1, e))


def _head_major(w, e):
    d = w.shape[0]
    n_h = e // HG_HEAD
    rest = w[:, e:].reshape(d, 4, n_h, HG_HEAD).transpose(0, 2, 1, 3).reshape(d, 4 * e)
    return jnp.concatenate([w[:, :e], rest], axis=1)


def _z_first(w, e):
    return jnp.concatenate([w[..., -e:], w[..., :-e]], axis=-1)


def kernel(x, c, ctx, c_ctx, ada_w, ada_b, ln_g, ln_b, w_out, attn_w_in, attn_sink, s5_w_in, s5_lam_re, s5_lam_im, s5_log_step, s5_b_re, s5_b_im, s5_c_re, s5_c_im, s5_d, s5_glu_w, s5_glu_b, hg_w_in, hg_lb, hg_norm_g):
    bsz, n_lat, d = x.shape
    n_ctx = ctx.shape[1]
    depth = ada_w.shape[0]
    e = w_out.shape[1]
    alpha = (2.0 * depth) ** 0.25
    t = n_ctx + n_lat
    tm = 768 if t % 768 == 0 else 256

    mod_rows = -(-(bsz + 1) // MOD_ROWS_PAD) * MOD_ROWS_PAD
    c_all = jnp.concatenate([c.astype(F32), c_ctx.astype(F32)[None],
                             jnp.zeros((mod_rows - bsz - 1, d), F32)], axis=0)
    mod = _ada(c_all, ada_w.astype(F32), ada_b.astype(F32))
    mod3 = mod.reshape(depth * mod_rows, 1, 3 * d)

    xs = (ctx.astype(F32), x.astype(F32))
    for i in range(depth):
        kind, j = i % N_MIXERS, i // N_MIXERS
        w_in = _z_first((attn_w_in, s5_w_in, hg_w_in)[kind][j].astype(BF16), e)
        w_o = w_out[i].astype(BF16)
        if kind == 1:
            w_in = jnp.concatenate([_cg_order(w_in[:, :e], 1), _cg_order(w_in[:, e:], 1)], axis=1)
            w_o = _cg_order(w_o, 0)
        elif kind == 2:
            w_in = _head_major(w_in, e)
        pz = _inproj(xs, mod3, i, mod_rows, w_in.astype(BF16), n_ctx, tn=INPROJ_TN)
        last = i == depth - 1
        if kind == 0:
            y = _attention(pz, attn_sink[j], n_ctx, e, ctx_out=not last)
        elif kind == 1:
            ops = _s5_operators(s5_lam_re[j], s5_lam_im[j], s5_log_step[j], s5_b_re[j], s5_b_im[j],
                                s5_c_re[j], s5_c_im[j], s5_d[j])
            glu_w = _cg_order(_cg_order(s5_glu_w[j].astype(BF16), 0), 1)
            y = _s5_mixer(pz, n_ctx, e, ops, glu_w, _cg_order(s5_glu_b[j], 0))
        else:
            y = _hgrn2(pz, hg_lb, hg_norm_g[j], i, n_ctx, e)
        if last and kind == 0:
            return _outproj(y, pz, xs, mod3, i, mod_rows, w_o.astype(BF16), ln_g[i].astype(F32),
                            ln_b[i].astype(F32), n_ctx, alpha, LAST_TM, latent_only=True).astype(x.dtype)
        xs = (_outproj(y, pz, xs, mod3, i, mod_rows, w_o.astype(BF16), ln_g[i].astype(F32),
                       ln_b[i].astype(F32), n_ctx, alpha, tm),)
    return xs[0][:, n_ctx:].astype(x.dtype)
```

```python
import functools
import math

import jax
import jax.numpy as jnp
from jax import lax
from jax.experimental import pallas as pl
from jax.experimental.pallas import tpu as pltpu

F32 = jnp.float32
BF16 = jnp.bfloat16

N_MIXERS = 3
HEAD_DIM = 128
Q_PER_KV = 4
WINDOW = 128
ATTN_BLOCK = 128
GRID_W = 64
ROPE_THETA = 10000.0
S5_GROUP = 16
S5_STATE = 64
S5_CHUNK = 16
HG_HEAD = 128
HG_CHUNK = 64
HG_UNROLL_INTER = 12
HG_PAIR = 4
HG_NORM_ROWS = 256
NORM_EPS = 1e-5
NEG_INF = -1e30
LOG2E = math.log2(math.e)
ATTN_ROWS = 32
MOD_ROWS_PAD = 8
INPROJ_TN = 1024
LAST_TM = 256
LN_ROWS = 4
S5_A_ROWS = 16
GLU_PIECES = 8
VMEM_LIMIT = 56 * 1024 * 1024


def _cparams(n_axes):
    return pltpu.CompilerParams(dimension_semantics=("arbitrary",) * n_axes,
                                vmem_limit_bytes=VMEM_LIMIT)


def _sigmoid(x):
    return 1.0 / (1.0 + jnp.exp(-x))


def _ada_kernel(c_ref, w_ref, b_ref, o_ref):
    cv = c_ref[...]
    s = cv * _sigmoid(cv)
    o_ref[0] = jnp.dot(s, w_ref[0], preferred_element_type=F32,
                       precision=lax.Precision.HIGHEST) + b_ref[0]


def _ada(c_all, ada_w, ada_b):
    depth, d, d3 = ada_w.shape
    rows = c_all.shape[0]
    return pl.pallas_call(
        _ada_kernel,
        grid=(depth, d3 // d),
        in_specs=[pl.BlockSpec((rows, d), lambda i, j: (0, 0)),
                  pl.BlockSpec((1, d, d), lambda i, j: (i, 0, j)),
                  pl.BlockSpec((1, 1, d), lambda i, j: (i, 0, j))],
        out_specs=pl.BlockSpec((1, rows, d), lambda i, j: (i, 0, j)),
        out_shape=jax.ShapeDtypeStruct((depth, rows, d3), F32),
        compiler_params=_cparams(2),
    )(c_all, ada_w, ada_b.reshape(depth, 1, d3))


def _inproj_kernel(*refs, n_ctx, d, split):
    if split:
        xc_ref, xl_ref, mb_ref, mc_ref, w_ref, o_ref, h_ref = refs
    else:
        x_ref, mb_ref, mc_ref, w_ref, o_ref, h_ref = refs

    @pl.when(pl.program_id(1) == 0)
    def _():
        mb = mb_ref[0]
        mc = mc_ref[0]
        xc = xc_ref[0] if split else x_ref[0, :n_ctx]
        xl = xl_ref[0] if split else x_ref[0, n_ctx:]
        h_ref[:n_ctx] = (xc * (1.0 + mc[:, d:2 * d]) + mc[:, :d]).astype(BF16)
        h_ref[n_ctx:] = (xl * (1.0 + mb[:, d:2 * d]) + mb[:, :d]).astype(BF16)

    o_ref[0] = jnp.dot(h_ref[...], w_ref[...], preferred_element_type=F32).astype(o_ref.dtype)


def _inproj(xs, mod3, layer, mod_rows, w_bf, n_ctx, tn):
    split = len(xs) == 2
    bsz, _, d = xs[0].shape
    t = sum(a.shape[1] for a in xs)
    n = w_bf.shape[1]
    base = layer * mod_rows
    return pl.pallas_call(
        functools.partial(_inproj_kernel, n_ctx=n_ctx, d=d, split=split),
        grid=(bsz, n // tn),
        in_specs=[pl.BlockSpec((1, a.shape[1], d), lambda b, j: (b, 0, 0)) for a in xs] + [
                  pl.BlockSpec((1, 1, mod3.shape[2]), lambda b, j: (base + b, 0, 0)),
                  pl.BlockSpec((1, 1, mod3.shape[2]), lambda b, j: (base + bsz, 0, 0)),
                  pl.BlockSpec((d, tn), lambda b, j: (0, j))],
        out_specs=pl.BlockSpec((1, t, tn), lambda b, j: (b, 0, j)),
        out_shape=jax.ShapeDtypeStruct((bsz, t, n), BF16),
        scratch_shapes=[pltpu.VMEM((t, d), BF16)],
        compiler_params=_cparams(2),
    )(*xs, mod3, mod3, w_bf)


def _outproj_kernel(y_ref, z_ref, *refs, n_ctx, d, alpha, tm, first_tile, n_pieces):
    if n_pieces:
        x_refs, (c_ref, mb_ref, mc_ref, w_ref, ln_ref, o_ref) = refs[:n_pieces], refs[n_pieces:]
        first = jnp.where(pl.program_id(1) == 0, c_ref[0], x_refs[0][0])
        x_tile = jnp.concatenate([first] + [r[0] for r in x_refs[1:]], axis=0)
    else:
        x_ref, mb_ref, mc_ref, w_ref, ln_ref, o_ref = refs
        x_tile = x_ref[0]
    z = z_ref[0]
    a = y_ref[0] * (z * _sigmoid(z))
    br = jnp.dot(a, w_ref[...], preferred_element_type=F32)
    rows = (pl.program_id(1) + first_tile) * tm + lax.broadcasted_iota(jnp.int32, (tm, 1), 0)
    gate = jnp.where(rows < n_ctx, mc_ref[0][:, 2 * d:3 * d], mb_ref[0][:, 2 * d:3 * d])
    v = alpha * x_tile + gate * br
    mu = jnp.mean(v, axis=-1, keepdims=True)
    vc = v - mu
    var = jnp.mean(vc * vc, axis=-1, keepdims=True)
    o_ref[0] = vc * lax.rsqrt(var + NORM_EPS) * ln_ref[0:1, :] + ln_ref[1:2, :]


def _outproj(y, pz, xs, mod3, layer, mod_rows, w_bf, ln_g, ln_b, n_ctx, alpha, tm, latent_only=False):
    split = len(xs) == 2
    bsz, _, d = xs[0].shape
    t = sum(a.shape[1] for a in xs)
    e = w_bf.shape[0]
    base = layer * mod_rows
    off = n_ctx // tm if latent_only else 0
    assert n_ctx % tm == 0 or not latent_only
    ln = jnp.zeros((LN_ROWS, d), F32).at[0].set(ln_g).at[1].set(ln_b)
    n_pieces = tm // n_ctx if split else 0
    assert not split or (tm % n_ctx == 0 and not latent_only)
    if split:
        x_specs = [pl.BlockSpec((1, n_ctx, d), lambda b, i, p=p: (b, jnp.maximum(i * n_pieces + p - 1, 0), 0))
                   for p in range(n_pieces)] + [pl.BlockSpec((1, n_ctx, d), lambda b, i: (b, 0, 0))]
        x_args = [xs[1]] * n_pieces + [xs[0]]
    else:
        x_specs = [pl.BlockSpec((1, tm, d), lambda b, i: (b, i + off, 0))]
        x_args = [xs[0]]
    return pl.pallas_call(
        functools.partial(_outproj_kernel, n_ctx=n_ctx, d=d, alpha=alpha, tm=tm, first_tile=off,
                          n_pieces=n_pieces),
        grid=(bsz, t // tm - off),
        in_specs=[pl.BlockSpec((1, tm, e), lambda b, i: (b, i, 0)),
                  pl.BlockSpec((1, tm, e), lambda b, i: (b, i + off, 0))] + x_specs + [
                  pl.BlockSpec((1, 1, mod3.shape[2]), lambda b, i: (base + b, 0, 0)),
                  pl.BlockSpec((1, 1, mod3.shape[2]), lambda b, i: (base + bsz, 0, 0)),
                  pl.BlockSpec((e, d), lambda b, i: (0, 0)),
                  pl.BlockSpec((LN_ROWS, d), lambda b, i: (0, 0))],
        out_specs=pl.BlockSpec((1, tm, d), lambda b, i: (b, i, 0)),
        out_shape=jax.ShapeDtypeStruct((bsz, t - off * tm, d), F32),
        compiler_params=_cparams(2),
    )(y, pz, *x_args, mod3, mod3, w_bf, ln)


def _rope_tables(n_lat):
    pos = jnp.arange(n_lat, dtype=jnp.int32)
    row = (pos // GRID_W).astype(F32)
    col = (pos % GRID_W).astype(F32)
    nf = HEAD_DIM // 4
    inv_freq = jnp.power(ROPE_THETA, -jnp.arange(nf, dtype=F32) / nf)
    ang_r = row[:, None] * inv_freq[None, :]
    ang_c = col[:, None] * inv_freq[None, :]
    zeros = jnp.zeros_like(ang_r)
    cos = jnp.concatenate([jnp.cos(ang_r)] * 2 + [jnp.cos(ang_c)] * 2, axis=-1)
    s1 = jnp.concatenate([-jnp.sin(ang_r), zeros, -jnp.sin(ang_c), zeros], axis=-1)
    s2 = jnp.concatenate([zeros, jnp.sin(ang_r), zeros, jnp.sin(ang_c)], axis=-1)
    return cos, s1, s2


def _rope(x, cos, s1, s2):
    quarter = HEAD_DIM // 4
    return (x * cos + pltpu.roll(x, HEAD_DIM - quarter, 1) * s1
            + pltpu.roll(x, quarter, 1) * s2)


def _attn_kernel(sink_ref, q_ref, kv_ref, cq_ref, s1q_ref, s2q_ref, ck_ref, s1k_ref, s2k_ref,
                 o_ref, kr_ref, vx_ref, s_ref, p_ref, e_ref, bias_ref, *, n_ctx, n_lat, n_kv, first_block):
    j = pl.program_id(1) + first_block
    nb_ctx = n_ctx // ATTN_BLOCK
    kvw = n_kv * HEAD_DIM
    n_win = ATTN_BLOCK + 2 * WINDOW
    qscale = HEAD_DIM ** -0.5 * LOG2E
    rows_q = Q_PER_KV * ATTN_BLOCK
    vxw = 2 * HEAD_DIM
    nt = (((1,), (1,)), ((), ()))

    @pl.when(pl.program_id(1) == 0)
    def _():
        for h in range(n_kv):
            sl = slice(h * HEAD_DIM, (h + 1) * HEAD_DIM)
            k = kv_ref[0, n_ctx:, sl].astype(F32)
            kr_ref[:, sl] = _rope(k, ck_ref[...], s1k_ref[...], s2k_ref[...]).astype(BF16)
            vx_ref[:, h * vxw:h * vxw + HEAD_DIM] = kv_ref[0, :, kvw + h * HEAD_DIM:kvw + (h + 1) * HEAD_DIM]
            vx_ref[:, h * vxw + HEAD_DIM:(h + 1) * vxw] = jnp.ones((n_ctx + n_lat, HEAD_DIM), BF16)

    def key_cols(kparts):
        cols, off = [], 0
        for kp in kparts:
            cols.append((off, kp.shape[0]))
            off += kp.shape[0]
        return cols

    def logit_tasks(h, q_fn, kparts):
        box = {}

        def task(kp, o_, n):
            def run():
                if "q" not in box:
                    box["q"] = q_fn(h)
                s_ref[h % 2, :, o_:o_ + n] = lax.dot_general(box["q"], kp, nt, preferred_element_type=F32)
            return run

        return [task(kp, o_, n) for kp, (o_, n) in zip(kparts, key_cols(kparts))]

    def softmax_rows(h, cols, use_bias, blocks):
        for rb in blocks:
            r0 = rb * ATTN_ROWS
            sk = sink_ref[h * Q_PER_KV + r0 // ATTN_BLOCK] * LOG2E
            parts = []
            for idx, (o_, n) in enumerate(cols):
                s = s_ref[h % 2, r0:r0 + ATTN_ROWS, o_:o_ + n]
                if use_bias and idx == 0:
                    ql0 = r0 % ATTN_BLOCK
                    s = s + bias_ref[ql0:ql0 + ATTN_ROWS, :]
                parts.append(s)
            m = jnp.max(parts[0], axis=-1, keepdims=True)
            for s in parts[1:]:
                m = jnp.maximum(m, jnp.max(s, axis=-1, keepdims=True))
            m = jnp.maximum(m, sk)
            for (o_, n), s in zip(cols, parts):
                p_ref[h % 2, r0:r0 + ATTN_ROWS, o_:o_ + n] = jnp.exp2(s - m).astype(BF16)
            e_ref[h % 2, r0:r0 + ATTN_ROWS, :] = jnp.exp2(sk - m)

    def value_tasks(h, cols, vparts):
        box = {}

        def task(idx, o_, n, vp):
            def run():
                part = jnp.dot(p_ref[h % 2, :, o_:o_ + n], vp, preferred_element_type=F32)
                box["ox"] = part if idx == 0 else box["ox"] + part
                if idx == len(cols) - 1:
                    ox = box["ox"]
                    l = ox[:, HEAD_DIM:HEAD_DIM + 1] + e_ref[h % 2]
                    o = ox[:, :HEAD_DIM] * (1.0 / l)
                    for g in range(Q_PER_KV):
                        hq = h * Q_PER_KV + g
                        o_ref[0, :, hq * HEAD_DIM:(hq + 1) * HEAD_DIM] = (
                            o[g * ATTN_BLOCK:(g + 1) * ATTN_BLOCK].astype(o_ref.dtype))
            return run

        return [task(i, o_, n, vp) for i, ((o_, n), vp) in enumerate(zip(cols, vparts))]

    def attend_all(q_fn, k_fn, v_fn, use_bias):
        n_blocks = rows_q // ATTN_ROWS
        quarter = n_blocks // 4
        cols = key_cols(k_fn(0))
        for task in logit_tasks(0, q_fn, k_fn(0)):
            task()
        for h in range(n_kv + 1):
            side = []
            if h + 1 < n_kv:
                side += logit_tasks(h + 1, q_fn, k_fn(h + 1))
            if 1 <= h:
                side += value_tasks(h - 1, cols, v_fn(h - 1))
            for qi in range(4):
                if h < n_kv:
                    softmax_rows(h, cols, use_bias, range(qi * quarter, (qi + 1) * quarter))
                for task in side[qi::4]:
                    task()

    def q_head(hq):
        return q_ref[0, :, hq * HEAD_DIM:(hq + 1) * HEAD_DIM].astype(F32)

    def head_slice(h):
        return slice(h * HEAD_DIM, (h + 1) * HEAD_DIM)

    @pl.when(j < nb_ctx)
    def _():
        attend_all(
            lambda h: jnp.concatenate([(q_head(h * Q_PER_KV + g) * qscale).astype(BF16)
                                       for g in range(Q_PER_KV)], axis=0),
            lambda h: [kv_ref[0, :n_ctx, head_slice(h)]],
            lambda h: [vx_ref[:n_ctx, h * vxw:(h + 1) * vxw]], False)

    @pl.when(j >= nb_ctx)
    def _():
        jl = j - nb_ctx
        start = jnp.clip((jl - 1) * ATTN_BLOCK, 0, n_lat - n_win)
        start = pl.multiple_of(start, ATTN_BLOCK)
        delta = jl * ATTN_BLOCK - start
        ql = lax.broadcasted_iota(jnp.int32, (ATTN_BLOCK, n_win), 0)
        kl = lax.broadcasted_iota(jnp.int32, (ATTN_BLOCK, n_win), 1)
        bias_ref[...] = jnp.where(jnp.abs(delta + ql - kl) <= WINDOW, 0.0, NEG_INF)
        cq = cq_ref[...] * qscale
        s1q = s1q_ref[...] * qscale
        s2q = s2q_ref[...] * qscale
        attend_all(
            lambda h: jnp.concatenate([_rope(q_head(h * Q_PER_KV + g), cq, s1q, s2q).astype(BF16)
                                       for g in range(Q_PER_KV)], axis=0),
            lambda h: [kr_ref[pl.ds(start, n_win), head_slice(h)], kv_ref[0, :n_ctx, head_slice(h)]],
            lambda h: [vx_ref[pl.ds(n_ctx + start, n_win), h * vxw:(h + 1) * vxw],
                       vx_ref[:n_ctx, h * vxw:(h + 1) * vxw]], True)


def _attention(pz, sink, n_ctx, e, ctx_out):
    bsz, t, _ = pz.shape
    n_lat = t - n_ctx
    n_q = e // HEAD_DIM
    n_kv = n_q // Q_PER_KV
    kvw = n_kv * HEAD_DIM
    n_win = ATTN_BLOCK + 2 * WINDOW
    rows_q = Q_PER_KV * ATTN_BLOCK
    assert (2 * e) % (2 * kvw) == 0 and n_lat >= n_win
    cos, s1, s2 = _rope_tables(n_lat)
    nb_ctx = n_ctx // ATTN_BLOCK
    first = 0 if ctx_out else nb_ctx
    qtab = pl.BlockSpec((ATTN_BLOCK, HEAD_DIM), lambda b, j: (jnp.maximum(j + first - nb_ctx, 0), 0))
    ktab = pl.BlockSpec((n_lat, HEAD_DIM), lambda b, j: (0, 0))
    return pl.pallas_call(
        functools.partial(_attn_kernel, n_ctx=n_ctx, n_lat=n_lat, n_kv=n_kv, first_block=first),
        grid=(bsz, t // ATTN_BLOCK - first),
        in_specs=[pl.BlockSpec(memory_space=pltpu.SMEM),
                  pl.BlockSpec((1, ATTN_BLOCK, e), lambda b, j: (b, j + first, 1)),
                  pl.BlockSpec((1, t, 2 * kvw), lambda b, j: (b, 0, 2 * e // (2 * kvw))),
                  qtab, qtab, qtab, ktab, ktab, ktab],
        out_specs=pl.BlockSpec((1, ATTN_BLOCK, e), lambda b, j: (b, j, 0)),
        out_shape=jax.ShapeDtypeStruct((bsz, t - first * ATTN_BLOCK, e), BF16),
        scratch_shapes=[pltpu.VMEM((n_lat, kvw), BF16),
                        pltpu.VMEM((t, n_kv * 2 * HEAD_DIM), BF16),
                        pltpu.VMEM((2, rows_q, n_win + n_ctx), F32),
                        pltpu.VMEM((2, rows_q, n_win + n_ctx), BF16),
                        pltpu.VMEM((2, rows_q, 1), F32),
                        pltpu.VMEM((ATTN_BLOCK, n_win), F32)],
        compiler_params=_cparams(2),
    )(sink.astype(F32), pz, pz, cos, s1, s2, cos, s1, s2)


def _cg_order(w, axis):
    axis = axis % w.ndim
    n = w.shape[axis]
    shp = w.shape[:axis] + (n // S5_GROUP, S5_GROUP) + w.shape[axis + 1:]
    return jnp.swapaxes(w.reshape(shp), axis, axis + 1).reshape(w.shape)


def _s5_operators(lam_re, lam_im, log_step, b_re, b_im, c_re, c_im, d_skip):
    n_g = lam_re.shape[1]
    tc = S5_CHUNK
    dt = jnp.exp(log_step.astype(F32))[..., None]
    lr, li = lam_re.astype(F32) * dt, lam_im.astype(F32) * dt
    k = jnp.arange(tc + 1, dtype=F32)[:, None, None, None]
    mag = jnp.exp(lr[None] * k)
    pr, pi = mag * jnp.cos(li[None] * k), mag * jnp.sin(li[None] * k)
    ar1, ai1 = pr[1] - 1.0, pi[1]
    den = lam_re.astype(F32) ** 2 + lam_im.astype(F32) ** 2
    cr = (ar1 * lam_re + ai1 * lam_im) / den
    ci = (ai1 * lam_re - ar1 * lam_im) / den
    br, bi = b_re.astype(F32), b_im.astype(F32)
    bbr = cr[..., None] * br - ci[..., None] * bi
    bbi = cr[..., None] * bi + ci[..., None] * br
    ccr, cci = c_re.astype(F32), c_im.astype(F32)

    def kern(d):
        xr = pr[:tc, d][..., None] * bbr[d][None] - pi[:tc, d][..., None] * bbi[d][None]
        xi = pr[:tc, d][..., None] * bbi[d][None] + pi[:tc, d][..., None] * bbr[d][None]
        return (jnp.einsum('gnp,tgpm->tgnm', ccr[d], xr) - jnp.einsum('gnp,tgpm->tgnm', cci[d], xi))

    kf, kb = kern(0), kern(1)
    s_idx = jnp.arange(tc)[:, None]
    t_idx = jnp.arange(tc)[None, :]
    lag = t_idx - s_idx
    mf = jnp.where((lag >= 0)[:, :, None, None, None], kf[jnp.clip(lag, 0, tc - 1)], 0.0)
    mb = jnp.where((lag <= 0)[:, :, None, None, None], kb[jnp.clip(-lag, 0, tc - 1)], 0.0)
    m = (mf + mb).transpose(2, 4, 0, 3, 1)
    eye = (jnp.eye(S5_GROUP, dtype=F32)[:, None, :, None] * jnp.eye(tc, dtype=F32)[None, :, None, :])
    m = m + eye[None] * d_skip.astype(F32).reshape(n_g, S5_GROUP, 1, 1, 1)
    m = m.reshape(n_g, tc * S5_GROUP, tc * S5_GROUP)

    def qpart(d, pw_r, pw_i):
        qr = pw_r[..., None] * bbr[d][None] - pw_i[..., None] * bbi[d][None]
        qi = pw_r[..., None] * bbi[d][None] + pw_i[..., None] * bbr[d][None]
        return qr.transpose(1, 3, 0, 2), qi.transpose(1, 3, 0, 2)

    qfr, qfi = qpart(0, pr[tc - 1 - jnp.arange(tc), 0], pi[tc - 1 - jnp.arange(tc), 0])
    qbr, qbi = qpart(1, pr[jnp.arange(tc), 1], pi[jnp.arange(tc), 1])
    q = jnp.concatenate([qfr, qbr, qfi, qbi], axis=-1).reshape(n_g, tc * S5_GROUP, 4 * S5_STATE)

    def ppart(d, pw_r, pw_i):
        xr = ccr[d][None] * pw_r[:, :, None, :] - cci[d][None] * pw_i[:, :, None, :]
        xi = ccr[d][None] * pw_i[:, :, None, :] + cci[d][None] * pw_r[:, :, None, :]
        return xr.transpose(1, 3, 2, 0), (-xi).transpose(1, 3, 2, 0)

    pfr, pfi = ppart(0, pr[1 + jnp.arange(tc), 0], pi[1 + jnp.arange(tc), 0])
    pbr, pbi = ppart(1, pr[tc - jnp.arange(tc), 1], pi[tc - jnp.arange(tc), 1])
    p = jnp.concatenate([pfr, pbr, pfi, pbi], axis=1).reshape(n_g, 4 * S5_STATE, tc * S5_GROUP)
    a = jnp.stack([jnp.concatenate([pr[tc, 0], pr[tc, 1]], axis=-1),
                   jnp.concatenate([pi[tc, 0], pi[tc, 1]], axis=-1)], axis=1)
    a = jnp.pad(a, ((0, 0), (0, S5_A_ROWS - 2), (0, 0)))
    return m.astype(BF16), q.astype(BF16), p.astype(BF16), a


def _s5_pack_kernel(x_ref, o_ref, f_ref, *, nb, n_g):
    half = S5_GROUP // 2
    for b in range(nb):
        xb = x_ref[b].astype(F32)
        for k in range(2):
            a = jnp.concatenate([xb[:, cc * n_g:(cc + 1) * n_g]
                                 for cc in range(k * half, (k + 1) * half)], axis=0)
            f_ref[k, pl.ds(b, n_g, stride=nb), :] = a.T
    full = jnp.concatenate([f_ref[0], f_ref[1]], axis=1)
    o_ref[...] = full.reshape(n_g, nb, S5_CHUNK * S5_GROUP).astype(o_ref.dtype)


def _s5_pack(pz, e):
    bsz, t, _ = pz.shape
    n_g = e // S5_GROUP
    n_chunks = t // S5_CHUNK
    w = S5_CHUNK * S5_GROUP
    return pl.pallas_call(
        functools.partial(_s5_pack_kernel, nb=bsz, n_g=n_g),
        grid=(n_chunks,),
        in_specs=[pl.BlockSpec((bsz, S5_CHUNK, e), lambda i: (0, i, 1))],
        out_specs=pl.BlockSpec((n_g, bsz, w), lambda i: (0, i, 0)),
        out_shape=jax.ShapeDtypeStruct((n_g, n_chunks * bsz, w), BF16),
        scratch_shapes=[pltpu.VMEM((2, n_g * bsz, w // 2), F32)],
        compiler_params=_cparams(1),
    )(pz)


def _s5_kernel(u_ref, m_ref, q_ref, p_ref, a_ref, y_ref, s_ref, h_ref, *, nb, n_chunks, nc_ctx):
    ns = S5_STATE
    u = u_ref[0]
    s_ref[...] = jnp.dot(u, q_ref[0], preferred_element_type=F32)
    ar = a_ref[0, 0:1, :]
    ai = a_ref[0, 1:2, :]
    is_fwd = lax.broadcasted_iota(jnp.int32, (nb, 2 * ns), 1) < ns

    def step(i, carry):
        hr, hi = carry
        cb = jnp.where(i < nc_ctx, nc_ctx - 1 - i, n_chunks - 1 - (i - nc_ctx))
        rf = pl.multiple_of(i * nb, nb)
        rb = pl.multiple_of(cb * nb, nb)
        h_ref[pl.ds(rf, nb), 0:ns] = hr[:, 0:ns]
        h_ref[pl.ds(rb, nb), ns:2 * ns] = hr[:, ns:2 * ns]
        h_ref[pl.ds(rf, nb), 2 * ns:3 * ns] = hi[:, 0:ns]
        h_ref[pl.ds(rb, nb), 3 * ns:4 * ns] = hi[:, ns:2 * ns]
        sf = s_ref[pl.ds(rf, nb), :]
        sb = s_ref[pl.ds(rb, nb), :]
        sr = jnp.where(is_fwd, sf[:, :2 * ns], sb[:, :2 * ns])
        si = jnp.where(is_fwd, sf[:, 2 * ns:], sb[:, 2 * ns:])
        return ar * hr - ai * hi + sr, ar * hi + ai * hr + si

    zero = jnp.zeros((nb, 2 * ns), F32)
    lax.fori_loop(0, n_chunks, step, (zero, zero), unroll=2)
    y = jnp.dot(u, m_ref[0], preferred_element_type=F32)
    y = y + jnp.dot(h_ref[...].astype(BF16), p_ref[0], preferred_element_type=F32)
    y_ref[0] = y.astype(y_ref.dtype)


def _s5_scan(uf, m, q, p, a, nb, nc_ctx):
    n_g, rows, w = uf.shape
    n_chunks = rows // nb
    wspec = pl.BlockSpec((1, w, w), lambda g: (g, 0, 0))
    return pl.pallas_call(
        functools.partial(_s5_kernel, nb=nb, n_chunks=n_chunks, nc_ctx=nc_ctx),
        grid=(n_g,),
        in_specs=[pl.BlockSpec((1, rows, w), lambda g: (g, 0, 0)), wspec, wspec, wspec,
                  pl.BlockSpec((1, S5_A_ROWS, 2 * S5_STATE), lambda g: (g, 0, 0))],
        out_specs=pl.BlockSpec((1, rows, w), lambda g: (g, 0, 0)),
        out_shape=jax.ShapeDtypeStruct((n_g, rows, w), BF16),
        scratch_shapes=[pltpu.VMEM((rows, w), F32), pltpu.VMEM((rows, w), F32)],
        compiler_params=_cparams(1),
    )(uf, m, q, p, a)


def _s5_glu_kernel(y0_ref, yn_ref, w_ref, b_ref, o_ref, f_ref, g_ref, *, nb, n_g):
    i = pl.program_id(0)
    w = S5_CHUNK * S5_GROUP
    half = S5_GROUP // 2
    e = n_g * S5_GROUP

    def widen(y_ref):
        yv = y_ref[...].astype(F32).reshape(n_g * nb, w)
        for k in range(2):
            f_ref[k] = yv[:, k * (w // 2):(k + 1) * (w // 2)]

    def build(slot, batches):
        for b in batches:
            for k in range(2):
                t = f_ref[k, pl.ds(b, n_g, stride=nb), :].T
                g = 0.5 * t * (1.0 + jnp.tanh(math.sqrt(2.0 / math.pi) * (t + 0.044715 * t * t * t)))
                g = g.astype(BF16)
                for cc in range(half):
                    ch = k * half + cc
                    g_ref[slot, b * S5_CHUNK:(b + 1) * S5_CHUNK, ch * n_g:(ch + 1) * n_g] = (
                        g[cc * S5_CHUNK:(cc + 1) * S5_CHUNK, :])

    @pl.when(i == 0)
    def _():
        widen(y0_ref)
        build(0, range(nb))

    cur = i % 2
    nxt = 1 - cur
    n_pieces = GLU_PIECES
    tn = e // n_pieces
    per = nb // n_pieces
    widen(yn_ref)
    for p in range(n_pieces):
        cols = slice(p * tn, (p + 1) * tn)
        g = g_ref[cur]
        lin = jnp.dot(g, w_ref[:, cols], preferred_element_type=F32) + b_ref[0:1, cols]
        out = (g[:, cols].astype(F32) * _sigmoid(lin)).astype(o_ref.dtype)
        o_ref[:, :, cols] = out.reshape(nb, S5_CHUNK, tn)
        build(nxt, range(p * per, (p + 1) * per))


def _s5_glu(yf, w_bf, bias, bsz):
    n_g, rows, w = yf.shape
    e = n_g * S5_GROUP
    n_chunks = rows // bsz
    assert bsz % GLU_PIECES == 0 and e % GLU_PIECES == 0
    return pl.pallas_call(
        functools.partial(_s5_glu_kernel, nb=bsz, n_g=n_g),
        grid=(n_chunks,),
        in_specs=[pl.BlockSpec((n_g, bsz, w), lambda i: (0, 0, 0)),
                  pl.BlockSpec((n_g, bsz, w), lambda i: (0, jnp.minimum(i + 1, n_chunks - 1), 0)),
                  pl.BlockSpec((e, e), lambda i: (0, 0)),
                  pl.BlockSpec((2, e), lambda i: (0, 0))],
        out_specs=pl.BlockSpec((bsz, S5_CHUNK, e), lambda i: (0, i, 0)),
        out_shape=jax.ShapeDtypeStruct((bsz, n_chunks * S5_CHUNK, e), BF16),
        scratch_shapes=[pltpu.VMEM((2, n_g * bsz, w // 2), F32),
                        pltpu.VMEM((2, bsz * S5_CHUNK, e), BF16)],
        compiler_params=_cparams(1),
    )(yf, yf, w_bf, jnp.zeros((2, e), F32).at[0].set(bias.astype(F32)))


def _s5_mixer(pz, n_ctx, e, ops, glu_w_bf, glu_b):
    bsz = pz.shape[0]
    m, q, p, a = ops
    uf = _s5_pack(pz, e)
    yf = _s5_scan(uf, m, q, p, a, bsz, n_ctx // S5_CHUNK)
    return _s5_glu(yf, glu_w_bf, glu_b, bsz)


def _cumsum_rows_multi(xs, reverse):
    n = xs[0].shape[0]
    sub = 8
    n_slabs = n // sub
    row = lax.broadcasted_iota(jnp.int32, (sub, xs[0].shape[1]), 0)
    slabs = [[x[i * sub:(i + 1) * sub] for i in range(n_slabs)] for x in xs]
    s = 1
    while s < sub:
        nxt = []
        for arr, rev in zip(slabs, reverse):
            if rev:
                nxt.append([sl + jnp.where(row < sub - s, pltpu.roll(sl, sub - s, 0), 0.0) for sl in arr])
            else:
                nxt.append([sl + jnp.where(row >= s, pltpu.roll(sl, s, 0), 0.0) for sl in arr])
        slabs = nxt
        s *= 2
    out = []
    for arr, rev in zip(slabs, reverse):
        order = range(n_slabs - 1, -1, -1) if rev else range(n_slabs)
        edge = 0 if rev else sub - 1
        done = [None] * n_slabs
        carry = None
        for i in order:
            sl = arr[i] if carry is None else arr[i] + carry
            done[i] = sl
            carry = jnp.broadcast_to(sl[edge:edge + 1], sl.shape)
        out.append(jnp.concatenate(done, axis=0))
    return out


def _hgrn2_kernel(x_ref, lb_ref, ng_ref, o_ref,
                  g_ref, k_ref, qe_ref, ke_ref, qd_ref, kd_ref, ds_ref, dec_ref, st_ref, oi_ref,
                  *, layer, n_chunks, nc_ctx, n_tok):
    c = HG_CHUNK
    hd = HG_HEAD
    nt = (((1,), (1,)), ((), ()))
    tn = (((0,), (0,)), ((), ()))

    lbw = lb_ref[...].astype(F32)
    ew = jnp.exp(lbw - jnp.max(lbw, axis=0, keepdims=True))
    lb = jnp.sum(ew[1:layer + 1], axis=0, keepdims=True) / jnp.sum(ew, axis=0, keepdims=True)

    def gates(ci):
        r0 = pl.multiple_of(ci * c, c)
        for d in range(2):
            fl = x_ref[0, pl.ds(r0, c), (1 + d) * hd:(2 + d) * hd].astype(F32)
            f = lb + (1.0 - lb) * _sigmoid(fl)
            g_ref[d, pl.ds(r0, c), :] = jnp.log2(f)
            k_ref[d, pl.ds(r0, c), :] = 1.0 - f

    def cumdecay(cis):
        r0s = [pl.multiple_of(ci * c, c) for ci in cis]
        xs = [g_ref[d, pl.ds(r0, c), :] for r0 in r0s for d in range(2)]
        xs = _cumsum_rows_multi(xs, [False, True] * len(cis))
        k = 0
        for r0 in r0s:
            for d in range(2):
                g_ref[d, pl.ds(r0, c), :] = xs[k]
                k += 1

    def decayed(ci):
        r0 = pl.multiple_of(ci * c, c)
        q = x_ref[0, pl.ds(r0, c), 0:hd].astype(F32)
        for d in range(2):
            mid = c // 2 if d == 0 else c // 2 - 1
            last = c - 1 if d == 0 else 0
            b = g_ref[d, pl.ds(r0, c), :]
            ref = g_ref[d, pl.ds(r0 + mid, 1), :]
            b_last = g_ref[d, pl.ds(r0 + last, 1), :]
            qe = q * jnp.exp2(b - ref)
            ke = k_ref[d, pl.ds(r0, c), :] * jnp.exp2(ref - b)
            qe_ref[d, pl.ds(r0, c), :] = qe.astype(BF16)
            ke_ref[d, pl.ds(r0, c), :] = ke.astype(BF16)
            qd_ref[pl.ds(r0, c), d * hd:(d + 1) * hd] = (qe * jnp.exp2(ref)).astype(BF16)
            kd_ref[pl.ds(r0, c), d * hd:(d + 1) * hd] = (ke * jnp.exp2(b_last - ref)).astype(BF16)
            dec_ref[ci, :, d * hd:(d + 1) * hd] = jnp.exp2(b_last)

    def prepare(cis):
        for ci in cis:
            gates(ci)
        cumdecay(cis)
        for ci in cis:
            decayed(ci)

    ti = lax.broadcasted_iota(jnp.int32, (c, 2 * c), 0)
    si = lax.broadcasted_iota(jnp.int32, (c, 2 * c), 1)
    m_fwd = si <= ti
    m_bwd = si - c >= ti

    def logits(cis):
        r0s = [pl.multiple_of(ci * c, c) for ci in cis]
        vs = [x_ref[0, pl.ds(r0, c), 3 * hd:4 * hd] for r0 in r0s]
        aa = [lax.dot_general(
            jnp.concatenate([qe_ref[0, pl.ds(r0, c), :], qe_ref[1, pl.ds(r0, c), :]], axis=0),
            jnp.concatenate([ke_ref[0, pl.ds(r0, c), :], ke_ref[1, pl.ds(r0, c), :]], axis=0),
            nt, preferred_element_type=F32) for r0 in r0s]
        dss = [lax.dot_general(v, kd_ref[pl.ds(r0, c), :], tn, preferred_element_type=F32)
               for v, r0 in zip(vs, r0s)]
        return r0s, vs, aa, dss

    def values(cis, vs, aa, dss):
        atts = [jnp.where(m_fwd, a[:c], jnp.where(m_bwd, a[c:], 0.0)).astype(BF16) for a in aa]
        ois = [jnp.dot(att, jnp.concatenate([v, v], axis=0), preferred_element_type=F32)
               for att, v in zip(atts, vs)]
        for ci, ds in zip(cis, dss):
            ds_ref[ci] = ds
        return tuple(ois)

    def store_intra(cis, ois):
        for ci, oi in zip(cis, ois):
            oi_ref[pl.ds(pl.multiple_of(ci * c, c), c), :] = oi

    n_pairs = n_chunks // HG_PAIR

    def pair(t):
        return [t * HG_PAIR + u for u in range(HG_PAIR)]

    def stage(t, pending, with_next):
        cur = pair(t)
        _, vs, aa, dss = logits(cur)
        if pending is not None:
            store_intra(pair(t - 1), pending)
        if with_next:
            nxt = pair(t + 1)
            for ci in nxt:
                gates(ci)
            cumdecay(nxt)
        ois = values(cur, vs, aa, dss)
        if with_next:
            for ci in nxt:
                decayed(ci)
        return ois

    prepare(pair(0))
    pending = stage(0, None, True)
    for t in range(1, n_pairs - 1):
        pending = stage(t, pending, True)
    pending = stage(n_pairs - 1, pending, False)
    store_intra(pair(n_pairs - 1), pending)

    def scan_b(i, carry):
        sf, sb = carry
        cb = jnp.where(i < nc_ctx, nc_ctx - 1 - i, n_chunks - 1 - (i - nc_ctx))
        st_ref[i, :, 0:hd] = sf.astype(BF16)
        st_ref[cb, :, hd:2 * hd] = sb.astype(BF16)
        sf = sf * dec_ref[i, :, 0:hd] + ds_ref[i, :, 0:hd]
        sb = sb * dec_ref[cb, :, hd:2 * hd] + ds_ref[cb, :, hd:2 * hd]
        return sf, sb

    zero = jnp.zeros((hd, hd), F32)
    lax.fori_loop(0, n_chunks, scan_b, (zero, zero), unroll=2)

    def inter(cis):
        os_ = [lax.dot_general(qd_ref[ci * c:(ci + 1) * c, :], st_ref[ci], nt, preferred_element_type=F32)
               for ci in cis]
        for ci, o in zip(cis, os_):
            oi_ref[ci * c:(ci + 1) * c, :] += o

    gn = ng_ref[...].astype(F32)

    def head_norm(r0):
        o = oi_ref[r0:r0 + HG_NORM_ROWS, :]
        o = o * lax.rsqrt(jnp.mean(o * o, axis=-1, keepdims=True) + NORM_EPS) * gn
        o_ref[0, r0:r0 + HG_NORM_ROWS, :] = o.astype(o_ref.dtype)

    group_rows = HG_UNROLL_INTER * c
    n_groups = n_chunks // HG_UNROLL_INTER
    for gi in range(n_groups + 1):
        if gi < n_groups:
            inter(range(gi * HG_UNROLL_INTER, (gi + 1) * HG_UNROLL_INTER))
        if gi >= 1:
            for r0 in range((gi - 1) * group_rows, gi * group_rows, HG_NORM_ROWS):
                head_norm(r0)


def _hgrn2(pz, hg_lb, norm_g, layer, n_ctx, e):
    bsz, t, _ = pz.shape
    n_h = e // HG_HEAD
    n_chunks = t // HG_CHUNK
    depth = hg_lb.shape[0]
    hd = HG_HEAD
    assert n_chunks % HG_PAIR == 0 and n_chunks // HG_PAIR >= 3 and n_chunks % HG_UNROLL_INTER == 0
    assert t % HG_NORM_ROWS == 0

    return pl.pallas_call(
        functools.partial(_hgrn2_kernel, layer=layer, n_chunks=n_chunks, nc_ctx=n_ctx // HG_CHUNK, n_tok=t),
        grid=(bsz, n_h),
        in_specs=[pl.BlockSpec((1, t, 4 * hd), lambda b, h: (b, 0, e // (4 * hd) + h)),
                  pl.BlockSpec((depth, hd), lambda b, h: (0, h)),
                  pl.BlockSpec((1, hd), lambda b, h: (0, h))],
        out_specs=pl.BlockSpec((1, t, hd), lambda b, h: (b, 0, h)),
        out_shape=jax.ShapeDtypeStruct((bsz, t, e), BF16),
        scratch_shapes=[pltpu.VMEM((2, t, hd), F32),
                        pltpu.VMEM((2, t, hd), F32),
                        pltpu.VMEM((2, t, hd), BF16),
                        pltpu.VMEM((2, t, hd), BF16),
                        pltpu.VMEM((t, 2 * hd), BF16),
                        pltpu.VMEM((t, 2 * hd), BF16),
                        pltpu.VMEM((n_chunks, hd, 2 * hd), F32),
                        pltpu.VMEM((-(-n_chunks // 16) * 16, 1, 2 * hd), F32),
                        pltpu.VMEM((n_chunks, hd, 2 * hd), BF16),
                        pltpu.VMEM((t, hd), F32)],
        compiler_params=_cparams(2),
    )(pz, hg_lb.astype(F32), norm_g.astype(F32).reshape(1, e))


def _head_major(w, e):
    d = w.shape[0]
    n_h = e // HG_HEAD
    rest = w[:, e:].reshape(d, 4, n_h, HG_HEAD).transpose(0, 2, 1, 3).reshape(d, 4 * e)
    return jnp.concatenate([w[:, :e], rest], axis=1)


def _z_first(w, e):
    return jnp.concatenate([w[..., -e:], w[..., :-e]], axis=-1)


def kernel(x, c, ctx, c_ctx, ada_w, ada_b, ln_g, ln_b, w_out, attn_w_in, attn_sink, s5_w_in, s5_lam_re, s5_lam_im, s5_log_step, s5_b_re, s5_b_im, s5_c_re, s5_c_im, s5_d, s5_glu_w, s5_glu_b, hg_w_in, hg_lb, hg_norm_g):
    bsz, n_lat, d = x.shape
    n_ctx = ctx.shape[1]
    depth = ada_w.shape[0]
    e = w_out.shape[1]
    alpha = (2.0 * depth) ** 0.25
    t = n_ctx + n_lat
    tm = 768 if t % 768 == 0 else 256

    mod_rows = -(-(bsz + 1) // MOD_ROWS_PAD) * MOD_ROWS_PAD
    c_all = jnp.concatenate([c.astype(F32), c_ctx.astype(F32)[None],
                             jnp.zeros((mod_rows - bsz - 1, d), F32)], axis=0)
    mod = _ada(c_all, ada_w.astype(F32), ada_b.astype(F32))
    mod3 = jnp.pad(mod, ((0, 0), (0, 0), (0, d))).reshape(depth * mod_rows, 1, 4 * d)

    xs = (ctx.astype(F32), x.astype(F32))
    for i in range(depth):
        kind, j = i % N_MIXERS, i // N_MIXERS
        w_in = _z_first((attn_w_in, s5_w_in, hg_w_in)[kind][j].astype(BF16), e)
        w_o = w_out[i].astype(BF16)
        if kind == 1:
            w_in = jnp.concatenate([_cg_order(w_in[:, :e], 1), _cg_order(w_in[:, e:], 1)], axis=1)
            w_o = _cg_order(w_o, 0)
        elif kind == 2:
            w_in = _head_major(w_in, e)
        pz = _inproj(xs, mod3, i, mod_rows, w_in.astype(BF16), n_ctx, tn=INPROJ_TN)
        last = i == depth - 1
        if kind == 0:
            y = _attention(pz, attn_sink[j], n_ctx, e, ctx_out=not last)
        elif kind == 1:
            ops = _s5_operators(s5_lam_re[j], s5_lam_im[j], s5_log_step[j], s5_b_re[j], s5_b_im[j],
                                s5_c_re[j], s5_c_im[j], s5_d[j])
            glu_w = _cg_order(_cg_order(s5_glu_w[j].astype(BF16), 0), 1)
            y = _s5_mixer(pz, n_ctx, e, ops, glu_w, _cg_order(s5_glu_b[j], 0))
        else:
            y = _hgrn2(pz, hg_lb, hg_norm_g[j], i, n_ctx, e)
        if last and kind == 0:
            return _outproj(y, pz, xs, mod3, i, mod_rows, w_o.astype(BF16), ln_g[i].astype(F32),
                            ln_b[i].astype(F32), n_ctx, alpha, LAST_TM, latent_only=True).astype(x.dtype)
        xs = (_outproj(y, pz, xs, mod3, i, mod_rows, w_o.astype(BF16), ln_g[i].astype(F32),
                       ln_b[i].astype(F32), n_ctx, alpha, tm),)
    return xs[0][:, n_ctx:].astype(x.dtype)
```

```python
import functools
import math

import jax
import jax.numpy as jnp
from jax import lax
from jax.experimental import pallas as pl
from jax.experimental.pallas import tpu as pltpu

F32 = jnp.float32
BF16 = jnp.bfloat16

N_MIXERS = 3
HEAD_DIM = 128
Q_PER_KV = 4
WINDOW = 128
ATTN_BLOCK = 128
GRID_W = 64
ROPE_THETA = 10000.0
S5_GROUP = 16
S5_STATE = 64
S5_CHUNK = 16
HG_HEAD = 128
HG_CHUNK = 64
HG_UNROLL_INTER = 12
HG_PAIR = 4
HG_NORM_ROWS = 256
NORM_EPS = 1e-5
NEG_INF = -1e30
LOG2E = math.log2(math.e)
ATTN_ROWS = 32
MOD_ROWS_PAD = 8
INPROJ_TN = 1024
LAST_TM = 256
LN_ROWS = 4
S5_A_ROWS = 16
GLU_PIECES = 8
VMEM_LIMIT = 56 * 1024 * 1024


def _cparams(n_axes):
    return pltpu.CompilerParams(dimension_semantics=("arbitrary",) * n_axes,
                                vmem_limit_bytes=VMEM_LIMIT)


def _sigmoid(x):
    return 1.0 / (1.0 + jnp.exp(-x))


def _ada_kernel(c_ref, w_ref, b_ref, o_ref):
    cv = c_ref[...]
    s = cv * _sigmoid(cv)
    o_ref[0] = jnp.dot(s, w_ref[0], preferred_element_type=F32,
                       precision=lax.Precision.HIGHEST) + b_ref[0]


def _ada(c_all, ada_w, ada_b):
    depth, d, d3 = ada_w.shape
    rows = c_all.shape[0]
    return pl.pallas_call(
        _ada_kernel,
        grid=(depth, d3 // d),
        in_specs=[pl.BlockSpec((rows, d), lambda i, j: (0, 0)),
                  pl.BlockSpec((1, d, d), lambda i, j: (i, 0, j)),
                  pl.BlockSpec((1, 1, d), lambda i, j: (i, 0, j))],
        out_specs=pl.BlockSpec((1, rows, d), lambda i, j: (i, 0, j)),
        out_shape=jax.ShapeDtypeStruct((depth, rows, d3), F32),
        compiler_params=_cparams(2),
    )(c_all, ada_w, ada_b.reshape(depth, 1, d3))


def _inproj_kernel(*refs, n_ctx, d, split):
    if split:
        xc_ref, xl_ref, mb_ref, mc_ref, w_ref, o_ref, h_ref = refs
    else:
        x_ref, mb_ref, mc_ref, w_ref, o_ref, h_ref = refs

    @pl.when(pl.program_id(1) == 0)
    def _():
        mb = mb_ref[0]
        mc = mc_ref[0]
        xc = xc_ref[0] if split else x_ref[0, :n_ctx]
        xl = xl_ref[0] if split else x_ref[0, n_ctx:]
        h_ref[:n_ctx] = (xc * (1.0 + mc[:, d:2 * d]) + mc[:, :d]).astype(BF16)
        h_ref[n_ctx:] = (xl * (1.0 + mb[:, d:2 * d]) + mb[:, :d]).astype(BF16)

    o_ref[0] = jnp.dot(h_ref[...], w_ref[...], preferred_element_type=F32).astype(o_ref.dtype)


def _inproj(xs, mod3, layer, mod_rows, w_bf, n_ctx, tn):
    split = len(xs) == 2
    bsz, _, d = xs[0].shape
    t = sum(a.shape[1] for a in xs)
    n = w_bf.shape[1]
    base = layer * mod_rows
    return pl.pallas_call(
        functools.partial(_inproj_kernel, n_ctx=n_ctx, d=d, split=split),
        grid=(bsz, n // tn),
        in_specs=[pl.BlockSpec((1, a.shape[1], d), lambda b, j: (b, 0, 0)) for a in xs] + [
                  pl.BlockSpec((1, 1, mod3.shape[2]), lambda b, j: (base + b, 0, 0)),
                  pl.BlockSpec((1, 1, mod3.shape[2]), lambda b, j: (base + bsz, 0, 0)),
                  pl.BlockSpec((d, tn), lambda b, j: (0, j))],
        out_specs=pl.BlockSpec((1, t, tn), lambda b, j: (b, 0, j)),
        out_shape=jax.ShapeDtypeStruct((bsz, t, n), BF16),
        scratch_shapes=[pltpu.VMEM((t, d), BF16)],
        compiler_params=_cparams(2),
    )(*xs, mod3, mod3, w_bf)


def _outproj_kernel(y_ref, z_ref, *refs, n_ctx, d, alpha, tm, first_tile, n_pieces):
    if n_pieces:
        x_refs, (c_ref, mb_ref, mc_ref, w_ref, ln_ref, o_ref) = refs[:n_pieces], refs[n_pieces:]
        first = jnp.where(pl.program_id(1) == 0, c_ref[0], x_refs[0][0])
        x_tile = jnp.concatenate([first] + [r[0] for r in x_refs[1:]], axis=0)
    else:
        x_ref, mb_ref, mc_ref, w_ref, ln_ref, o_ref = refs
        x_tile = x_ref[0]
    z = z_ref[0]
    a = y_ref[0] * (z * _sigmoid(z))
    br = jnp.dot(a, w_ref[...], preferred_element_type=F32)
    rows = (pl.program_id(1) + first_tile) * tm + lax.broadcasted_iota(jnp.int32, (tm, 1), 0)
    gate = jnp.where(rows < n_ctx, mc_ref[0][:, 2 * d:3 * d], mb_ref[0][:, 2 * d:3 * d])
    v = alpha * x_tile + gate * br
    mu = jnp.mean(v, axis=-1, keepdims=True)
    vc = v - mu
    var = jnp.mean(vc * vc, axis=-1, keepdims=True)
    o_ref[0] = vc * lax.rsqrt(var + NORM_EPS) * ln_ref[0:1, :] + ln_ref[1:2, :]


def _outproj(y, pz, xs, mod3, layer, mod_rows, w_bf, ln_g, ln_b, n_ctx, alpha, tm, latent_only=False):
    split = len(xs) == 2
    bsz, _, d = xs[0].shape
    t = sum(a.shape[1] for a in xs)
    e = w_bf.shape[0]
    base = layer * mod_rows
    off = n_ctx // tm if latent_only else 0
    assert n_ctx % tm == 0 or not latent_only
    ln = jnp.zeros((LN_ROWS, d), F32).at[0].set(ln_g).at[1].set(ln_b)
    n_pieces = tm // n_ctx if split else 0
    assert not split or (tm % n_ctx == 0 and not latent_only)
    if split:
        x_specs = [pl.BlockSpec((1, n_ctx, d), lambda b, i, p=p: (b, jnp.maximum(i * n_pieces + p - 1, 0), 0))
                   for p in range(n_pieces)] + [pl.BlockSpec((1, n_ctx, d), lambda b, i: (b, 0, 0))]
        x_args = [xs[1]] * n_pieces + [xs[0]]
    else:
        x_specs = [pl.BlockSpec((1, tm, d), lambda b, i: (b, i + off, 0))]
        x_args = [xs[0]]
    return pl.pallas_call(
        functools.partial(_outproj_kernel, n_ctx=n_ctx, d=d, alpha=alpha, tm=tm, first_tile=off,
                          n_pieces=n_pieces),
        grid=(bsz, t // tm - off),
        in_specs=[pl.BlockSpec((1, tm, e), lambda b, i: (b, i, 0)),
                  pl.BlockSpec((1, tm, e), lambda b, i: (b, i + off, 0))] + x_specs + [
                  pl.BlockSpec((1, 1, mod3.shape[2]), lambda b, i: (base + b, 0, 0)),
                  pl.BlockSpec((1, 1, mod3.shape[2]), lambda b, i: (base + bsz, 0, 0)),
                  pl.BlockSpec((e, d), lambda b, i: (0, 0)),
                  pl.BlockSpec((LN_ROWS, d), lambda b, i: (0, 0))],
        out_specs=pl.BlockSpec((1, tm, d), lambda b, i: (b, i, 0)),
        out_shape=jax.ShapeDtypeStruct((bsz, t - off * tm, d), F32),
        compiler_params=_cparams(2),
    )(y, pz, *x_args, mod3, mod3, w_bf, ln)


def _rope_tables(n_lat):
    pos = jnp.arange(n_lat, dtype=jnp.int32)
    row = (pos // GRID_W).astype(F32)
    col = (pos % GRID_W).astype(F32)
    nf = HEAD_DIM // 4
    inv_freq = jnp.power(ROPE_THETA, -jnp.arange(nf, dtype=F32) / nf)
    ang_r = row[:, None] * inv_freq[None, :]
    ang_c = col[:, None] * inv_freq[None, :]
    zeros = jnp.zeros_like(ang_r)
    cos = jnp.concatenate([jnp.cos(ang_r)] * 2 + [jnp.cos(ang_c)] * 2, axis=-1)
    s1 = jnp.concatenate([-jnp.sin(ang_r), zeros, -jnp.sin(ang_c), zeros], axis=-1)
    s2 = jnp.concatenate([zeros, jnp.sin(ang_r), zeros, jnp.sin(ang_c)], axis=-1)
    return cos, s1, s2


def _rope(x, cos, s1, s2):
    quarter = HEAD_DIM // 4
    return (x * cos + pltpu.roll(x, HEAD_DIM - quarter, 1) * s1
            + pltpu.roll(x, quarter, 1) * s2)


def _attn_kernel(sink_ref, q_ref, kv_ref, cq_ref, s1q_ref, s2q_ref, ck_ref, s1k_ref, s2k_ref,
                 o_ref, kr_ref, vx_ref, s_ref, p_ref, e_ref, bias_ref, *, n_ctx, n_lat, n_kv, first_block):
    j = pl.program_id(1) + first_block
    nb_ctx = n_ctx // ATTN_BLOCK
    kvw = n_kv * HEAD_DIM
    n_win = ATTN_BLOCK + 2 * WINDOW
    qscale = HEAD_DIM ** -0.5 * LOG2E
    rows_q = Q_PER_KV * ATTN_BLOCK
    vxw = 2 * HEAD_DIM
    nt = (((1,), (1,)), ((), ()))

    @pl.when(pl.program_id(1) == 0)
    def _():
        for h in range(n_kv):
            sl = slice(h * HEAD_DIM, (h + 1) * HEAD_DIM)
            k = kv_ref[0, n_ctx:, sl].astype(F32)
            kr_ref[:, sl] = _rope(k, ck_ref[...], s1k_ref[...], s2k_ref[...]).astype(BF16)
            vx_ref[:, h * vxw:h * vxw + HEAD_DIM] = kv_ref[0, :, kvw + h * HEAD_DIM:kvw + (h + 1) * HEAD_DIM]
            vx_ref[:, h * vxw + HEAD_DIM:(h + 1) * vxw] = jnp.ones((n_ctx + n_lat, HEAD_DIM), BF16)

    def key_cols(kparts):
        cols, off = [], 0
        for kp in kparts:
            cols.append((off, kp.shape[0]))
            off += kp.shape[0]
        return cols

    def logit_tasks(h, q_fn, kparts):
        box = {}

        def task(kp, o_, n):
            def run():
                if "q" not in box:
                    box["q"] = q_fn(h)
                s_ref[h % 2, :, o_:o_ + n] = lax.dot_general(box["q"], kp, nt, preferred_element_type=F32)
            return run

        return [task(kp, o_, n) for kp, (o_, n) in zip(kparts, key_cols(kparts))]

    def softmax_rows(h, cols, use_bias, blocks):
        for rb in blocks:
            r0 = rb * ATTN_ROWS
            sk = sink_ref[h * Q_PER_KV + r0 // ATTN_BLOCK] * LOG2E
            parts = []
            for idx, (o_, n) in enumerate(cols):
                s = s_ref[h % 2, r0:r0 + ATTN_ROWS, o_:o_ + n]
                if use_bias and idx == 0:
                    ql0 = r0 % ATTN_BLOCK
                    s = s + bias_ref[ql0:ql0 + ATTN_ROWS, :]
                parts.append(s)
            m = jnp.max(parts[0], axis=-1, keepdims=True)
            for s in parts[1:]:
                m = jnp.maximum(m, jnp.max(s, axis=-1, keepdims=True))
            m = jnp.maximum(m, sk)
            for (o_, n), s in zip(cols, parts):
                p_ref[h % 2, r0:r0 + ATTN_ROWS, o_:o_ + n] = jnp.exp2(s - m).astype(BF16)
            e_ref[h % 2, r0:r0 + ATTN_ROWS, :] = jnp.exp2(sk - m)

    def value_tasks(h, cols, vparts):
        box = {}

        def task(idx, o_, n, vp):
            def run():
                part = jnp.dot(p_ref[h % 2, :, o_:o_ + n], vp, preferred_element_type=F32)
                box["ox"] = part if idx == 0 else box["ox"] + part
                if idx == len(cols) - 1:
                    ox = box["ox"]
                    l = ox[:, HEAD_DIM:HEAD_DIM + 1] + e_ref[h % 2]
                    o = ox[:, :HEAD_DIM] * (1.0 / l)
                    for g in range(Q_PER_KV):
                        hq = h * Q_PER_KV + g
                        o_ref[0, :, hq * HEAD_DIM:(hq + 1) * HEAD_DIM] = (
                            o[g * ATTN_BLOCK:(g + 1) * ATTN_BLOCK].astype(o_ref.dtype))
            return run

        return [task(i, o_, n, vp) for i, ((o_, n), vp) in enumerate(zip(cols, vparts))]

    def attend_all(q_fn, k_fn, v_fn, use_bias):
        n_blocks = rows_q // ATTN_ROWS
        quarter = n_blocks // 4
        cols = key_cols(k_fn(0))
        for task in logit_tasks(0, q_fn, k_fn(0)):
            task()
        for h in range(n_kv + 1):
            side = []
            if h + 1 < n_kv:
                side += logit_tasks(h + 1, q_fn, k_fn(h + 1))
            if 1 <= h:
                side += value_tasks(h - 1, cols, v_fn(h - 1))
            for qi in range(4):
                if h < n_kv:
                    softmax_rows(h, cols, use_bias, range(qi * quarter, (qi + 1) * quarter))
                for task in side[qi::4]:
                    task()

    def q_head(hq):
        return q_ref[0, :, hq * HEAD_DIM:(hq + 1) * HEAD_DIM].astype(F32)

    def head_slice(h):
        return slice(h * HEAD_DIM, (h + 1) * HEAD_DIM)

    @pl.when(j < nb_ctx)
    def _():
        attend_all(
            lambda h: jnp.concatenate([(q_head(h * Q_PER_KV + g) * qscale).astype(BF16)
                                       for g in range(Q_PER_KV)], axis=0),
            lambda h: [kv_ref[0, :n_ctx, head_slice(h)]],
            lambda h: [vx_ref[:n_ctx, h * vxw:(h + 1) * vxw]], False)

    @pl.when(j >= nb_ctx)
    def _():
        jl = j - nb_ctx
        start = jnp.clip((jl - 1) * ATTN_BLOCK, 0, n_lat - n_win)
        start = pl.multiple_of(start, ATTN_BLOCK)
        delta = jl * ATTN_BLOCK - start
        ql = lax.broadcasted_iota(jnp.int32, (ATTN_BLOCK, n_win), 0)
        kl = lax.broadcasted_iota(jnp.int32, (ATTN_BLOCK, n_win), 1)
        bias_ref[...] = jnp.where(jnp.abs(delta + ql - kl) <= WINDOW, 0.0, NEG_INF)
        cq = cq_ref[...] * qscale
        s1q = s1q_ref[...] * qscale
        s2q = s2q_ref[...] * qscale
        attend_all(
            lambda h: jnp.concatenate([_rope(q_head(h * Q_PER_KV + g), cq, s1q, s2q).astype(BF16)
                                       for g in range(Q_PER_KV)], axis=0),
            lambda h: [kr_ref[pl.ds(start, n_win), head_slice(h)], kv_ref[0, :n_ctx, head_slice(h)]],
            lambda h: [vx_ref[pl.ds(n_ctx + start, n_win), h * vxw:(h + 1) * vxw],
                       vx_ref[:n_ctx, h * vxw:(h + 1) * vxw]], True)


def _attention(pz, sink, n_ctx, e, ctx_out):
    bsz, t, _ = pz.shape
    n_lat = t - n_ctx
    n_q = e // HEAD_DIM
    n_kv = n_q // Q_PER_KV
    kvw = n_kv * HEAD_DIM
    n_win = ATTN_BLOCK + 2 * WINDOW
    rows_q = Q_PER_KV * ATTN_BLOCK
    assert (2 * e) % (2 * kvw) == 0 and n_lat >= n_win
    cos, s1, s2 = _rope_tables(n_lat)
    nb_ctx = n_ctx // ATTN_BLOCK
    first = 0 if ctx_out else nb_ctx
    qtab = pl.BlockSpec((ATTN_BLOCK, HEAD_DIM), lambda b, j: (jnp.maximum(j + first - nb_ctx, 0), 0))
    ktab = pl.BlockSpec((n_lat, HEAD_DIM), lambda b, j: (0, 0))
    return pl.pallas_call(
        functools.partial(_attn_kernel, n_ctx=n_ctx, n_lat=n_lat, n_kv=n_kv, first_block=first),
        grid=(bsz, t // ATTN_BLOCK - first),
        in_specs=[pl.BlockSpec(memory_space=pltpu.SMEM),
                  pl.BlockSpec((1, ATTN_BLOCK, e), lambda b, j: (b, j + first, 1)),
                  pl.BlockSpec((1, t, 2 * kvw), lambda b, j: (b, 0, 2 * e // (2 * kvw))),
                  qtab, qtab, qtab, ktab, ktab, ktab],
        out_specs=pl.BlockSpec((1, ATTN_BLOCK, e), lambda b, j: (b, j, 0)),
        out_shape=jax.ShapeDtypeStruct((bsz, t - first * ATTN_BLOCK, e), BF16),
        scratch_shapes=[pltpu.VMEM((n_lat, kvw), BF16),
                        pltpu.VMEM((t, n_kv * 2 * HEAD_DIM), BF16),
                        pltpu.VMEM((2, rows_q, n_win + n_ctx), F32),
                        pltpu.VMEM((2, rows_q, n_win + n_ctx), BF16),
                        pltpu.VMEM((2, rows_q, 1), F32),
                        pltpu.VMEM((ATTN_BLOCK, n_win), F32)],
        compiler_params=_cparams(2),
    )(sink.astype(F32), pz, pz, cos, s1, s2, cos, s1, s2)


def _cg_order(w, axis):
    axis = axis % w.ndim
    n = w.shape[axis]
    shp = w.shape[:axis] + (n // S5_GROUP, S5_GROUP) + w.shape[axis + 1:]
    return jnp.swapaxes(w.reshape(shp), axis, axis + 1).reshape(w.shape)


def _s5_operators(lam_re, lam_im, log_step, b_re, b_im, c_re, c_im, d_skip):
    n_g = lam_re.shape[1]
    tc = S5_CHUNK
    dt = jnp.exp(log_step.astype(F32))[..., None]
    lr, li = lam_re.astype(F32) * dt, lam_im.astype(F32) * dt
    k = jnp.arange(tc + 1, dtype=F32)[:, None, None, None]
    mag = jnp.exp(lr[None] * k)
    pr, pi = mag * jnp.cos(li[None] * k), mag * jnp.sin(li[None] * k)
    ar1, ai1 = pr[1] - 1.0, pi[1]
    den = lam_re.astype(F32) ** 2 + lam_im.astype(F32) ** 2
    cr = (ar1 * lam_re + ai1 * lam_im) / den
    ci = (ai1 * lam_re - ar1 * lam_im) / den
    br, bi = b_re.astype(F32), b_im.astype(F32)
    bbr = cr[..., None] * br - ci[..., None] * bi
    bbi = cr[..., None] * bi + ci[..., None] * br
    ccr, cci = c_re.astype(F32), c_im.astype(F32)

    def kern(d):
        xr = pr[:tc, d][..., None] * bbr[d][None] - pi[:tc, d][..., None] * bbi[d][None]
        xi = pr[:tc, d][..., None] * bbi[d][None] + pi[:tc, d][..., None] * bbr[d][None]
        return (jnp.einsum('gnp,tgpm->tgnm', ccr[d], xr) - jnp.einsum('gnp,tgpm->tgnm', cci[d], xi))

    kf, kb = kern(0), kern(1)
    s_idx = jnp.arange(tc)[:, None]
    t_idx = jnp.arange(tc)[None, :]
    lag = t_idx - s_idx
    mf = jnp.where((lag >= 0)[:, :, None, None, None], kf[jnp.clip(lag, 0, tc - 1)], 0.0)
    mb = jnp.where((lag <= 0)[:, :, None, None, None], kb[jnp.clip(-lag, 0, tc - 1)], 0.0)
    m = (mf + mb).transpose(2, 4, 0, 3, 1)
    eye = (jnp.eye(S5_GROUP, dtype=F32)[:, None, :, None] * jnp.eye(tc, dtype=F32)[None, :, None, :])
    m = m + eye[None] * d_skip.astype(F32).reshape(n_g, S5_GROUP, 1, 1, 1)
    m = m.reshape(n_g, tc * S5_GROUP, tc * S5_GROUP)

    def qpart(d, pw_r, pw_i):
        qr = pw_r[..., None] * bbr[d][None] - pw_i[..., None] * bbi[d][None]
        qi = pw_r[..., None] * bbi[d][None] + pw_i[..., None] * bbr[d][None]
        return qr.transpose(1, 3, 0, 2), qi.transpose(1, 3, 0, 2)

    qfr, qfi = qpart(0, pr[tc - 1 - jnp.arange(tc), 0], pi[tc - 1 - jnp.arange(tc), 0])
    qbr, qbi = qpart(1, pr[jnp.arange(tc), 1], pi[jnp.arange(tc), 1])
    q = jnp.concatenate([qfr, qbr, qfi, qbi], axis=-1).reshape(n_g, tc * S5_GROUP, 4 * S5_STATE)

    def ppart(d, pw_r, pw_i):
        xr = ccr[d][None] * pw_r[:, :, None, :] - cci[d][None] * pw_i[:, :, None, :]
        xi = ccr[d][None] * pw_i[:, :, None, :] + cci[d][None] * pw_r[:, :, None, :]
        return xr.transpose(1, 3, 2, 0), (-xi).transpose(1, 3, 2, 0)

    pfr, pfi = ppart(0, pr[1 + jnp.arange(tc), 0], pi[1 + jnp.arange(tc), 0])
    pbr, pbi = ppart(1, pr[tc - jnp.arange(tc), 1], pi[tc - jnp.arange(tc), 1])
    p = jnp.concatenate([pfr, pbr, pfi, pbi], axis=1).reshape(n_g, 4 * S5_STATE, tc * S5_GROUP)
    a = jnp.stack([jnp.concatenate([pr[tc, 0], pr[tc, 1]], axis=-1),
                   jnp.concatenate([pi[tc, 0], pi[tc, 1]], axis=-1)], axis=1)
    a = jnp.pad(a, ((0, 0), (0, S5_A_ROWS - 2), (0, 0)))
    return m.astype(BF16), q.astype(BF16), p.astype(BF16), a


def _s5_pack_kernel(x_ref, o_ref, f_ref, *, nb, n_g):
    half = S5_GROUP // 2
    for b in range(nb):
        xb = x_ref[b].astype(F32)
        for k in range(2):
            a = jnp.concatenate([xb[:, cc * n_g:(cc + 1) * n_g]
                                 for cc in range(k * half, (k + 1) * half)], axis=0)
            f_ref[k, pl.ds(b, n_g, stride=nb), :] = a.T
    full = jnp.concatenate([f_ref[0], f_ref[1]], axis=1)
    o_ref[...] = full.reshape(n_g, nb, S5_CHUNK * S5_GROUP).astype(o_ref.dtype)


def _s5_pack(pz, e):
    bsz, t, _ = pz.shape
    n_g = e // S5_GROUP
    n_chunks = t // S5_CHUNK
    w = S5_CHUNK * S5_GROUP
    return pl.pallas_call(
        functools.partial(_s5_pack_kernel, nb=bsz, n_g=n_g),
        grid=(n_chunks,),
        in_specs=[pl.BlockSpec((bsz, S5_CHUNK, e), lambda i: (0, i, 1))],
        out_specs=pl.BlockSpec((n_g, bsz, w), lambda i: (0, i, 0)),
        out_shape=jax.ShapeDtypeStruct((n_g, n_chunks * bsz, w), BF16),
        scratch_shapes=[pltpu.VMEM((2, n_g * bsz, w // 2), F32)],
        compiler_params=_cparams(1),
    )(pz)


def _s5_kernel(u_ref, m_ref, q_ref, p_ref, a_ref, y_ref, s_ref, h_ref, *, nb, n_chunks, nc_ctx):
    ns = S5_STATE
    u = u_ref[0]
    s_ref[...] = jnp.dot(u, q_ref[0], preferred_element_type=F32)
    ar = a_ref[0, 0:1, :]
    ai = a_ref[0, 1:2, :]
    is_fwd = lax.broadcasted_iota(jnp.int32, (nb, 2 * ns), 1) < ns

    hr = hi = jnp.zeros((nb, 2 * ns), F32)
    for i in range(n_chunks):
        cb = nc_ctx - 1 - i if i < nc_ctx else n_chunks - 1 - (i - nc_ctx)
        rf, rb = i * nb, cb * nb
        h_ref[rf:rf + nb, 0:ns] = hr[:, 0:ns]
        h_ref[rb:rb + nb, ns:2 * ns] = hr[:, ns:2 * ns]
        h_ref[rf:rf + nb, 2 * ns:3 * ns] = hi[:, 0:ns]
        h_ref[rb:rb + nb, 3 * ns:4 * ns] = hi[:, ns:2 * ns]
        sf = s_ref[rf:rf + nb, :]
        sb = s_ref[rb:rb + nb, :]
        sr = jnp.where(is_fwd, sf[:, :2 * ns], sb[:, :2 * ns])
        si = jnp.where(is_fwd, sf[:, 2 * ns:], sb[:, 2 * ns:])
        hr, hi = ar * hr - ai * hi + sr, ar * hi + ai * hr + si
    y = jnp.dot(u, m_ref[0], preferred_element_type=F32)
    y = y + jnp.dot(h_ref[...].astype(BF16), p_ref[0], preferred_element_type=F32)
    y_ref[0] = y.astype(y_ref.dtype)


def _s5_scan(uf, m, q, p, a, nb, nc_ctx):
    n_g, rows, w = uf.shape
    n_chunks = rows // nb
    wspec = pl.BlockSpec((1, w, w), lambda g: (g, 0, 0))
    return pl.pallas_call(
        functools.partial(_s5_kernel, nb=nb, n_chunks=n_chunks, nc_ctx=nc_ctx),
        grid=(n_g,),
        in_specs=[pl.BlockSpec((1, rows, w), lambda g: (g, 0, 0)), wspec, wspec, wspec,
                  pl.BlockSpec((1, S5_A_ROWS, 2 * S5_STATE), lambda g: (g, 0, 0))],
        out_specs=pl.BlockSpec((1, rows, w), lambda g: (g, 0, 0)),
        out_shape=jax.ShapeDtypeStruct((n_g, rows, w), BF16),
        scratch_shapes=[pltpu.VMEM((rows, w), F32), pltpu.VMEM((rows, w), F32)],
        compiler_params=_cparams(1),
    )(uf, m, q, p, a)


def _s5_glu_kernel(y0_ref, yn_ref, w_ref, b_ref, o_ref, f_ref, g_ref, *, nb, n_g):
    i = pl.program_id(0)
    w = S5_CHUNK * S5_GROUP
    half = S5_GROUP // 2
    e = n_g * S5_GROUP

    def widen(y_ref):
        yv = y_ref[...].astype(F32).reshape(n_g * nb, w)
        for k in range(2):
            f_ref[k] = yv[:, k * (w // 2):(k + 1) * (w // 2)]

    def build(slot, batches):
        for b in batches:
            for k in range(2):
                t = f_ref[k, pl.ds(b, n_g, stride=nb), :].T
                g = 0.5 * t * (1.0 + jnp.tanh(math.sqrt(2.0 / math.pi) * (t + 0.044715 * t * t * t)))
                g = g.astype(BF16)
                for cc in range(half):
                    ch = k * half + cc
                    g_ref[slot, b * S5_CHUNK:(b + 1) * S5_CHUNK, ch * n_g:(ch + 1) * n_g] = (
                        g[cc * S5_CHUNK:(cc + 1) * S5_CHUNK, :])

    @pl.when(i == 0)
    def _():
        widen(y0_ref)
        build(0, range(nb))

    cur = i % 2
    nxt = 1 - cur
    n_pieces = GLU_PIECES
    tn = e // n_pieces
    per = nb // n_pieces
    widen(yn_ref)
    for p in range(n_pieces):
        cols = slice(p * tn, (p + 1) * tn)
        g = g_ref[cur]
        lin = jnp.dot(g, w_ref[:, cols], preferred_element_type=F32) + b_ref[0:1, cols]
        out = (g[:, cols].astype(F32) * _sigmoid(lin)).astype(o_ref.dtype)
        o_ref[:, :, cols] = out.reshape(nb, S5_CHUNK, tn)
        build(nxt, range(p * per, (p + 1) * per))


def _s5_glu(yf, w_bf, bias, bsz):
    n_g, rows, w = yf.shape
    e = n_g * S5_GROUP
    n_chunks = rows // bsz
    assert bsz % GLU_PIECES == 0 and e % GLU_PIECES == 0
    return pl.pallas_call(
        functools.partial(_s5_glu_kernel, nb=bsz, n_g=n_g),
        grid=(n_chunks,),
        in_specs=[pl.BlockSpec((n_g, bsz, w), lambda i: (0, 0, 0)),
                  pl.BlockSpec((n_g, bsz, w), lambda i: (0, jnp.minimum(i + 1, n_chunks - 1), 0)),
                  pl.BlockSpec((e, e), lambda i: (0, 0)),
                  pl.BlockSpec((2, e), lambda i: (0, 0))],
        out_specs=pl.BlockSpec((bsz, S5_CHUNK, e), lambda i: (0, i, 0)),
        out_shape=jax.ShapeDtypeStruct((bsz, n_chunks * S5_CHUNK, e), BF16),
        scratch_shapes=[pltpu.VMEM((2, n_g * bsz, w // 2), F32),
                        pltpu.VMEM((2, bsz * S5_CHUNK, e), BF16)],
        compiler_params=_cparams(1),
    )(yf, yf, w_bf, jnp.zeros((2, e), F32).at[0].set(bias.astype(F32)))


def _s5_mixer(pz, n_ctx, e, ops, glu_w_bf, glu_b):
    bsz = pz.shape[0]
    m, q, p, a = ops
    uf = _s5_pack(pz, e)
    yf = _s5_scan(uf, m, q, p, a, bsz, n_ctx // S5_CHUNK)
    return _s5_glu(yf, glu_w_bf, glu_b, bsz)


def _cumsum_rows_multi(xs, reverse):
    n = xs[0].shape[0]
    sub = 8
    n_slabs = n // sub
    row = lax.broadcasted_iota(jnp.int32, (sub, xs[0].shape[1]), 0)
    slabs = [[x[i * sub:(i + 1) * sub] for i in range(n_slabs)] for x in xs]
    s = 1
    while s < sub:
        nxt = []
        for arr, rev in zip(slabs, reverse):
            if rev:
                nxt.append([sl + jnp.where(row < sub - s, pltpu.roll(sl, sub - s, 0), 0.0) for sl in arr])
            else:
                nxt.append([sl + jnp.where(row >= s, pltpu.roll(sl, s, 0), 0.0) for sl in arr])
        slabs = nxt
        s *= 2
    out = []
    for arr, rev in zip(slabs, reverse):
        order = range(n_slabs - 1, -1, -1) if rev else range(n_slabs)
        edge = 0 if rev else sub - 1
        done = [None] * n_slabs
        carry = None
        for i in order:
            sl = arr[i] if carry is None else arr[i] + carry
            done[i] = sl
            carry = jnp.broadcast_to(sl[edge:edge + 1], sl.shape)
        out.append(jnp.concatenate(done, axis=0))
    return out


def _hgrn2_kernel(x_ref, lb_ref, ng_ref, o_ref,
                  g_ref, k_ref, qe_ref, ke_ref, qd_ref, kd_ref, ds_ref, dec_ref, st_ref, oi_ref,
                  *, layer, n_chunks, nc_ctx, n_tok):
    c = HG_CHUNK
    hd = HG_HEAD
    nt = (((1,), (1,)), ((), ()))
    tn = (((0,), (0,)), ((), ()))

    lbw = lb_ref[...].astype(F32)
    ew = jnp.exp(lbw - jnp.max(lbw, axis=0, keepdims=True))
    lb = jnp.sum(ew[1:layer + 1], axis=0, keepdims=True) / jnp.sum(ew, axis=0, keepdims=True)

    def gates(ci):
        r0 = pl.multiple_of(ci * c, c)
        for d in range(2):
            fl = x_ref[0, pl.ds(r0, c), (1 + d) * hd:(2 + d) * hd].astype(F32)
            f = lb + (1.0 - lb) * _sigmoid(fl)
            g_ref[d, pl.ds(r0, c), :] = jnp.log2(f)
            k_ref[d, pl.ds(r0, c), :] = 1.0 - f

    def cumdecay(cis):
        r0s = [pl.multiple_of(ci * c, c) for ci in cis]
        xs = [g_ref[d, pl.ds(r0, c), :] for r0 in r0s for d in range(2)]
        xs = _cumsum_rows_multi(xs, [False, True] * len(cis))
        k = 0
        for r0 in r0s:
            for d in range(2):
                g_ref[d, pl.ds(r0, c), :] = xs[k]
                k += 1

    def decayed(ci):
        r0 = pl.multiple_of(ci * c, c)
        q = x_ref[0, pl.ds(r0, c), 0:hd].astype(F32)
        for d in range(2):
            mid = c // 2 if d == 0 else c // 2 - 1
            last = c - 1 if d == 0 else 0
            b = g_ref[d, pl.ds(r0, c), :]
            ref = g_ref[d, pl.ds(r0 + mid, 1), :]
            b_last = g_ref[d, pl.ds(r0 + last, 1), :]
            qe = q * jnp.exp2(b - ref)
            ke = k_ref[d, pl.ds(r0, c), :] * jnp.exp2(ref - b)
            qe_ref[d, pl.ds(r0, c), :] = qe.astype(BF16)
            ke_ref[d, pl.ds(r0, c), :] = ke.astype(BF16)
            qd_ref[pl.ds(r0, c), d * hd:(d + 1) * hd] = (qe * jnp.exp2(ref)).astype(BF16)
            kd_ref[pl.ds(r0, c), d * hd:(d + 1) * hd] = (ke * jnp.exp2(b_last - ref)).astype(BF16)
            dec_ref[ci, :, d * hd:(d + 1) * hd] = jnp.exp2(b_last)

    def prepare(cis):
        for ci in cis:
            gates(ci)
        cumdecay(cis)
        for ci in cis:
            decayed(ci)

    ti = lax.broadcasted_iota(jnp.int32, (c, 2 * c), 0)
    si = lax.broadcasted_iota(jnp.int32, (c, 2 * c), 1)
    m_fwd = si <= ti
    m_bwd = si - c >= ti

    def logits(cis):
        r0s = [pl.multiple_of(ci * c, c) for ci in cis]
        vs = [x_ref[0, pl.ds(r0, c), 3 * hd:4 * hd] for r0 in r0s]
        aa = [lax.dot_general(
            jnp.concatenate([qe_ref[0, pl.ds(r0, c), :], qe_ref[1, pl.ds(r0, c), :]], axis=0),
            jnp.concatenate([ke_ref[0, pl.ds(r0, c), :], ke_ref[1, pl.ds(r0, c), :]], axis=0),
            nt, preferred_element_type=F32) for r0 in r0s]
        dss = [lax.dot_general(v, kd_ref[pl.ds(r0, c), :], tn, preferred_element_type=F32)
               for v, r0 in zip(vs, r0s)]
        return r0s, vs, aa, dss

    def values(cis, vs, aa, dss):
        atts = [jnp.where(m_fwd, a[:c], jnp.where(m_bwd, a[c:], 0.0)).astype(BF16) for a in aa]
        ois = [jnp.dot(att, jnp.concatenate([v, v], axis=0), preferred_element_type=F32)
               for att, v in zip(atts, vs)]
        for ci, ds in zip(cis, dss):
            ds_ref[ci] = ds
        return tuple(ois)

    def store_intra(cis, ois):
        for ci, oi in zip(cis, ois):
            oi_ref[pl.ds(pl.multiple_of(ci * c, c), c), :] = oi

    n_pairs = n_chunks // HG_PAIR

    def pair(t):
        return [t * HG_PAIR + u for u in range(HG_PAIR)]

    def stage(t, pending, with_next):
        cur = pair(t)
        _, vs, aa, dss = logits(cur)
        if pending is not None:
            store_intra(pair(t - 1), pending)
        if with_next:
            nxt = pair(t + 1)
            for ci in nxt:
                gates(ci)
            cumdecay(nxt)
        ois = values(cur, vs, aa, dss)
        if with_next:
            for ci in nxt:
                decayed(ci)
        return ois

    prepare(pair(0))
    pending = stage(0, None, True)
    for t in range(1, n_pairs - 1):
        pending = stage(t, pending, True)
    pending = stage(n_pairs - 1, pending, False)
    store_intra(pair(n_pairs - 1), pending)

    sf = sb = jnp.zeros((hd, hd), F32)
    for i in range(n_chunks):
        cb = nc_ctx - 1 - i if i < nc_ctx else n_chunks - 1 - (i - nc_ctx)
        st_ref[i, :, 0:hd] = sf.astype(BF16)
        st_ref[cb, :, hd:2 * hd] = sb.astype(BF16)
        sf = sf * dec_ref[i, :, 0:hd] + ds_ref[i, :, 0:hd]
        sb = sb * dec_ref[cb, :, hd:2 * hd] + ds_ref[cb, :, hd:2 * hd]

    def inter(cis):
        os_ = [lax.dot_general(qd_ref[ci * c:(ci + 1) * c, :], st_ref[ci], nt, preferred_element_type=F32)
               for ci in cis]
        for ci, o in zip(cis, os_):
            oi_ref[ci * c:(ci + 1) * c, :] += o

    gn = ng_ref[...].astype(F32)

    def head_norm(r0):
        o = oi_ref[r0:r0 + HG_NORM_ROWS, :]
        o = o * lax.rsqrt(jnp.mean(o * o, axis=-1, keepdims=True) + NORM_EPS) * gn
        o_ref[0, r0:r0 + HG_NORM_ROWS, :] = o.astype(o_ref.dtype)

    group_rows = HG_UNROLL_INTER * c
    n_groups = n_chunks // HG_UNROLL_INTER
    for gi in range(n_groups + 1):
        if gi < n_groups:
            inter(range(gi * HG_UNROLL_INTER, (gi + 1) * HG_UNROLL_INTER))
        if gi >= 1:
            for r0 in range((gi - 1) * group_rows, gi * group_rows, HG_NORM_ROWS):
                head_norm(r0)


def _hgrn2(pz, hg_lb, norm_g, layer, n_ctx, e):
    bsz, t, _ = pz.shape
    n_h = e // HG_HEAD
    n_chunks = t // HG_CHUNK
    depth = hg_lb.shape[0]
    hd = HG_HEAD
    assert n_chunks % HG_PAIR == 0 and n_chunks // HG_PAIR >= 3 and n_chunks % HG_UNROLL_INTER == 0
    assert t % HG_NORM_ROWS == 0

    return pl.pallas_call(
        functools.partial(_hgrn2_kernel, layer=layer, n_chunks=n_chunks, nc_ctx=n_ctx // HG_CHUNK, n_tok=t),
        grid=(bsz, n_h),
        in_specs=[pl.BlockSpec((1, t, 4 * hd), lambda b, h: (b, 0, e // (4 * hd) + h)),
                  pl.BlockSpec((depth, hd), lambda b, h: (0, h)),
                  pl.BlockSpec((1, hd), lambda b, h: (0, h))],
        out_specs=pl.BlockSpec((1, t, hd), lambda b, h: (b, 0, h)),
        out_shape=jax.ShapeDtypeStruct((bsz, t, e), BF16),
        scratch_shapes=[pltpu.VMEM((2, t, hd), F32),
                        pltpu.VMEM((2, t, hd), F32),
                        pltpu.VMEM((2, t, hd), BF16),
                        pltpu.VMEM((2, t, hd), BF16),
                        pltpu.VMEM((t, 2 * hd), BF16),
                        pltpu.VMEM((t, 2 * hd), BF16),
                        pltpu.VMEM((n_chunks, hd, 2 * hd), F32),
                        pltpu.VMEM((-(-n_chunks // 16) * 16, 1, 2 * hd), F32),
                        pltpu.VMEM((n_chunks, hd, 2 * hd), BF16),
                        pltpu.VMEM((t, hd), F32)],
        compiler_params=_cparams(2),
    )(pz, hg_lb.astype(F32), norm_g.astype(F32).reshape(1, e))


def _head_major(w, e):
    d = w.shape[0]
    n_h = e // HG_HEAD
    rest = w[:, e:].reshape(d, 4, n_h, HG_HEAD).transpose(0, 2, 1, 3).reshape(d, 4 * e)
    return jnp.concatenate([w[:, :e], rest], axis=1)


def _z_first(w, e):
    return jnp.concatenate([w[..., -e:], w[..., :-e]], axis=-1)


def kernel(x, c, ctx, c_ctx, ada_w, ada_b, ln_g, ln_b, w_out, attn_w_in, attn_sink, s5_w_in, s5_lam_re, s5_lam_im, s5_log_step, s5_b_re, s5_b_im, s5_c_re, s5_c_im, s5_d, s5_glu_w, s5_glu_b, hg_w_in, hg_lb, hg_norm_g):
    bsz, n_lat, d = x.shape
    n_ctx = ctx.shape[1]
    depth = ada_w.shape[0]
    e = w_out.shape[1]
    alpha = (2.0 * depth) ** 0.25
    t = n_ctx + n_lat
    tm = 768 if t % 768 == 0 else 256

    mod_rows = -(-(bsz + 1) // MOD_ROWS_PAD) * MOD_ROWS_PAD
    c_all = jnp.concatenate([c.astype(F32), c_ctx.astype(F32)[None],
                             jnp.zeros((mod_rows - bsz - 1, d), F32)], axis=0)
    mod = _ada(c_all, ada_w.astype(F32), ada_b.astype(F32))
    mod3 = jnp.pad(mod, ((0, 0), (0, 0), (0, d))).reshape(depth * mod_rows, 1, 4 * d)

    xs = (ctx.astype(F32), x.astype(F32))
    for i in range(depth):
        kind, j = i % N_MIXERS, i // N_MIXERS
        w_in = _z_first((attn_w_in, s5_w_in, hg_w_in)[kind][j].astype(BF16), e)
        w_o = w_out[i].astype(BF16)
        if kind == 1:
            w_in = jnp.concatenate([_cg_order(w_in[:, :e], 1), _cg_order(w_in[:, e:], 1)], axis=1)
            w_o = _cg_order(w_o, 0)
        elif kind == 2:
            w_in = _head_major(w_in, e)
        pz = _inproj(xs, mod3, i, mod_rows, w_in.astype(BF16), n_ctx, tn=INPROJ_TN)
        last = i == depth - 1
        if kind == 0:
            y = _attention(pz, attn_sink[j], n_ctx, e, ctx_out=not last)
        elif kind == 1:
            ops = _s5_operators(s5_lam_re[j], s5_lam_im[j], s5_log_step[j], s5_b_re[j], s5_b_im[j],
                                s5_c_re[j], s5_c_im[j], s5_d[j])
            glu_w = _cg_order(_cg_order(s5_glu_w[j].astype(BF16), 0), 1)
            y = _s5_mixer(pz, n_ctx, e, ops, glu_w, _cg_order(s5_glu_b[j], 0))
        else:
            y = _hgrn2(pz, hg_lb, hg_norm_g[j], i, n_ctx, e)
        if last and kind == 0:
            return _outproj(y, pz, xs, mod3, i, mod_rows, w_o.astype(BF16), ln_g[i].astype(F32),
                            ln_b[i].astype(F32), n_ctx, alpha, LAST_TM, latent_only=True).astype(x.dtype)
        xs = (_outproj(y, pz, xs, mod3, i, mod_rows, w_o.astype(BF16), ln_g[i].astype(F32),
                       ln_b[i].astype(F32), n_ctx, alpha, tm),)
    return xs[0][:, n_ctx:].astype(x.dtype)
```

```python
import functools
import math

import jax
import jax.numpy as jnp
from jax import lax
from jax.experimental import pallas as pl
from jax.experimental.pallas import tpu as pltpu

F32 = jnp.float32
BF16 = jnp.bfloat16

N_MIXERS = 3
HEAD_DIM = 128
Q_PER_KV = 4
WINDOW = 128
ATTN_BLOCK = 128
GRID_W = 64
ROPE_THETA = 10000.0
S5_GROUP = 16
S5_STATE = 64
S5_CHUNK = 16
HG_HEAD = 128
HG_CHUNK = 64
HG_UNROLL_INTER = 12
HG_PAIR = 4
HG_NORM_ROWS = 256
NORM_EPS = 1e-5
NEG_INF = -1e30
LOG2E = math.log2(math.e)
ATTN_ROWS = 32
MOD_ROWS_PAD = 8
INPROJ_TN = 1024
LAST_TM = 256
LN_ROWS = 4
S5_A_ROWS = 16
GLU_PIECES = 8
VMEM_LIMIT = 56 * 1024 * 1024


def _cparams(n_axes):
    return pltpu.CompilerParams(dimension_semantics=("arbitrary",) * n_axes,
                                vmem_limit_bytes=VMEM_LIMIT)


def _sigmoid(x):
    return 1.0 / (1.0 + jnp.exp(-x))


def _ada_kernel(c_ref, w_ref, b_ref, o_ref):
    cv = c_ref[...]
    s = cv * _sigmoid(cv)
    o_ref[0] = jnp.dot(s, w_ref[0], preferred_element_type=F32,
                       precision=lax.Precision.HIGHEST) + b_ref[0]


def _ada(c_all, ada_w, ada_b):
    depth, d, d3 = ada_w.shape
    rows = c_all.shape[0]
    return pl.pallas_call(
        _ada_kernel,
        grid=(depth, d3 // d),
        in_specs=[pl.BlockSpec((rows, d), lambda i, j: (0, 0)),
                  pl.BlockSpec((1, d, d), lambda i, j: (i, 0, j)),
                  pl.BlockSpec((1, 1, d), lambda i, j: (i, 0, j))],
        out_specs=pl.BlockSpec((1, rows, d), lambda i, j: (i, 0, j)),
        out_shape=jax.ShapeDtypeStruct((depth, rows, d3), F32),
        compiler_params=_cparams(2),
    )(c_all, ada_w, ada_b.reshape(depth, 1, d3))


def _inproj_kernel(*refs, n_ctx, d, split):
    if split:
        xc_ref, xl_ref, mb_ref, mc_ref, w_ref, o_ref, h_ref = refs
    else:
        x_ref, mb_ref, mc_ref, w_ref, o_ref, h_ref = refs

    @pl.when(pl.program_id(1) == 0)
    def _():
        mb = mb_ref[0]
        mc = mc_ref[0]
        xc = xc_ref[0] if split else x_ref[0, :n_ctx]
        xl = xl_ref[0] if split else x_ref[0, n_ctx:]
        h_ref[:n_ctx] = (xc * (1.0 + mc[:, d:2 * d]) + mc[:, :d]).astype(BF16)
        h_ref[n_ctx:] = (xl * (1.0 + mb[:, d:2 * d]) + mb[:, :d]).astype(BF16)

    o_ref[0] = jnp.dot(h_ref[...], w_ref[...], preferred_element_type=F32).astype(o_ref.dtype)


def _inproj(xs, mod3, layer, mod_rows, w_bf, n_ctx, tn):
    split = len(xs) == 2
    bsz, _, d = xs[0].shape
    t = sum(a.shape[1] for a in xs)
    n = w_bf.shape[1]
    base = layer * mod_rows
    return pl.pallas_call(
        functools.partial(_inproj_kernel, n_ctx=n_ctx, d=d, split=split),
        grid=(bsz, n // tn),
        in_specs=[pl.BlockSpec((1, a.shape[1], d), lambda b, j: (b, 0, 0)) for a in xs] + [
                  pl.BlockSpec((1, 1, mod3.shape[2]), lambda b, j: (base + b, 0, 0)),
                  pl.BlockSpec((1, 1, mod3.shape[2]), lambda b, j: (base + bsz, 0, 0)),
                  pl.BlockSpec((d, tn), lambda b, j: (0, j))],
        out_specs=pl.BlockSpec((1, t, tn), lambda b, j: (b, 0, j)),
        out_shape=jax.ShapeDtypeStruct((bsz, t, n), BF16),
        scratch_shapes=[pltpu.VMEM((t, d), BF16)],
        compiler_params=_cparams(2),
    )(*xs, mod3, mod3, w_bf)


def _outproj_kernel(y_ref, z_ref, *refs, n_ctx, d, alpha, tm, first_tile, n_pieces):
    if n_pieces:
        x_refs, (c_ref, mb_ref, mc_ref, w_ref, ln_ref, o_ref) = refs[:n_pieces], refs[n_pieces:]
        first = jnp.where(pl.program_id(1) == 0, c_ref[0], x_refs[0][0])
        x_tile = jnp.concatenate([first] + [r[0] for r in x_refs[1:]], axis=0)
    else:
        x_ref, mb_ref, mc_ref, w_ref, ln_ref, o_ref = refs
        x_tile = x_ref[0]
    z = z_ref[0]
    a = y_ref[0] * (z * _sigmoid(z))
    br = jnp.dot(a, w_ref[...], preferred_element_type=F32)
    rows = (pl.program_id(1) + first_tile) * tm + lax.broadcasted_iota(jnp.int32, (tm, 1), 0)
    gate = jnp.where(rows < n_ctx, mc_ref[0][:, 2 * d:3 * d], mb_ref[0][:, 2 * d:3 * d])
    v = alpha * x_tile + gate * br
    mu = jnp.mean(v, axis=-1, keepdims=True)
    vc = v - mu
    var = jnp.mean(vc * vc, axis=-1, keepdims=True)
    o_ref[0] = vc * lax.rsqrt(var + NORM_EPS) * ln_ref[0:1, :] + ln_ref[1:2, :]


def _outproj(y, pz, xs, mod3, layer, mod_rows, w_bf, ln_g, ln_b, n_ctx, alpha, tm, latent_only=False):
    split = len(xs) == 2
    bsz, _, d = xs[0].shape
    t = sum(a.shape[1] for a in xs)
    e = w_bf.shape[0]
    base = layer * mod_rows
    off = n_ctx // tm if latent_only else 0
    assert n_ctx % tm == 0 or not latent_only
    ln = jnp.zeros((LN_ROWS, d), F32).at[0].set(ln_g).at[1].set(ln_b)
    n_pieces = tm // n_ctx if split else 0
    assert not split or (tm % n_ctx == 0 and not latent_only)
    if split:
        x_specs = [pl.BlockSpec((1, n_ctx, d), lambda b, i, p=p: (b, jnp.maximum(i * n_pieces + p - 1, 0), 0))
                   for p in range(n_pieces)] + [pl.BlockSpec((1, n_ctx, d), lambda b, i: (b, 0, 0))]
        x_args = [xs[1]] * n_pieces + [xs[0]]
    else:
        x_specs = [pl.BlockSpec((1, tm, d), lambda b, i: (b, i + off, 0))]
        x_args = [xs[0]]
    return pl.pallas_call(
        functools.partial(_outproj_kernel, n_ctx=n_ctx, d=d, alpha=alpha, tm=tm, first_tile=off,
                          n_pieces=n_pieces),
        grid=(bsz, t // tm - off),
        in_specs=[pl.BlockSpec((1, tm, e), lambda b, i: (b, i, 0)),
                  pl.BlockSpec((1, tm, e), lambda b, i: (b, i + off, 0))] + x_specs + [
                  pl.BlockSpec((1, 1, mod3.shape[2]), lambda b, i: (base + b, 0, 0)),
                  pl.BlockSpec((1, 1, mod3.shape[2]), lambda b, i: (base + bsz, 0, 0)),
                  pl.BlockSpec((e, d), lambda b, i: (0, 0)),
                  pl.BlockSpec((LN_ROWS, d), lambda b, i: (0, 0))],
        out_specs=pl.BlockSpec((1, tm, d), lambda b, i: (b, i, 0)),
        out_shape=jax.ShapeDtypeStruct((bsz, t - off * tm, d), F32),
        compiler_params=_cparams(2),
    )(y, pz, *x_args, mod3, mod3, w_bf, ln)


def _rope_tables(n_lat):
    pos = jnp.arange(n_lat, dtype=jnp.int32)
    row = (pos // GRID_W).astype(F32)
    col = (pos % GRID_W).astype(F32)
    nf = HEAD_DIM // 4
    inv_freq = jnp.power(ROPE_THETA, -jnp.arange(nf, dtype=F32) / nf)
    ang_r = row[:, None] * inv_freq[None, :]
    ang_c = col[:, None] * inv_freq[None, :]
    zeros = jnp.zeros_like(ang_r)
    cos = jnp.concatenate([jnp.cos(ang_r)] * 2 + [jnp.cos(ang_c)] * 2, axis=-1)
    s1 = jnp.concatenate([-jnp.sin(ang_r), zeros, -jnp.sin(ang_c), zeros], axis=-1)
    s2 = jnp.concatenate([zeros, jnp.sin(ang_r), zeros, jnp.sin(ang_c)], axis=-1)
    return cos, s1, s2


def _rope(x, cos, s1, s2):
    quarter = HEAD_DIM // 4
    return (x * cos + pltpu.roll(x, HEAD_DIM - quarter, 1) * s1
            + pltpu.roll(x, quarter, 1) * s2)


def _attn_kernel(sink_ref, q_ref, kv_ref, cq_ref, s1q_ref, s2q_ref, ck_ref, s1k_ref, s2k_ref,
                 o_ref, kr_ref, vx_ref, s_ref, p_ref, e_ref, bias_ref, *, n_ctx, n_lat, n_kv, first_block):
    j = pl.program_id(1) + first_block
    nb_ctx = n_ctx // ATTN_BLOCK
    kvw = n_kv * HEAD_DIM
    n_win = ATTN_BLOCK + 2 * WINDOW
    qscale = HEAD_DIM ** -0.5 * LOG2E
    rows_q = Q_PER_KV * ATTN_BLOCK
    vxw = 2 * HEAD_DIM
    nt = (((1,), (1,)), ((), ()))

    @pl.when(pl.program_id(1) == 0)
    def _():
        for h in range(n_kv):
            sl = slice(h * HEAD_DIM, (h + 1) * HEAD_DIM)
            k = kv_ref[0, n_ctx:, sl].astype(F32)
            kr_ref[:, sl] = _rope(k, ck_ref[...], s1k_ref[...], s2k_ref[...]).astype(BF16)
            vx_ref[:, h * vxw:h * vxw + HEAD_DIM] = kv_ref[0, :, kvw + h * HEAD_DIM:kvw + (h + 1) * HEAD_DIM]
            vx_ref[:, h * vxw + HEAD_DIM:(h + 1) * vxw] = jnp.ones((n_ctx + n_lat, HEAD_DIM), BF16)

    def key_cols(kparts):
        cols, off = [], 0
        for kp in kparts:
            cols.append((off, kp.shape[0]))
            off += kp.shape[0]
        return cols

    def logit_tasks(h, q_fn, kparts):
        box = {}

        def task(kp, o_, n):
            def run():
                if "q" not in box:
                    box["q"] = q_fn(h)
                s_ref[h % 2, :, o_:o_ + n] = lax.dot_general(box["q"], kp, nt, preferred_element_type=F32)
            return run

        return [task(kp, o_, n) for kp, (o_, n) in zip(kparts, key_cols(kparts))]

    def softmax_rows(h, cols, use_bias, blocks):
        for rb in blocks:
            r0 = rb * ATTN_ROWS
            sk = sink_ref[h * Q_PER_KV + r0 // ATTN_BLOCK] * LOG2E
            parts = []
            for idx, (o_, n) in enumerate(cols):
                s = s_ref[h % 2, r0:r0 + ATTN_ROWS, o_:o_ + n]
                if use_bias and idx == 0:
                    ql0 = r0 % ATTN_BLOCK
                    s = s + bias_ref[ql0:ql0 + ATTN_ROWS, :]
                parts.append(s)
            m = jnp.max(parts[0], axis=-1, keepdims=True)
            for s in parts[1:]:
                m = jnp.maximum(m, jnp.max(s, axis=-1, keepdims=True))
            m = jnp.maximum(m, sk)
            for (o_, n), s in zip(cols, parts):
                p_ref[h % 2, r0:r0 + ATTN_ROWS, o_:o_ + n] = jnp.exp2(s - m).astype(BF16)
            e_ref[h % 2, r0:r0 + ATTN_ROWS, :] = jnp.exp2(sk - m)

    def value_tasks(h, cols, vparts):
        box = {}

        def task(idx, o_, n, vp):
            def run():
                part = jnp.dot(p_ref[h % 2, :, o_:o_ + n], vp, preferred_element_type=F32)
                box["ox"] = part if idx == 0 else box["ox"] + part
                if idx == len(cols) - 1:
                    ox = box["ox"]
                    l = ox[:, HEAD_DIM:HEAD_DIM + 1] + e_ref[h % 2]
                    o = ox[:, :HEAD_DIM] * (1.0 / l)
                    for g in range(Q_PER_KV):
                        hq = h * Q_PER_KV + g
                        o_ref[0, :, hq * HEAD_DIM:(hq + 1) * HEAD_DIM] = (
                            o[g * ATTN_BLOCK:(g + 1) * ATTN_BLOCK].astype(o_ref.dtype))
            return run

        return [task(i, o_, n, vp) for i, ((o_, n), vp) in enumerate(zip(cols, vparts))]

    def attend_all(q_fn, k_fn, v_fn, use_bias):
        n_blocks = rows_q // ATTN_ROWS
        quarter = n_blocks // 4
        cols = key_cols(k_fn(0))
        for task in logit_tasks(0, q_fn, k_fn(0)):
            task()
        for h in range(n_kv + 1):
            side = []
            if h + 1 < n_kv:
                side += logit_tasks(h + 1, q_fn, k_fn(h + 1))
            if 1 <= h:
                side += value_tasks(h - 1, cols, v_fn(h - 1))
            for qi in range(4):
                if h < n_kv:
                    softmax_rows(h, cols, use_bias, range(qi * quarter, (qi + 1) * quarter))
                for task in side[qi::4]:
                    task()

    def q_head(hq):
        return q_ref[0, :, hq * HEAD_DIM:(hq + 1) * HEAD_DIM].astype(F32)

    def head_slice(h):
        return slice(h * HEAD_DIM, (h + 1) * HEAD_DIM)

    @pl.when(j < nb_ctx)
    def _():
        attend_all(
            lambda h: jnp.concatenate([(q_head(h * Q_PER_KV + g) * qscale).astype(BF16)
                                       for g in range(Q_PER_KV)], axis=0),
            lambda h: [kv_ref[0, :n_ctx, head_slice(h)]],
            lambda h: [vx_ref[:n_ctx, h * vxw:(h + 1) * vxw]], False)

    @pl.when(j >= nb_ctx)
    def _():
        jl = j - nb_ctx
        start = jnp.clip((jl - 1) * ATTN_BLOCK, 0, n_lat - n_win)
        start = pl.multiple_of(start, ATTN_BLOCK)
        delta = jl * ATTN_BLOCK - start
        ql = lax.broadcasted_iota(jnp.int32, (ATTN_BLOCK, n_win), 0)
        kl = lax.broadcasted_iota(jnp.int32, (ATTN_BLOCK, n_win), 1)
        bias_ref[...] = jnp.where(jnp.abs(delta + ql - kl) <= WINDOW, 0.0, NEG_INF)
        cq = cq_ref[...] * qscale
        s1q = s1q_ref[...] * qscale
        s2q = s2q_ref[...] * qscale
        attend_all(
            lambda h: jnp.concatenate([_rope(q_head(h * Q_PER_KV + g), cq, s1q, s2q).astype(BF16)
                                       for g in range(Q_PER_KV)], axis=0),
            lambda h: [kr_ref[pl.ds(start, n_win), head_slice(h)], kv_ref[0, :n_ctx, head_slice(h)]],
            lambda h: [vx_ref[pl.ds(n_ctx + start, n_win), h * vxw:(h + 1) * vxw],
                       vx_ref[:n_ctx, h * vxw:(h + 1) * vxw]], True)


def _attention(pz, sink, n_ctx, e, ctx_out):
    bsz, t, _ = pz.shape
    n_lat = t - n_ctx
    n_q = e // HEAD_DIM
    n_kv = n_q // Q_PER_KV
    kvw = n_kv * HEAD_DIM
    n_win = ATTN_BLOCK + 2 * WINDOW
    rows_q = Q_PER_KV * ATTN_BLOCK
    assert (2 * e) % (2 * kvw) == 0 and n_lat >= n_win
    cos, s1, s2 = _rope_tables(n_lat)
    nb_ctx = n_ctx // ATTN_BLOCK
    first = 0 if ctx_out else nb_ctx
    qtab = pl.BlockSpec((ATTN_BLOCK, HEAD_DIM), lambda b, j: (jnp.maximum(j + first - nb_ctx, 0), 0))
    ktab = pl.BlockSpec((n_lat, HEAD_DIM), lambda b, j: (0, 0))
    return pl.pallas_call(
        functools.partial(_attn_kernel, n_ctx=n_ctx, n_lat=n_lat, n_kv=n_kv, first_block=first),
        grid=(bsz, t // ATTN_BLOCK - first),
        in_specs=[pl.BlockSpec(memory_space=pltpu.SMEM),
                  pl.BlockSpec((1, ATTN_BLOCK, e), lambda b, j: (b, j + first, 1)),
                  pl.BlockSpec((1, t, 2 * kvw), lambda b, j: (b, 0, 2 * e // (2 * kvw))),
                  qtab, qtab, qtab, ktab, ktab, ktab],
        out_specs=pl.BlockSpec((1, ATTN_BLOCK, e), lambda b, j: (b, j, 0)),
        out_shape=jax.ShapeDtypeStruct((bsz, t - first * ATTN_BLOCK, e), BF16),
        scratch_shapes=[pltpu.VMEM((n_lat, kvw), BF16),
                        pltpu.VMEM((t, n_kv * 2 * HEAD_DIM), BF16),
                        pltpu.VMEM((2, rows_q, n_win + n_ctx), F32),
                        pltpu.VMEM((2, rows_q, n_win + n_ctx), BF16),
                        pltpu.VMEM((2, rows_q, 1), F32),
                        pltpu.VMEM((ATTN_BLOCK, n_win), F32)],
        compiler_params=_cparams(2),
    )(sink.astype(F32), pz, pz, cos, s1, s2, cos, s1, s2)


def _cg_order(w, axis):
    axis = axis % w.ndim
    n = w.shape[axis]
    shp = w.shape[:axis] + (n // S5_GROUP, S5_GROUP) + w.shape[axis + 1:]
    return jnp.swapaxes(w.reshape(shp), axis, axis + 1).reshape(w.shape)


def _s5_operators(lam_re, lam_im, log_step, b_re, b_im, c_re, c_im, d_skip):
    n_g = lam_re.shape[1]
    tc = S5_CHUNK
    dt = jnp.exp(log_step.astype(F32))[..., None]
    lr, li = lam_re.astype(F32) * dt, lam_im.astype(F32) * dt
    k = jnp.arange(tc + 1, dtype=F32)[:, None, None, None]
    mag = jnp.exp(lr[None] * k)
    pr, pi = mag * jnp.cos(li[None] * k), mag * jnp.sin(li[None] * k)
    ar1, ai1 = pr[1] - 1.0, pi[1]
    den = lam_re.astype(F32) ** 2 + lam_im.astype(F32) ** 2
    cr = (ar1 * lam_re + ai1 * lam_im) / den
    ci = (ai1 * lam_re - ar1 * lam_im) / den
    br, bi = b_re.astype(F32), b_im.astype(F32)
    bbr = cr[..., None] * br - ci[..., None] * bi
    bbi = cr[..., None] * bi + ci[..., None] * br
    ccr, cci = c_re.astype(F32), c_im.astype(F32)

    def kern(d):
        xr = pr[:tc, d][..., None] * bbr[d][None] - pi[:tc, d][..., None] * bbi[d][None]
        xi = pr[:tc, d][..., None] * bbi[d][None] + pi[:tc, d][..., None] * bbr[d][None]
        return (jnp.einsum('gnp,tgpm->tgnm', ccr[d], xr) - jnp.einsum('gnp,tgpm->tgnm', cci[d], xi))

    kf, kb = kern(0), kern(1)
    s_idx = jnp.arange(tc)[:, None]
    t_idx = jnp.arange(tc)[None, :]
    lag = t_idx - s_idx
    taus = jnp.arange(tc)[:, None, None]
    place_f = (lag[None] == taus).astype(F32)
    place_b = (-lag[None] == taus).astype(F32)
    m = (jnp.einsum('xgnm,xst->gmsnt', kf, place_f, precision=lax.Precision.HIGHEST)
         + jnp.einsum('xgnm,xst->gmsnt', kb, place_b, precision=lax.Precision.HIGHEST))
    eye = (jnp.eye(S5_GROUP, dtype=F32)[:, None, :, None] * jnp.eye(tc, dtype=F32)[None, :, None, :])
    m = m + eye[None] * d_skip.astype(F32).reshape(n_g, S5_GROUP, 1, 1, 1)
    m = m.reshape(n_g, tc * S5_GROUP, tc * S5_GROUP)

    def qpart(d, pw_r, pw_i):
        qr = pw_r[..., None] * bbr[d][None] - pw_i[..., None] * bbi[d][None]
        qi = pw_r[..., None] * bbi[d][None] + pw_i[..., None] * bbr[d][None]
        return qr.transpose(1, 3, 0, 2), qi.transpose(1, 3, 0, 2)

    qfr, qfi = qpart(0, pr[tc - 1 - jnp.arange(tc), 0], pi[tc - 1 - jnp.arange(tc), 0])
    qbr, qbi = qpart(1, pr[jnp.arange(tc), 1], pi[jnp.arange(tc), 1])
    q = jnp.concatenate([qfr, qbr, qfi, qbi], axis=-1).reshape(n_g, tc * S5_GROUP, 4 * S5_STATE)

    def ppart(d, pw_r, pw_i):
        xr = ccr[d][None] * pw_r[:, :, None, :] - cci[d][None] * pw_i[:, :, None, :]
        xi = ccr[d][None] * pw_i[:, :, None, :] + cci[d][None] * pw_r[:, :, None, :]
        return xr.transpose(1, 3, 2, 0), (-xi).transpose(1, 3, 2, 0)

    pfr, pfi = ppart(0, pr[1 + jnp.arange(tc), 0], pi[1 + jnp.arange(tc), 0])
    pbr, pbi = ppart(1, pr[tc - jnp.arange(tc), 1], pi[tc - jnp.arange(tc), 1])
    p = jnp.concatenate([pfr, pbr, pfi, pbi], axis=1).reshape(n_g, 4 * S5_STATE, tc * S5_GROUP)
    a = jnp.stack([jnp.concatenate([pr[tc, 0], pr[tc, 1]], axis=-1),
                   jnp.concatenate([pi[tc, 0], pi[tc, 1]], axis=-1)], axis=1)
    a = jnp.pad(a, ((0, 0), (0, S5_A_ROWS - 2), (0, 0)))
    return m.astype(BF16), q.astype(BF16), p.astype(BF16), a


def _s5_pack_kernel(x_ref, o_ref, f_ref, *, nb, n_g):
    half = S5_GROUP // 2
    for b in range(nb):
        xb = x_ref[b].astype(F32)
        for k in range(2):
            a = jnp.concatenate([xb[:, cc * n_g:(cc + 1) * n_g]
                                 for cc in range(k * half, (k + 1) * half)], axis=0)
            f_ref[k, pl.ds(b, n_g, stride=nb), :] = a.T
    full = jnp.concatenate([f_ref[0], f_ref[1]], axis=1)
    o_ref[...] = full.reshape(n_g, nb, S5_CHUNK * S5_GROUP).astype(o_ref.dtype)


def _s5_pack(pz, e):
    bsz, t, _ = pz.shape
    n_g = e // S5_GROUP
    n_chunks = t // S5_CHUNK
    w = S5_CHUNK * S5_GROUP
    return pl.pallas_call(
        functools.partial(_s5_pack_kernel, nb=bsz, n_g=n_g),
        grid=(n_chunks,),
        in_specs=[pl.BlockSpec((bsz, S5_CHUNK, e), lambda i: (0, i, 1))],
        out_specs=pl.BlockSpec((n_g, bsz, w), lambda i: (0, i, 0)),
        out_shape=jax.ShapeDtypeStruct((n_g, n_chunks * bsz, w), BF16),
        scratch_shapes=[pltpu.VMEM((2, n_g * bsz, w // 2), F32)],
        compiler_params=_cparams(1),
    )(pz)


def _s5_kernel(u_ref, m_ref, q_ref, p_ref, a_ref, y_ref, s_ref, h_ref, *, nb, n_chunks, nc_ctx):
    ns = S5_STATE
    u = u_ref[0]
    s_ref[...] = jnp.dot(u, q_ref[0], preferred_element_type=F32)
    ar = a_ref[0, 0:1, :]
    ai = a_ref[0, 1:2, :]
    is_fwd = lax.broadcasted_iota(jnp.int32, (nb, 2 * ns), 1) < ns

    hr = hi = jnp.zeros((nb, 2 * ns), F32)
    for i in range(n_chunks):
        cb = nc_ctx - 1 - i if i < nc_ctx else n_chunks - 1 - (i - nc_ctx)
        rf, rb = i * nb, cb * nb
        h_ref[rf:rf + nb, 0:ns] = hr[:, 0:ns]
        h_ref[rb:rb + nb, ns:2 * ns] = hr[:, ns:2 * ns]
        h_ref[rf:rf + nb, 2 * ns:3 * ns] = hi[:, 0:ns]
        h_ref[rb:rb + nb, 3 * ns:4 * ns] = hi[:, ns:2 * ns]
        sf = s_ref[rf:rf + nb, :]
        sb = s_ref[rb:rb + nb, :]
        sr = jnp.where(is_fwd, sf[:, :2 * ns], sb[:, :2 * ns])
        si = jnp.where(is_fwd, sf[:, 2 * ns:], sb[:, 2 * ns:])
        hr, hi = ar * hr - ai * hi + sr, ar * hi + ai * hr + si
    y = jnp.dot(u, m_ref[0], preferred_element_type=F32)
    y = y + jnp.dot(h_ref[...].astype(BF16), p_ref[0], preferred_element_type=F32)
    y_ref[0] = y.astype(y_ref.dtype)


def _s5_scan(uf, m, q, p, a, nb, nc_ctx):
    n_g, rows, w = uf.shape
    n_chunks = rows // nb
    wspec = pl.BlockSpec((1, w, w), lambda g: (g, 0, 0))
    return pl.pallas_call(
        functools.partial(_s5_kernel, nb=nb, n_chunks=n_chunks, nc_ctx=nc_ctx),
        grid=(n_g,),
        in_specs=[pl.BlockSpec((1, rows, w), lambda g: (g, 0, 0)), wspec, wspec, wspec,
                  pl.BlockSpec((1, S5_A_ROWS, 2 * S5_STATE), lambda g: (g, 0, 0))],
        out_specs=pl.BlockSpec((1, rows, w), lambda g: (g, 0, 0)),
        out_shape=jax.ShapeDtypeStruct((n_g, rows, w), BF16),
        scratch_shapes=[pltpu.VMEM((rows, w), F32), pltpu.VMEM((rows, w), F32)],
        compiler_params=_cparams(1),
    )(uf, m, q, p, a)


def _s5_glu_kernel(y0_ref, yn_ref, w_ref, b_ref, o_ref, f_ref, g_ref, *, nb, n_g):
    i = pl.program_id(0)
    w = S5_CHUNK * S5_GROUP
    half = S5_GROUP // 2
    e = n_g * S5_GROUP

    def widen(y_ref):
        yv = y_ref[...].astype(F32).reshape(n_g * nb, w)
        for k in range(2):
            f_ref[k] = yv[:, k * (w // 2):(k + 1) * (w // 2)]

    def build(slot, batches):
        for b in batches:
            for k in range(2):
                t = f_ref[k, pl.ds(b, n_g, stride=nb), :].T
                g = 0.5 * t * (1.0 + jnp.tanh(math.sqrt(2.0 / math.pi) * (t + 0.044715 * t * t * t)))
                g = g.astype(BF16)
                for cc in range(half):
                    ch = k * half + cc
                    g_ref[slot, b * S5_CHUNK:(b + 1) * S5_CHUNK, ch * n_g:(ch + 1) * n_g] = (
                        g[cc * S5_CHUNK:(cc + 1) * S5_CHUNK, :])

    @pl.when(i == 0)
    def _():
        widen(y0_ref)
        build(0, range(nb))

    cur = i % 2
    nxt = 1 - cur
    n_pieces = GLU_PIECES
    tn = e // n_pieces
    per = nb // n_pieces
    widen(yn_ref)
    for p in range(n_pieces):
        cols = slice(p * tn, (p + 1) * tn)
        g = g_ref[cur]
        lin = jnp.dot(g, w_ref[:, cols], preferred_element_type=F32) + b_ref[0:1, cols]
        out = (g[:, cols].astype(F32) * _sigmoid(lin)).astype(o_ref.dtype)
        o_ref[:, :, cols] = out.reshape(nb, S5_CHUNK, tn)
        build(nxt, range(p * per, (p + 1) * per))


def _s5_glu(yf, w_bf, bias, bsz):
    n_g, rows, w = yf.shape
    e = n_g * S5_GROUP
    n_chunks = rows // bsz
    assert bsz % GLU_PIECES == 0 and e % GLU_PIECES == 0
    return pl.pallas_call(
        functools.partial(_s5_glu_kernel, nb=bsz, n_g=n_g),
        grid=(n_chunks,),
        in_specs=[pl.BlockSpec((n_g, bsz, w), lambda i: (0, 0, 0)),
                  pl.BlockSpec((n_g, bsz, w), lambda i: (0, jnp.minimum(i + 1, n_chunks - 1), 0)),
                  pl.BlockSpec((e, e), lambda i: (0, 0)),
                  pl.BlockSpec((2, e), lambda i: (0, 0))],
        out_specs=pl.BlockSpec((bsz, S5_CHUNK, e), lambda i: (0, i, 0)),
        out_shape=jax.ShapeDtypeStruct((bsz, n_chunks * S5_CHUNK, e), BF16),
        scratch_shapes=[pltpu.VMEM((2, n_g * bsz, w // 2), F32),
                        pltpu.VMEM((2, bsz * S5_CHUNK, e), BF16)],
        compiler_params=_cparams(1),
    )(yf, yf, w_bf, jnp.zeros((2, e), F32).at[0].set(bias.astype(F32)))


def _s5_mixer(pz, n_ctx, e, ops, glu_w_bf, glu_b):
    bsz = pz.shape[0]
    m, q, p, a = ops
    uf = _s5_pack(pz, e)
    yf = _s5_scan(uf, m, q, p, a, bsz, n_ctx // S5_CHUNK)
    return _s5_glu(yf, glu_w_bf, glu_b, bsz)


def _cumsum_rows_multi(xs, reverse):
    n = xs[0].shape[0]
    sub = 8
    n_slabs = n // sub
    row = lax.broadcasted_iota(jnp.int32, (sub, xs[0].shape[1]), 0)
    slabs = [[x[i * sub:(i + 1) * sub] for i in range(n_slabs)] for x in xs]
    s = 1
    while s < sub:
        nxt = []
        for arr, rev in zip(slabs, reverse):
            if rev:
                nxt.append([sl + jnp.where(row < sub - s, pltpu.roll(sl, sub - s, 0), 0.0) for sl in arr])
            else:
                nxt.append([sl + jnp.where(row >= s, pltpu.roll(sl, s, 0), 0.0) for sl in arr])
        slabs = nxt
        s *= 2
    out = []
    for arr, rev in zip(slabs, reverse):
        order = range(n_slabs - 1, -1, -1) if rev else range(n_slabs)
        edge = 0 if rev else sub - 1
        done = [None] * n_slabs
        carry = None
        for i in order:
            sl = arr[i] if carry is None else arr[i] + carry
            done[i] = sl
            carry = jnp.broadcast_to(sl[edge:edge + 1], sl.shape)
        out.append(jnp.concatenate(done, axis=0))
    return out


def _hgrn2_kernel(x_ref, lb_ref, ng_ref, o_ref,
                  g_ref, k_ref, qe_ref, ke_ref, qd_ref, kd_ref, ds_ref, dec_ref, st_ref, oi_ref,
                  *, layer, n_chunks, nc_ctx, n_tok):
    c = HG_CHUNK
    hd = HG_HEAD
    nt = (((1,), (1,)), ((), ()))
    tn = (((0,), (0,)), ((), ()))

    lbw = lb_ref[...].astype(F32)
    ew = jnp.exp(lbw - jnp.max(lbw, axis=0, keepdims=True))
    lb = jnp.sum(ew[1:layer + 1], axis=0, keepdims=True) / jnp.sum(ew, axis=0, keepdims=True)

    def gates(ci):
        r0 = pl.multiple_of(ci * c, c)
        for d in range(2):
            fl = x_ref[0, pl.ds(r0, c), (1 + d) * hd:(2 + d) * hd].astype(F32)
            f = lb + (1.0 - lb) * _sigmoid(fl)
            g_ref[d, pl.ds(r0, c), :] = jnp.log2(f)
            k_ref[d, pl.ds(r0, c), :] = 1.0 - f

    def cumdecay(cis):
        r0s = [pl.multiple_of(ci * c, c) for ci in cis]
        xs = [g_ref[d, pl.ds(r0, c), :] for r0 in r0s for d in range(2)]
        xs = _cumsum_rows_multi(xs, [False, True] * len(cis))
        k = 0
        for r0 in r0s:
            for d in range(2):
                g_ref[d, pl.ds(r0, c), :] = xs[k]
                k += 1

    def decayed(ci):
        r0 = pl.multiple_of(ci * c, c)
        q = x_ref[0, pl.ds(r0, c), 0:hd].astype(F32)
        for d in range(2):
            mid = c // 2 if d == 0 else c // 2 - 1
            last = c - 1 if d == 0 else 0
            b = g_ref[d, pl.ds(r0, c), :]
            ref = g_ref[d, pl.ds(r0 + mid, 1), :]
            b_last = g_ref[d, pl.ds(r0 + last, 1), :]
            qe = q * jnp.exp2(b - ref)
            ke = k_ref[d, pl.ds(r0, c), :] * jnp.exp2(ref - b)
            qe_ref[d, pl.ds(r0, c), :] = qe.astype(BF16)
            ke_ref[d, pl.ds(r0, c), :] = ke.astype(BF16)
            qd_ref[pl.ds(r0, c), d * hd:(d + 1) * hd] = (qe * jnp.exp2(ref)).astype(BF16)
            kd_ref[pl.ds(r0, c), d * hd:(d + 1) * hd] = (ke * jnp.exp2(b_last - ref)).astype(BF16)
            dec_ref[ci, :, d * hd:(d + 1) * hd] = jnp.exp2(b_last)

    def prepare(cis):
        for ci in cis:
            gates(ci)
        cumdecay(cis)
        for ci in cis:
            decayed(ci)

    ti = lax.broadcasted_iota(jnp.int32, (c, 2 * c), 0)
    si = lax.broadcasted_iota(jnp.int32, (c, 2 * c), 1)
    m_fwd = si <= ti
    m_bwd = si - c >= ti

    def logits(cis):
        r0s = [pl.multiple_of(ci * c, c) for ci in cis]
        vs = [x_ref[0, pl.ds(r0, c), 3 * hd:4 * hd] for r0 in r0s]
        aa = [lax.dot_general(
            jnp.concatenate([qe_ref[0, pl.ds(r0, c), :], qe_ref[1, pl.ds(r0, c), :]], axis=0),
            jnp.concatenate([ke_ref[0, pl.ds(r0, c), :], ke_ref[1, pl.ds(r0, c), :]], axis=0),
            nt, preferred_element_type=F32) for r0 in r0s]
        dss = [lax.dot_general(v, kd_ref[pl.ds(r0, c), :], tn, preferred_element_type=F32)
               for v, r0 in zip(vs, r0s)]
        return r0s, vs, aa, dss

    def values(cis, vs, aa, dss):
        atts = [jnp.where(m_fwd, a[:c], jnp.where(m_bwd, a[c:], 0.0)).astype(BF16) for a in aa]
        ois = [jnp.dot(att, jnp.concatenate([v, v], axis=0), preferred_element_type=F32)
               for att, v in zip(atts, vs)]
        for ci, ds in zip(cis, dss):
            ds_ref[ci] = ds
        return tuple(ois)

    def store_intra(cis, ois):
        for ci, oi in zip(cis, ois):
            oi_ref[pl.ds(pl.multiple_of(ci * c, c), c), :] = oi

    n_pairs = n_chunks // HG_PAIR

    def pair(t):
        return [t * HG_PAIR + u for u in range(HG_PAIR)]

    def stage(t, pending, with_next):
        cur = pair(t)
        _, vs, aa, dss = logits(cur)
        if pending is not None:
            store_intra(pair(t - 1), pending)
        if with_next:
            nxt = pair(t + 1)
            for ci in nxt:
                gates(ci)
            cumdecay(nxt)
        ois = values(cur, vs, aa, dss)
        if with_next:
            for ci in nxt:
                decayed(ci)
        return ois

    prepare(pair(0))
    pending = stage(0, None, True)
    for t in range(1, n_pairs - 1):
        pending = stage(t, pending, True)
    pending = stage(n_pairs - 1, pending, False)
    store_intra(pair(n_pairs - 1), pending)

    sf = sb = jnp.zeros((hd, hd), F32)
    for i in range(n_chunks):
        cb = nc_ctx - 1 - i if i < nc_ctx else n_chunks - 1 - (i - nc_ctx)
        st_ref[i, :, 0:hd] = sf.astype(BF16)
        st_ref[cb, :, hd:2 * hd] = sb.astype(BF16)
        sf = sf * dec_ref[i, :, 0:hd] + ds_ref[i, :, 0:hd]
        sb = sb * dec_ref[cb, :, hd:2 * hd] + ds_ref[cb, :, hd:2 * hd]

    def inter(cis):
        os_ = [lax.dot_general(qd_ref[ci * c:(ci + 1) * c, :], st_ref[ci], nt, preferred_element_type=F32)
               for ci in cis]
        for ci, o in zip(cis, os_):
            oi_ref[ci * c:(ci + 1) * c, :] += o

    gn = ng_ref[...].astype(F32)

    def head_norm(r0):
        o = oi_ref[r0:r0 + HG_NORM_ROWS, :]
        o = o * lax.rsqrt(jnp.mean(o * o, axis=-1, keepdims=True) + NORM_EPS) * gn
        o_ref[0, r0:r0 + HG_NORM_ROWS, :] = o.astype(o_ref.dtype)

    group_rows = HG_UNROLL_INTER * c
    n_groups = n_chunks // HG_UNROLL_INTER
    for gi in range(n_groups + 1):
        if gi < n_groups:
            inter(range(gi * HG_UNROLL_INTER, (gi + 1) * HG_UNROLL_INTER))
        if gi >= 1:
            for r0 in range((gi - 1) * group_rows, gi * group_rows, HG_NORM_ROWS):
                head_norm(r0)


def _hgrn2(pz, hg_lb, norm_g, layer, n_ctx, e):
    bsz, t, _ = pz.shape
    n_h = e // HG_HEAD
    n_chunks = t // HG_CHUNK
    depth = hg_lb.shape[0]
    hd = HG_HEAD
    assert n_chunks % HG_PAIR == 0 and n_chunks // HG_PAIR >= 3 and n_chunks % HG_UNROLL_INTER == 0
    assert t % HG_NORM_ROWS == 0

    return pl.pallas_call(
        functools.partial(_hgrn2_kernel, layer=layer, n_chunks=n_chunks, nc_ctx=n_ctx // HG_CHUNK, n_tok=t),
        grid=(bsz, n_h),
        in_specs=[pl.BlockSpec((1, t, 4 * hd), lambda b, h: (b, 0, e // (4 * hd) + h)),
                  pl.BlockSpec((depth, hd), lambda b, h: (0, h)),
                  pl.BlockSpec((1, hd), lambda b, h: (0, h))],
        out_specs=pl.BlockSpec((1, t, hd), lambda b, h: (b, 0, h)),
        out_shape=jax.ShapeDtypeStruct((bsz, t, e), BF16),
        scratch_shapes=[pltpu.VMEM((2, t, hd), F32),
                        pltpu.VMEM((2, t, hd), F32),
                        pltpu.VMEM((2, t, hd), BF16),
                        pltpu.VMEM((2, t, hd), BF16),
                        pltpu.VMEM((t, 2 * hd), BF16),
                        pltpu.VMEM((t, 2 * hd), BF16),
                        pltpu.VMEM((n_chunks, hd, 2 * hd), F32),
                        pltpu.VMEM((-(-n_chunks // 16) * 16, 1, 2 * hd), F32),
                        pltpu.VMEM((n_chunks, hd, 2 * hd), BF16),
                        pltpu.VMEM((t, hd), F32)],
        compiler_params=_cparams(2),
    )(pz, hg_lb.astype(F32), norm_g.astype(F32).reshape(1, e))


def _head_major(w, e):
    d = w.shape[0]
    n_h = e // HG_HEAD
    rest = w[:, e:].reshape(d, 4, n_h, HG_HEAD).transpose(0, 2, 1, 3).reshape(d, 4 * e)
    return jnp.concatenate([w[:, :e], rest], axis=1)


def _z_first(w, e):
    return jnp.concatenate([w[..., -e:], w[..., :-e]], axis=-1)


def kernel(x, c, ctx, c_ctx, ada_w, ada_b, ln_g, ln_b, w_out, attn_w_in, attn_sink, s5_w_in, s5_lam_re, s5_lam_im, s5_log_step, s5_b_re, s5_b_im, s5_c_re, s5_c_im, s5_d, s5_glu_w, s5_glu_b, hg_w_in, hg_lb, hg_norm_g):
    bsz, n_lat, d = x.shape
    n_ctx = ctx.shape[1]
    depth = ada_w.shape[0]
    e = w_out.shape[1]
    alpha = (2.0 * depth) ** 0.25
    t = n_ctx + n_lat
    tm = 768 if t % 768 == 0 else 256

    mod_rows = -(-(bsz + 1) // MOD_ROWS_PAD) * MOD_ROWS_PAD
    c_all = jnp.concatenate([c.astype(F32), c_ctx.astype(F32)[None],
                             jnp.zeros((mod_rows - bsz - 1, d), F32)], axis=0)
    mod = _ada(c_all, ada_w.astype(F32), ada_b.astype(F32))
    mod3 = jnp.pad(mod, ((0, 0), (0, 0), (0, d))).reshape(depth * mod_rows, 1, 4 * d)

    xs = (ctx.astype(F32), x.astype(F32))
    for i in range(depth):
        kind, j = i % N_MIXERS, i // N_MIXERS
        w_in = _z_first((attn_w_in, s5_w_in, hg_w_in)[kind][j].astype(BF16), e)
        w_o = w_out[i].astype(BF16)
        if kind == 1:
            w_in = jnp.concatenate([_cg_order(w_in[:, :e], 1), _cg_order(w_in[:, e:], 1)], axis=1)
            w_o = _cg_order(w_o, 0)
        elif kind == 2:
            w_in = _head_major(w_in, e)
        pz = _inproj(xs, mod3, i, mod_rows, w_in.astype(BF16), n_ctx, tn=INPROJ_TN)
        last = i == depth - 1
        if kind == 0:
            y = _attention(pz, attn_sink[j], n_ctx, e, ctx_out=not last)
        elif kind == 1:
            ops = _s5_operators(s5_lam_re[j], s5_lam_im[j], s5_log_step[j], s5_b_re[j], s5_b_im[j],
                                s5_c_re[j], s5_c_im[j], s5_d[j])
            glu_w = _cg_order(_cg_order(s5_glu_w[j].astype(BF16), 0), 1)
            y = _s5_mixer(pz, n_ctx, e, ops, glu_w, _cg_order(s5_glu_b[j], 0))
        else:
            y = _hgrn2(pz, hg_lb, hg_norm_g[j], i, n_ctx, e)
        if last and kind == 0:
            return _outproj(y, pz, xs, mod3, i, mod_rows, w_o.astype(BF16), ln_g[i].astype(F32),
                            ln_b[i].astype(F32), n_ctx, alpha, LAST_TM, latent_only=True).astype(x.dtype)
        xs = (_outproj(y, pz, xs, mod3, i, mod_rows, w_o.astype(BF16), ln_g[i].astype(F32),
                       ln_b[i].astype(F32), n_ctx, alpha, tm),)
    return xs[0][:, n_ctx:].astype(x.dtype)
```

```python
import functools
import math

import jax
import jax.numpy as jnp
from jax import lax
from jax.experimental import pallas as pl
from jax.experimental.pallas import tpu as pltpu

F32 = jnp.float32
BF16 = jnp.bfloat16

N_MIXERS = 3
HEAD_DIM = 128
Q_PER_KV = 4
WINDOW = 128
ATTN_BLOCK = 128
GRID_W = 64
ROPE_THETA = 10000.0
S5_GROUP = 16
S5_STATE = 64
S5_CHUNK = 16
HG_HEAD = 128
HG_CHUNK = 64
HG_UNROLL_INTER = 12
HG_PAIR = 4
HG_NORM_ROWS = 256
NORM_EPS = 1e-5
NEG_INF = -1e30
LOG2E = math.log2(math.e)
ATTN_ROWS = 32
MOD_ROWS_PAD = 8
INPROJ_TN = 1024
LAST_TM = 512
LN_ROWS = 4
S5_A_ROWS = 16
GLU_PIECES = 8
VMEM_LIMIT = 56 * 1024 * 1024


def _cparams(n_axes):
    return pltpu.CompilerParams(dimension_semantics=("arbitrary",) * n_axes,
                                vmem_limit_bytes=VMEM_LIMIT)


def _sigmoid(x):
    return 1.0 / (1.0 + jnp.exp(-x))


def _ada_kernel(c_ref, w_ref, b_ref, o_ref):
    cv = c_ref[...]
    s = cv * _sigmoid(cv)
    o_ref[0] = jnp.dot(s, w_ref[0], preferred_element_type=F32,
                       precision=lax.Precision.HIGHEST) + b_ref[0]


def _ada(c_all, ada_w, ada_b):
    depth, d, d3 = ada_w.shape
    rows = c_all.shape[0]
    return pl.pallas_call(
        _ada_kernel,
        grid=(depth, d3 // d),
        in_specs=[pl.BlockSpec((rows, d), lambda i, j: (0, 0)),
                  pl.BlockSpec((1, d, d), lambda i, j: (i, 0, j)),
                  pl.BlockSpec((1, 1, d), lambda i, j: (i, 0, j))],
        out_specs=pl.BlockSpec((1, rows, d), lambda i, j: (i, 0, j)),
        out_shape=jax.ShapeDtypeStruct((depth, rows, d3), F32),
        compiler_params=_cparams(2),
    )(c_all, ada_w, ada_b.reshape(depth, 1, d3))


def _inproj_kernel(*refs, n_ctx, d, split):
    if split:
        xc_ref, xl_ref, mb_ref, mc_ref, w_ref, o_ref, h_ref = refs
    else:
        x_ref, mb_ref, mc_ref, w_ref, o_ref, h_ref = refs

    @pl.when(pl.program_id(1) == 0)
    def _():
        mb = mb_ref[0]
        mc = mc_ref[0]
        xc = xc_ref[0] if split else x_ref[0, :n_ctx]
        xl = xl_ref[0] if split else x_ref[0, n_ctx:]
        h_ref[:n_ctx] = (xc * (1.0 + mc[:, d:2 * d]) + mc[:, :d]).astype(BF16)
        h_ref[n_ctx:] = (xl * (1.0 + mb[:, d:2 * d]) + mb[:, :d]).astype(BF16)

    o_ref[0] = jnp.dot(h_ref[...], w_ref[...], preferred_element_type=F32).astype(o_ref.dtype)


def _inproj(xs, mod3, layer, mod_rows, w_bf, n_ctx, tn):
    split = len(xs) == 2
    bsz, _, d = xs[0].shape
    t = sum(a.shape[1] for a in xs)
    n = w_bf.shape[1]
    base = layer * mod_rows
    return pl.pallas_call(
        functools.partial(_inproj_kernel, n_ctx=n_ctx, d=d, split=split),
        grid=(bsz, n // tn),
        in_specs=[pl.BlockSpec((1, a.shape[1], d), lambda b, j: (b, 0, 0)) for a in xs] + [
                  pl.BlockSpec((1, 1, mod3.shape[2]), lambda b, j: (base + b, 0, 0)),
                  pl.BlockSpec((1, 1, mod3.shape[2]), lambda b, j: (base + bsz, 0, 0)),
                  pl.BlockSpec((d, tn), lambda b, j: (0, j))],
        out_specs=pl.BlockSpec((1, t, tn), lambda b, j: (b, 0, j)),
        out_shape=jax.ShapeDtypeStruct((bsz, t, n), BF16),
        scratch_shapes=[pltpu.VMEM((t, d), BF16)],
        compiler_params=_cparams(2),
    )(*xs, mod3, mod3, w_bf)


def _outproj_kernel(y_ref, *refs, n_ctx, d, alpha, tm, row_off, n_z, n_x, with_ctx):
    z_refs, x_refs, rest = refs[:n_z], refs[n_z:n_z + n_x], refs[n_z + n_x:]
    if with_ctx:
        c_ref, mb_ref, mc_ref, w_ref, ln_ref, o_ref = rest
        first = jnp.where(pl.program_id(1) == 0, c_ref[0], x_refs[0][0])
        x_tile = jnp.concatenate([first] + [r[0] for r in x_refs[1:]], axis=0)
    else:
        mb_ref, mc_ref, w_ref, ln_ref, o_ref = rest
        x_tile = x_refs[0][0] if n_x == 1 else jnp.concatenate([r[0] for r in x_refs], axis=0)
    z = z_refs[0][0] if n_z == 1 else jnp.concatenate([r[0] for r in z_refs], axis=0)
    a = y_ref[0] * (z * _sigmoid(z))
    br = jnp.dot(a, w_ref[...], preferred_element_type=F32)
    rows = pl.program_id(1) * tm + row_off + lax.broadcasted_iota(jnp.int32, (tm, 1), 0)
    gate = jnp.where(rows < n_ctx, mc_ref[0][:, 2 * d:3 * d], mb_ref[0][:, 2 * d:3 * d])
    v = alpha * x_tile + gate * br
    mu = jnp.mean(v, axis=-1, keepdims=True)
    vc = v - mu
    var = jnp.mean(vc * vc, axis=-1, keepdims=True)
    o_ref[0] = vc * lax.rsqrt(var + NORM_EPS) * ln_ref[0:1, :] + ln_ref[1:2, :]


def _outproj(y, pz, xs, mod3, layer, mod_rows, w_bf, ln_g, ln_b, n_ctx, alpha, tm, latent_only=False):
    split = len(xs) == 2
    bsz, _, d = xs[0].shape
    t = sum(a.shape[1] for a in xs)
    e = w_bf.shape[0]
    base = layer * mod_rows
    ln = jnp.zeros((LN_ROWS, d), F32).at[0].set(ln_g).at[1].set(ln_b)
    n_pieces = tm // n_ctx
    assert not (split and latent_only) and (tm % n_ctx == 0 or not (split or latent_only))
    z_specs = [pl.BlockSpec((1, tm, e), lambda b, i: (b, i, 0))]
    z_args = [pz]
    if split:
        x_specs = [pl.BlockSpec((1, n_ctx, d), lambda b, i, p=p: (b, jnp.maximum(i * n_pieces + p - 1, 0), 0))
                   for p in range(n_pieces)] + [pl.BlockSpec((1, n_ctx, d), lambda b, i: (b, 0, 0))]
        x_args = [xs[1]] * n_pieces + [xs[0]]
    elif latent_only:
        z_specs = [pl.BlockSpec((1, n_ctx, e), lambda b, i, p=p: (b, i * n_pieces + p + 1, 0))
                   for p in range(n_pieces)]
        z_args = [pz] * n_pieces
        x_specs = [pl.BlockSpec((1, n_ctx, d), lambda b, i, p=p: (b, i * n_pieces + p + 1, 0))
                   for p in range(n_pieces)]
        x_args = [xs[0]] * n_pieces
    else:
        x_specs = [pl.BlockSpec((1, tm, d), lambda b, i: (b, i, 0))]
        x_args = [xs[0]]
    t_out = t - n_ctx if latent_only else t
    return pl.pallas_call(
        functools.partial(_outproj_kernel, n_ctx=n_ctx, d=d, alpha=alpha, tm=tm,
                          row_off=n_ctx if latent_only else 0, n_z=len(z_specs),
                          n_x=len(x_specs) - (1 if split else 0), with_ctx=split),
        grid=(bsz, t_out // tm),
        in_specs=[pl.BlockSpec((1, tm, e), lambda b, i: (b, i, 0))] + z_specs + x_specs + [
                  pl.BlockSpec((1, 1, mod3.shape[2]), lambda b, i: (base + b, 0, 0)),
                  pl.BlockSpec((1, 1, mod3.shape[2]), lambda b, i: (base + bsz, 0, 0)),
                  pl.BlockSpec((e, d), lambda b, i: (0, 0)),
                  pl.BlockSpec((LN_ROWS, d), lambda b, i: (0, 0))],
        out_specs=pl.BlockSpec((1, tm, d), lambda b, i: (b, i, 0)),
        out_shape=jax.ShapeDtypeStruct((bsz, t_out, d), F32),
        compiler_params=_cparams(2),
    )(y, *z_args, *x_args, mod3, mod3, w_bf, ln)


def _rope_tables(n_lat):
    pos = jnp.arange(n_lat, dtype=jnp.int32)
    row = (pos // GRID_W).astype(F32)
    col = (pos % GRID_W).astype(F32)
    nf = HEAD_DIM // 4
    inv_freq = jnp.power(ROPE_THETA, -jnp.arange(nf, dtype=F32) / nf)
    ang_r = row[:, None] * inv_freq[None, :]
    ang_c = col[:, None] * inv_freq[None, :]
    zeros = jnp.zeros_like(ang_r)
    cos = jnp.concatenate([jnp.cos(ang_r)] * 2 + [jnp.cos(ang_c)] * 2, axis=-1)
    s1 = jnp.concatenate([-jnp.sin(ang_r), zeros, -jnp.sin(ang_c), zeros], axis=-1)
    s2 = jnp.concatenate([zeros, jnp.sin(ang_r), zeros, jnp.sin(ang_c)], axis=-1)
    return cos, s1, s2


def _rope(x, cos, s1, s2):
    quarter = HEAD_DIM // 4
    return (x * cos + pltpu.roll(x, HEAD_DIM - quarter, 1) * s1
            + pltpu.roll(x, quarter, 1) * s2)


def _attn_kernel(sink_ref, q_ref, kv_ref, cq_ref, s1q_ref, s2q_ref, ck_ref, s1k_ref, s2k_ref,
                 o_ref, kr_ref, vx_ref, s_ref, p_ref, e_ref, bias_ref, *, n_ctx, n_lat, n_kv, first_block):
    j = pl.program_id(1) + first_block
    nb_ctx = n_ctx // ATTN_BLOCK
    kvw = n_kv * HEAD_DIM
    n_win = ATTN_BLOCK + 2 * WINDOW
    qscale = HEAD_DIM ** -0.5 * LOG2E
    rows_q = Q_PER_KV * ATTN_BLOCK
    vxw = 2 * HEAD_DIM
    nt = (((1,), (1,)), ((), ()))

    @pl.when(pl.program_id(1) == 0)
    def _():
        for h in range(n_kv):
            sl = slice(h * HEAD_DIM, (h + 1) * HEAD_DIM)
            k = kv_ref[0, n_ctx:, sl].astype(F32)
            kr_ref[:, sl] = _rope(k, ck_ref[...], s1k_ref[...], s2k_ref[...]).astype(BF16)
            vx_ref[:, h * vxw:h * vxw + HEAD_DIM] = kv_ref[0, :, kvw + h * HEAD_DIM:kvw + (h + 1) * HEAD_DIM]
            vx_ref[:, h * vxw + HEAD_DIM:(h + 1) * vxw] = jnp.ones((n_ctx + n_lat, HEAD_DIM), BF16)

    def key_cols(kparts):
        cols, off = [], 0
        for kp in kparts:
            cols.append((off, kp.shape[0]))
            off += kp.shape[0]
        return cols

    def logit_tasks(h, q_fn, kparts):
        box = {}

        def task(kp, o_, n):
            def run():
                if "q" not in box:
                    box["q"] = q_fn(h)
                s_ref[h % 2, :, o_:o_ + n] = lax.dot_general(box["q"], kp, nt, preferred_element_type=F32)
            return run

        return [task(kp, o_, n) for kp, (o_, n) in zip(kparts, key_cols(kparts))]

    def softmax_rows(h, cols, use_bias, blocks):
        for rb in blocks:
            r0 = rb * ATTN_ROWS
            sk = sink_ref[h * Q_PER_KV + r0 // ATTN_BLOCK] * LOG2E
            parts = []
            for idx, (o_, n) in enumerate(cols):
                s = s_ref[h % 2, r0:r0 + ATTN_ROWS, o_:o_ + n]
                if use_bias and idx == 0:
                    ql0 = r0 % ATTN_BLOCK
                    s = s + bias_ref[ql0:ql0 + ATTN_ROWS, :]
                parts.append(s)
            m = jnp.max(parts[0], axis=-1, keepdims=True)
            for s in parts[1:]:
                m = jnp.maximum(m, jnp.max(s, axis=-1, keepdims=True))
            m = jnp.maximum(m, sk)
            for (o_, n), s in zip(cols, parts):
                p_ref[h % 2, r0:r0 + ATTN_ROWS, o_:o_ + n] = jnp.exp2(s - m).astype(BF16)
            e_ref[h % 2, r0:r0 + ATTN_ROWS, :] = jnp.exp2(sk - m)

    def value_tasks(h, cols, vparts):
        box = {}

        def task(idx, o_, n, vp):
            def run():
                part = jnp.dot(p_ref[h % 2, :, o_:o_ + n], vp, preferred_element_type=F32)
                box["ox"] = part if idx == 0 else box["ox"] + part
                if idx == len(cols) - 1:
                    ox = box["ox"]
                    l = ox[:, HEAD_DIM:HEAD_DIM + 1] + e_ref[h % 2]
                    o = ox[:, :HEAD_DIM] * (1.0 / l)
                    for g in range(Q_PER_KV):
                        hq = h * Q_PER_KV + g
                        o_ref[0, :, hq * HEAD_DIM:(hq + 1) * HEAD_DIM] = (
                            o[g * ATTN_BLOCK:(g + 1) * ATTN_BLOCK].astype(o_ref.dtype))
            return run

        return [task(i, o_, n, vp) for i, ((o_, n), vp) in enumerate(zip(cols, vparts))]

    def attend_all(q_fn, k_fn, v_fn, use_bias):
        n_blocks = rows_q // ATTN_ROWS
        quarter = n_blocks // 4
        cols = key_cols(k_fn(0))
        for task in logit_tasks(0, q_fn, k_fn(0)):
            task()
        for h in range(n_kv + 1):
            side = []
            if h + 1 < n_kv:
                side += logit_tasks(h + 1, q_fn, k_fn(h + 1))
            if 1 <= h:
                side += value_tasks(h - 1, cols, v_fn(h - 1))
            for qi in range(4):
                if h < n_kv:
                    softmax_rows(h, cols, use_bias, range(qi * quarter, (qi + 1) * quarter))
                for task in side[qi::4]:
                    task()

    def q_head(hq):
        return q_ref[0, :, hq * HEAD_DIM:(hq + 1) * HEAD_DIM].astype(F32)

    def head_slice(h):
        return slice(h * HEAD_DIM, (h + 1) * HEAD_DIM)

    @pl.when(j < nb_ctx)
    def _():
        attend_all(
            lambda h: jnp.concatenate([(q_head(h * Q_PER_KV + g) * qscale).astype(BF16)
                                       for g in range(Q_PER_KV)], axis=0),
            lambda h: [kv_ref[0, :n_ctx, head_slice(h)]],
            lambda h: [vx_ref[:n_ctx, h * vxw:(h + 1) * vxw]], False)

    @pl.when(j >= nb_ctx)
    def _():
        jl = j - nb_ctx
        start = jnp.clip((jl - 1) * ATTN_BLOCK, 0, n_lat - n_win)
        start = pl.multiple_of(start, ATTN_BLOCK)
        delta = jl * ATTN_BLOCK - start
        ql = lax.broadcasted_iota(jnp.int32, (ATTN_BLOCK, n_win), 0)
        kl = lax.broadcasted_iota(jnp.int32, (ATTN_BLOCK, n_win), 1)
        bias_ref[...] = jnp.where(jnp.abs(delta + ql - kl) <= WINDOW, 0.0, NEG_INF)
        cq = cq_ref[...] * qscale
        s1q = s1q_ref[...] * qscale
        s2q = s2q_ref[...] * qscale
        attend_all(
            lambda h: jnp.concatenate([_rope(q_head(h * Q_PER_KV + g), cq, s1q, s2q).astype(BF16)
                                       for g in range(Q_PER_KV)], axis=0),
            lambda h: [kr_ref[pl.ds(start, n_win), head_slice(h)], kv_ref[0, :n_ctx, head_slice(h)]],
            lambda h: [vx_ref[pl.ds(n_ctx + start, n_win), h * vxw:(h + 1) * vxw],
                       vx_ref[:n_ctx, h * vxw:(h + 1) * vxw]], True)


def _attention(pz, sink, n_ctx, e, ctx_out):
    bsz, t, _ = pz.shape
    n_lat = t - n_ctx
    n_q = e // HEAD_DIM
    n_kv = n_q // Q_PER_KV
    kvw = n_kv * HEAD_DIM
    n_win = ATTN_BLOCK + 2 * WINDOW
    rows_q = Q_PER_KV * ATTN_BLOCK
    assert (2 * e) % (2 * kvw) == 0 and n_lat >= n_win
    cos, s1, s2 = _rope_tables(n_lat)
    nb_ctx = n_ctx // ATTN_BLOCK
    first = 0 if ctx_out else nb_ctx
    qtab = pl.BlockSpec((ATTN_BLOCK, HEAD_DIM), lambda b, j: (jnp.maximum(j + first - nb_ctx, 0), 0))
    ktab = pl.BlockSpec((n_lat, HEAD_DIM), lambda b, j: (0, 0))
    return pl.pallas_call(
        functools.partial(_attn_kernel, n_ctx=n_ctx, n_lat=n_lat, n_kv=n_kv, first_block=first),
        grid=(bsz, t // ATTN_BLOCK - first),
        in_specs=[pl.BlockSpec(memory_space=pltpu.SMEM),
                  pl.BlockSpec((1, ATTN_BLOCK, e), lambda b, j: (b, j + first, 1)),
                  pl.BlockSpec((1, t, 2 * kvw), lambda b, j: (b, 0, 2 * e // (2 * kvw))),
                  qtab, qtab, qtab, ktab, ktab, ktab],
        out_specs=pl.BlockSpec((1, ATTN_BLOCK, e), lambda b, j: (b, j, 0)),
        out_shape=jax.ShapeDtypeStruct((bsz, t - first * ATTN_BLOCK, e), BF16),
        scratch_shapes=[pltpu.VMEM((n_lat, kvw), BF16),
                        pltpu.VMEM((t, n_kv * 2 * HEAD_DIM), BF16),
                        pltpu.VMEM((2, rows_q, n_win + n_ctx), F32),
                        pltpu.VMEM((2, rows_q, n_win + n_ctx), BF16),
                        pltpu.VMEM((2, rows_q, 1), F32),
                        pltpu.VMEM((ATTN_BLOCK, n_win), F32)],
        compiler_params=_cparams(2),
    )(sink.astype(F32), pz, pz, cos, s1, s2, cos, s1, s2)


def _cg_order(w, axis):
    axis = axis % w.ndim
    n = w.shape[axis]
    shp = w.shape[:axis] + (n // S5_GROUP, S5_GROUP) + w.shape[axis + 1:]
    return jnp.swapaxes(w.reshape(shp), axis, axis + 1).reshape(w.shape)


def _s5_operators(lam_re, lam_im, log_step, b_re, b_im, c_re, c_im, d_skip):
    n_g = lam_re.shape[1]
    tc = S5_CHUNK
    dt = jnp.exp(log_step.astype(F32))[..., None]
    lr, li = lam_re.astype(F32) * dt, lam_im.astype(F32) * dt
    k = jnp.arange(tc + 1, dtype=F32)[:, None, None, None]
    mag = jnp.exp(lr[None] * k)
    pr, pi = mag * jnp.cos(li[None] * k), mag * jnp.sin(li[None] * k)
    ar1, ai1 = pr[1] - 1.0, pi[1]
    den = lam_re.astype(F32) ** 2 + lam_im.astype(F32) ** 2
    cr = (ar1 * lam_re + ai1 * lam_im) / den
    ci = (ai1 * lam_re - ar1 * lam_im) / den
    br, bi = b_re.astype(F32), b_im.astype(F32)
    bbr = cr[..., None] * br - ci[..., None] * bi
    bbi = cr[..., None] * bi + ci[..., None] * br
    ccr, cci = c_re.astype(F32), c_im.astype(F32)

    def kern(d):
        xr = pr[:tc, d][..., None] * bbr[d][None] - pi[:tc, d][..., None] * bbi[d][None]
        xi = pr[:tc, d][..., None] * bbi[d][None] + pi[:tc, d][..., None] * bbr[d][None]
        return (jnp.einsum('gnp,tgpm->tgnm', ccr[d], xr) - jnp.einsum('gnp,tgpm->tgnm', cci[d], xi))

    kf, kb = kern(0), kern(1)
    s_idx = jnp.arange(tc)[:, None]
    t_idx = jnp.arange(tc)[None, :]
    lag = t_idx - s_idx
    taus = jnp.arange(tc)[:, None, None]
    place_f = (lag[None] == taus).astype(F32)
    place_b = (-lag[None] == taus).astype(F32)
    m = (jnp.einsum('xgnm,xst->gmsnt', kf, place_f, precision=lax.Precision.HIGHEST)
         + jnp.einsum('xgnm,xst->gmsnt', kb, place_b, precision=lax.Precision.HIGHEST))
    eye = (jnp.eye(S5_GROUP, dtype=F32)[:, None, :, None] * jnp.eye(tc, dtype=F32)[None, :, None, :])
    m = m + eye[None] * d_skip.astype(F32).reshape(n_g, S5_GROUP, 1, 1, 1)
    m = m.reshape(n_g, tc * S5_GROUP, tc * S5_GROUP)

    def qpart(d, pw_r, pw_i):
        qr = pw_r[..., None] * bbr[d][None] - pw_i[..., None] * bbi[d][None]
        qi = pw_r[..., None] * bbi[d][None] + pw_i[..., None] * bbr[d][None]
        return qr.transpose(1, 3, 0, 2), qi.transpose(1, 3, 0, 2)

    qfr, qfi = qpart(0, pr[tc - 1 - jnp.arange(tc), 0], pi[tc - 1 - jnp.arange(tc), 0])
    qbr, qbi = qpart(1, pr[jnp.arange(tc), 1], pi[jnp.arange(tc), 1])
    q = jnp.concatenate([qfr, qbr, qfi, qbi], axis=-1).reshape(n_g, tc * S5_GROUP, 4 * S5_STATE)

    def ppart(d, pw_r, pw_i):
        xr = ccr[d][None] * pw_r[:, :, None, :] - cci[d][None] * pw_i[:, :, None, :]
        xi = ccr[d][None] * pw_i[:, :, None, :] + cci[d][None] * pw_r[:, :, None, :]
        return xr.transpose(1, 3, 2, 0), (-xi).transpose(1, 3, 2, 0)

    pfr, pfi = ppart(0, pr[1 + jnp.arange(tc), 0], pi[1 + jnp.arange(tc), 0])
    pbr, pbi = ppart(1, pr[tc - jnp.arange(tc), 1], pi[tc - jnp.arange(tc), 1])
    p = jnp.concatenate([pfr, pbr, pfi, pbi], axis=1).reshape(n_g, 4 * S5_STATE, tc * S5_GROUP)
    a = jnp.stack([jnp.concatenate([pr[tc, 0], pr[tc, 1]], axis=-1),
                   jnp.concatenate([pi[tc, 0], pi[tc, 1]], axis=-1)], axis=1)
    a = jnp.pad(a, ((0, 0), (0, S5_A_ROWS - 2), (0, 0)))
    return m.astype(BF16), q.astype(BF16), p.astype(BF16), a


def _s5_pack_kernel(x_ref, o_ref, f_ref, *, nb, n_g):
    half = S5_GROUP // 2
    for b in range(nb):
        xb = x_ref[b].astype(F32)
        for k in range(2):
            a = jnp.concatenate([xb[:, cc * n_g:(cc + 1) * n_g]
                                 for cc in range(k * half, (k + 1) * half)], axis=0)
            f_ref[k, pl.ds(b, n_g, stride=nb), :] = a.T
    full = jnp.concatenate([f_ref[0], f_ref[1]], axis=1)
    o_ref[...] = full.reshape(n_g, nb, S5_CHUNK * S5_GROUP).astype(o_ref.dtype)


def _s5_pack(pz, e):
    bsz, t, _ = pz.shape
    n_g = e // S5_GROUP
    n_chunks = t // S5_CHUNK
    w = S5_CHUNK * S5_GROUP
    return pl.pallas_call(
        functools.partial(_s5_pack_kernel, nb=bsz, n_g=n_g),
        grid=(n_chunks,),
        in_specs=[pl.BlockSpec((bsz, S5_CHUNK, e), lambda i: (0, i, 1))],
        out_specs=pl.BlockSpec((n_g, bsz, w), lambda i: (0, i, 0)),
        out_shape=jax.ShapeDtypeStruct((n_g, n_chunks * bsz, w), BF16),
        scratch_shapes=[pltpu.VMEM((2, n_g * bsz, w // 2), F32)],
        compiler_params=_cparams(1),
    )(pz)


def _s5_kernel(u_ref, m_ref, q_ref, p_ref, a_ref, y_ref, s_ref, h_ref, *, nb, n_chunks, nc_ctx):
    ns = S5_STATE
    u = u_ref[0]
    s_ref[...] = jnp.dot(u, q_ref[0], preferred_element_type=F32)
    ar = a_ref[0, 0:1, :]
    ai = a_ref[0, 1:2, :]
    is_fwd = lax.broadcasted_iota(jnp.int32, (nb, 2 * ns), 1) < ns

    hr = hi = jnp.zeros((nb, 2 * ns), F32)
    for i in range(n_chunks):
        cb = nc_ctx - 1 - i if i < nc_ctx else n_chunks - 1 - (i - nc_ctx)
        rf, rb = i * nb, cb * nb
        h_ref[rf:rf + nb, 0:ns] = hr[:, 0:ns]
        h_ref[rb:rb + nb, ns:2 * ns] = hr[:, ns:2 * ns]
        h_ref[rf:rf + nb, 2 * ns:3 * ns] = hi[:, 0:ns]
        h_ref[rb:rb + nb, 3 * ns:4 * ns] = hi[:, ns:2 * ns]
        sf = s_ref[rf:rf + nb, :]
        sb = s_ref[rb:rb + nb, :]
        sr = jnp.where(is_fwd, sf[:, :2 * ns], sb[:, :2 * ns])
        si = jnp.where(is_fwd, sf[:, 2 * ns:], sb[:, 2 * ns:])
        hr, hi = ar * hr - ai * hi + sr, ar * hi + ai * hr + si
    y = jnp.dot(u, m_ref[0], preferred_element_type=F32)
    y = y + jnp.dot(h_ref[...].astype(BF16), p_ref[0], preferred_element_type=F32)
    y_ref[0] = y.astype(y_ref.dtype)


def _s5_scan(uf, m, q, p, a, nb, nc_ctx):
    n_g, rows, w = uf.shape
    n_chunks = rows // nb
    wspec = pl.BlockSpec((1, w, w), lambda g: (g, 0, 0))
    return pl.pallas_call(
        functools.partial(_s5_kernel, nb=nb, n_chunks=n_chunks, nc_ctx=nc_ctx),
        grid=(n_g,),
        in_specs=[pl.BlockSpec((1, rows, w), lambda g: (g, 0, 0)), wspec, wspec, wspec,
                  pl.BlockSpec((1, S5_A_ROWS, 2 * S5_STATE), lambda g: (g, 0, 0))],
        out_specs=pl.BlockSpec((1, rows, w), lambda g: (g, 0, 0)),
        out_shape=jax.ShapeDtypeStruct((n_g, rows, w), BF16),
        scratch_shapes=[pltpu.VMEM((rows, w), F32), pltpu.VMEM((rows, w), F32)],
        compiler_params=_cparams(1),
    )(uf, m, q, p, a)


def _s5_glu_kernel(y0_ref, yn_ref, w_ref, b_ref, o_ref, f_ref, g_ref, *, nb, n_g):
    i = pl.program_id(0)
    w = S5_CHUNK * S5_GROUP
    half = S5_GROUP // 2
    e = n_g * S5_GROUP

    def widen(y_ref):
        yv = y_ref[...].astype(F32).reshape(n_g * nb, w)
        for k in range(2):
            f_ref[k] = yv[:, k * (w // 2):(k + 1) * (w // 2)]

    def build(slot, batches):
        for b in batches:
            for k in range(2):
                t = f_ref[k, pl.ds(b, n_g, stride=nb), :].T
                g = 0.5 * t * (1.0 + jnp.tanh(math.sqrt(2.0 / math.pi) * (t + 0.044715 * t * t * t)))
                g = g.astype(BF16)
                for cc in range(half):
                    ch = k * half + cc
                    g_ref[slot, b * S5_CHUNK:(b + 1) * S5_CHUNK, ch * n_g:(ch + 1) * n_g] = (
                        g[cc * S5_CHUNK:(cc + 1) * S5_CHUNK, :])

    @pl.when(i == 0)
    def _():
        widen(y0_ref)
        build(0, range(nb))

    cur = i % 2
    nxt = 1 - cur
    n_pieces = GLU_PIECES
    tn = e // n_pieces
    per = nb // n_pieces
    widen(yn_ref)
    for p in range(n_pieces):
        cols = slice(p * tn, (p + 1) * tn)
        g = g_ref[cur]
        lin = jnp.dot(g, w_ref[:, cols], preferred_element_type=F32) + b_ref[0:1, cols]
        out = (g[:, cols].astype(F32) * _sigmoid(lin)).astype(o_ref.dtype)
        o_ref[:, :, cols] = out.reshape(nb, S5_CHUNK, tn)
        build(nxt, range(p * per, (p + 1) * per))


def _s5_glu(yf, w_bf, bias, bsz):
    n_g, rows, w = yf.shape
    e = n_g * S5_GROUP
    n_chunks = rows // bsz
    assert bsz % GLU_PIECES == 0 and e % GLU_PIECES == 0
    return pl.pallas_call(
        functools.partial(_s5_glu_kernel, nb=bsz, n_g=n_g),
        grid=(n_chunks,),
        in_specs=[pl.BlockSpec((n_g, bsz, w), lambda i: (0, 0, 0)),
                  pl.BlockSpec((n_g, bsz, w), lambda i: (0, jnp.minimum(i + 1, n_chunks - 1), 0)),
                  pl.BlockSpec((e, e), lambda i: (0, 0)),
                  pl.BlockSpec((2, e), lambda i: (0, 0))],
        out_specs=pl.BlockSpec((bsz, S5_CHUNK, e), lambda i: (0, i, 0)),
        out_shape=jax.ShapeDtypeStruct((bsz, n_chunks * S5_CHUNK, e), BF16),
        scratch_shapes=[pltpu.VMEM((2, n_g * bsz, w // 2), F32),
                        pltpu.VMEM((2, bsz * S5_CHUNK, e), BF16)],
        compiler_params=_cparams(1),
    )(yf, yf, w_bf, jnp.zeros((2, e), F32).at[0].set(bias.astype(F32)))


def _s5_mixer(pz, n_ctx, e, ops, glu_w_bf, glu_b):
    bsz = pz.shape[0]
    m, q, p, a = ops
    uf = _s5_pack(pz, e)
    yf = _s5_scan(uf, m, q, p, a, bsz, n_ctx // S5_CHUNK)
    return _s5_glu(yf, glu_w_bf, glu_b, bsz)


def _cumsum_rows_multi(xs, reverse):
    n = xs[0].shape[0]
    sub = 8
    n_slabs = n // sub
    row = lax.broadcasted_iota(jnp.int32, (sub, xs[0].shape[1]), 0)
    slabs = [[x[i * sub:(i + 1) * sub] for i in range(n_slabs)] for x in xs]
    s = 1
    while s < sub:
        nxt = []
        for arr, rev in zip(slabs, reverse):
            if rev:
                nxt.append([sl + jnp.where(row < sub - s, pltpu.roll(sl, sub - s, 0), 0.0) for sl in arr])
            else:
                nxt.append([sl + jnp.where(row >= s, pltpu.roll(sl, s, 0), 0.0) for sl in arr])
        slabs = nxt
        s *= 2
    out = []
    for arr, rev in zip(slabs, reverse):
        order = range(n_slabs - 1, -1, -1) if rev else range(n_slabs)
        edge = 0 if rev else sub - 1
        done = [None] * n_slabs
        carry = None
        for i in order:
            sl = arr[i] if carry is None else arr[i] + carry
            done[i] = sl
            carry = jnp.broadcast_to(sl[edge:edge + 1], sl.shape)
        out.append(jnp.concatenate(done, axis=0))
    return out


def _hgrn2_kernel(x_ref, lb_ref, ng_ref, o_ref,
                  g_ref, k_ref, qe_ref, ke_ref, qd_ref, kd_ref, ds_ref, dec_ref, st_ref, oi_ref,
                  *, layer, n_chunks, nc_ctx, n_tok):
    c = HG_CHUNK
    hd = HG_HEAD
    nt = (((1,), (1,)), ((), ()))
    tn = (((0,), (0,)), ((), ()))

    lbw = lb_ref[...].astype(F32)
    ew = jnp.exp(lbw - jnp.max(lbw, axis=0, keepdims=True))
    lb = jnp.sum(ew[1:layer + 1], axis=0, keepdims=True) / jnp.sum(ew, axis=0, keepdims=True)

    def gates(ci):
        r0 = pl.multiple_of(ci * c, c)
        for d in range(2):
            fl = x_ref[0, pl.ds(r0, c), (1 + d) * hd:(2 + d) * hd].astype(F32)
            f = lb + (1.0 - lb) * _sigmoid(fl)
            g_ref[d, pl.ds(r0, c), :] = jnp.log2(f)
            k_ref[d, pl.ds(r0, c), :] = 1.0 - f

    def cumdecay(cis):
        r0s = [pl.multiple_of(ci * c, c) for ci in cis]
        xs = [g_ref[d, pl.ds(r0, c), :] for r0 in r0s for d in range(2)]
        xs = _cumsum_rows_multi(xs, [False, True] * len(cis))
        k = 0
        for r0 in r0s:
            for d in range(2):
                g_ref[d, pl.ds(r0, c), :] = xs[k]
                k += 1

    def decayed(ci):
        r0 = pl.multiple_of(ci * c, c)
        q = x_ref[0, pl.ds(r0, c), 0:hd].astype(F32)
        for d in range(2):
            mid = c // 2 if d == 0 else c // 2 - 1
            last = c - 1 if d == 0 else 0
            b = g_ref[d, pl.ds(r0, c), :]
            ref = g_ref[d, pl.ds(r0 + mid, 1), :]
            b_last = g_ref[d, pl.ds(r0 + last, 1), :]
            qe = q * jnp.exp2(b - ref)
            ke = k_ref[d, pl.ds(r0, c), :] * jnp.exp2(ref - b)
            qe_ref[d, pl.ds(r0, c), :] = qe.astype(BF16)
            ke_ref[d, pl.ds(r0, c), :] = ke.astype(BF16)
            qd_ref[pl.ds(r0, c), d * hd:(d + 1) * hd] = (qe * jnp.exp2(ref)).astype(BF16)
            kd_ref[pl.ds(r0, c), d * hd:(d + 1) * hd] = (ke * jnp.exp2(b_last - ref)).astype(BF16)
            dec_ref[ci, :, d * hd:(d + 1) * hd] = jnp.exp2(b_last)

    def prepare(cis):
        for ci in cis:
            gates(ci)
        cumdecay(cis)
        for ci in cis:
            decayed(ci)

    ti = lax.broadcasted_iota(jnp.int32, (c, 2 * c), 0)
    si = lax.broadcasted_iota(jnp.int32, (c, 2 * c), 1)
    m_fwd = si <= ti
    m_bwd = si - c >= ti

    def logits(cis):
        r0s = [pl.multiple_of(ci * c, c) for ci in cis]
        vs = [x_ref[0, pl.ds(r0, c), 3 * hd:4 * hd] for r0 in r0s]
        aa = [lax.dot_general(
            jnp.concatenate([qe_ref[0, pl.ds(r0, c), :], qe_ref[1, pl.ds(r0, c), :]], axis=0),
            jnp.concatenate([ke_ref[0, pl.ds(r0, c), :], ke_ref[1, pl.ds(r0, c), :]], axis=0),
            nt, preferred_element_type=F32) for r0 in r0s]
        dss = [lax.dot_general(v, kd_ref[pl.ds(r0, c), :], tn, preferred_element_type=F32)
               for v, r0 in zip(vs, r0s)]
        return r0s, vs, aa, dss

    def values(cis, vs, aa, dss):
        atts = [jnp.where(m_fwd, a[:c], jnp.where(m_bwd, a[c:], 0.0)).astype(BF16) for a in aa]
        ois = [jnp.dot(att, jnp.concatenate([v, v], axis=0), preferred_element_type=F32)
               for att, v in zip(atts, vs)]
        for ci, ds in zip(cis, dss):
            ds_ref[ci] = ds
        return tuple(ois)

    def store_intra(cis, ois):
        for ci, oi in zip(cis, ois):
            oi_ref[pl.ds(pl.multiple_of(ci * c, c), c), :] = oi

    n_pairs = n_chunks // HG_PAIR

    def pair(t):
        return [t * HG_PAIR + u for u in range(HG_PAIR)]

    def stage(t, pending, with_next):
        cur = pair(t)
        _, vs, aa, dss = logits(cur)
        if pending is not None:
            store_intra(pair(t - 1), pending)
        if with_next:
            nxt = pair(t + 1)
            for ci in nxt:
                gates(ci)
            cumdecay(nxt)
        ois = values(cur, vs, aa, dss)
        if with_next:
            for ci in nxt:
                decayed(ci)
        return ois

    prepare(pair(0))
    pending = stage(0, None, True)
    for t in range(1, n_pairs - 1):
        pending = stage(t, pending, True)
    pending = stage(n_pairs - 1, pending, False)
    store_intra(pair(n_pairs - 1), pending)

    sf = sb = jnp.zeros((hd, hd), F32)
    for i in range(n_chunks):
        cb = nc_ctx - 1 - i if i < nc_ctx else n_chunks - 1 - (i - nc_ctx)
        st_ref[i, :, 0:hd] = sf.astype(BF16)
        st_ref[cb, :, hd:2 * hd] = sb.astype(BF16)
        sf = sf * dec_ref[i, :, 0:hd] + ds_ref[i, :, 0:hd]
        sb = sb * dec_ref[cb, :, hd:2 * hd] + ds_ref[cb, :, hd:2 * hd]

    def inter(cis):
        os_ = [lax.dot_general(qd_ref[ci * c:(ci + 1) * c, :], st_ref[ci], nt, preferred_element_type=F32)
               for ci in cis]
        for ci, o in zip(cis, os_):
            oi_ref[ci * c:(ci + 1) * c, :] += o

    gn = ng_ref[...].astype(F32)

    def head_norm(r0):
        o = oi_ref[r0:r0 + HG_NORM_ROWS, :]
        o = o * lax.rsqrt(jnp.mean(o * o, axis=-1, keepdims=True) + NORM_EPS) * gn
        o_ref[0, r0:r0 + HG_NORM_ROWS, :] = o.astype(o_ref.dtype)

    group_rows = HG_UNROLL_INTER * c
    n_groups = n_chunks // HG_UNROLL_INTER
    for gi in range(n_groups + 1):
        if gi < n_groups:
            inter(range(gi * HG_UNROLL_INTER, (gi + 1) * HG_UNROLL_INTER))
        if gi >= 1:
            for r0 in range((gi - 1) * group_rows, gi * group_rows, HG_NORM_ROWS):
                head_norm(r0)


def _hgrn2(pz, hg_lb, norm_g, layer, n_ctx, e):
    bsz, t, _ = pz.shape
    n_h = e // HG_HEAD
    n_chunks = t // HG_CHUNK
    depth = hg_lb.shape[0]
    hd = HG_HEAD
    assert n_chunks % HG_PAIR == 0 and n_chunks // HG_PAIR >= 3 and n_chunks % HG_UNROLL_INTER == 0
    assert t % HG_NORM_ROWS == 0

    return pl.pallas_call(
        functools.partial(_hgrn2_kernel, layer=layer, n_chunks=n_chunks, nc_ctx=n_ctx // HG_CHUNK, n_tok=t),
        grid=(bsz, n_h),
        in_specs=[pl.BlockSpec((1, t, 4 * hd), lambda b, h: (b, 0, e // (4 * hd) + h)),
                  pl.BlockSpec((depth, hd), lambda b, h: (0, h)),
                  pl.BlockSpec((1, hd), lambda b, h: (0, h))],
        out_specs=pl.BlockSpec((1, t, hd), lambda b, h: (b, 0, h)),
        out_shape=jax.ShapeDtypeStruct((bsz, t, e), BF16),
        scratch_shapes=[pltpu.VMEM((2, t, hd), F32),
                        pltpu.VMEM((2, t, hd), F32),
                        pltpu.VMEM((2, t, hd), BF16),
                        pltpu.VMEM((2, t, hd), BF16),
                        pltpu.VMEM((t, 2 * hd), BF16),
                        pltpu.VMEM((t, 2 * hd), BF16),
                        pltpu.VMEM((n_chunks, hd, 2 * hd), F32),
                        pltpu.VMEM((-(-n_chunks // 16) * 16, 1, 2 * hd), F32),
                        pltpu.VMEM((n_chunks, hd, 2 * hd), BF16),
                        pltpu.VMEM((t, hd), F32)],
        compiler_params=_cparams(2),
    )(pz, hg_lb.astype(F32), norm_g.astype(F32).reshape(1, e))


def _head_major(w, e):
    d = w.shape[0]
    n_h = e // HG_HEAD
    rest = w[:, e:].reshape(d, 4, n_h, HG_HEAD).transpose(0, 2, 1, 3).reshape(d, 4 * e)
    return jnp.concatenate([w[:, :e], rest], axis=1)


def _z_first(w, e):
    return jnp.concatenate([w[..., -e:], w[..., :-e]], axis=-1)


def kernel(x, c, ctx, c_ctx, ada_w, ada_b, ln_g, ln_b, w_out, attn_w_in, attn_sink, s5_w_in, s5_lam_re, s5_lam_im, s5_log_step, s5_b_re, s5_b_im, s5_c_re, s5_c_im, s5_d, s5_glu_w, s5_glu_b, hg_w_in, hg_lb, hg_norm_g):
    bsz, n_lat, d = x.shape
    n_ctx = ctx.shape[1]
    depth = ada_w.shape[0]
    e = w_out.shape[1]
    alpha = (2.0 * depth) ** 0.25
    t = n_ctx + n_lat
    tm = 768 if t % 768 == 0 else 256

    mod_rows = -(-(bsz + 1) // MOD_ROWS_PAD) * MOD_ROWS_PAD
    c_all = jnp.concatenate([c.astype(F32), c_ctx.astype(F32)[None],
                             jnp.zeros((mod_rows - bsz - 1, d), F32)], axis=0)
    mod = _ada(c_all, ada_w.astype(F32), ada_b.astype(F32))
    mod3 = jnp.pad(mod, ((0, 0), (0, 0), (0, d))).reshape(depth * mod_rows, 1, 4 * d)

    xs = (ctx.astype(F32), x.astype(F32))
    for i in range(depth):
        kind, j = i % N_MIXERS, i // N_MIXERS
        w_in = _z_first((attn_w_in, s5_w_in, hg_w_in)[kind][j].astype(BF16), e)
        w_o = w_out[i].astype(BF16)
        if kind == 1:
            w_in = jnp.concatenate([_cg_order(w_in[:, :e], 1), _cg_order(w_in[:, e:], 1)], axis=1)
            w_o = _cg_order(w_o, 0)
        elif kind == 2:
            w_in = _head_major(w_in, e)
        pz = _inproj(xs, mod3, i, mod_rows, w_in.astype(BF16), n_ctx, tn=INPROJ_TN)
        last = i == depth - 1
        if kind == 0:
            y = _attention(pz, attn_sink[j], n_ctx, e, ctx_out=not last)
        elif kind == 1:
            ops = _s5_operators(s5_lam_re[j], s5_lam_im[j], s5_log_step[j], s5_b_re[j], s5_b_im[j],
                                s5_c_re[j], s5_c_im[j], s5_d[j])
            glu_w = _cg_order(_cg_order(s5_glu_w[j].astype(BF16), 0), 1)
            y = _s5_mixer(pz, n_ctx, e, ops, glu_w, _cg_order(s5_glu_b[j], 0))
        else:
            y = _hgrn2(pz, hg_lb, hg_norm_g[j], i, n_ctx, e)
        if last and kind == 0:
            return _outproj(y, pz, xs, mod3, i, mod_rows, w_o.astype(BF16), ln_g[i].astype(F32),
                            ln_b[i].astype(F32), n_ctx, alpha, LAST_TM, latent_only=True).astype(x.dtype)
        xs = (_outproj(y, pz, xs, mod3, i, mod_rows, w_o.astype(BF16), ln_g[i].astype(F32),
                       ln_b[i].astype(F32), n_ctx, alpha, tm),)
    return xs[0][:, n_ctx:].astype(x.dtype)
```
